```python
import math
import jax
import jax.numpy as jnp
from jax import lax
import numpy as np

D_MODEL = 1024
BATCH = 8
SEQ = 8192
DEPTH = 1

CHUNK = 64
GMLP_GROUPS = 8
GMLP_GROUP_DIM = 64
GMLP_WIDTH = GMLP_GROUPS * GMLP_GROUP_DIM
GMLP_BLOCK = 128
DIFF_HEADS = 8
DIFF_QK_DIM = 64
DIFF_V_DIM = 2 * DIFF_QK_DIM
DIFF_QK_WIDTH = DIFF_HEADS * 2 * DIFF_QK_DIM
DIFF_WIDTH = DIFF_HEADS * DIFF_V_DIM
Q_BLOCK = 128
N_GROUPS = 4
EXPERTS_PER_GROUP = 8
N_EXPERTS = N_GROUPS * EXPERTS_PER_GROUP
EXPERT_TOPK = 2
EXPERT_HIDDEN = 512
DISPATCH_BLOCK = 256
ALPHA = (2.0 * DEPTH) ** 0.25
BETA = (8.0 * DEPTH) ** -0.25
LN_EPS = 1e-5
RMS_EPS = 1e-5
OFF_U = 0
OFF_V = OFF_U + GMLP_WIDTH
OFF_Q = OFF_V + GMLP_WIDTH
OFF_K = OFF_Q + DIFF_QK_WIDTH
OFF_VAL = OFF_K + DIFF_QK_WIDTH
OFF_GA = OFF_VAL + DIFF_WIDTH
OFF_GB = OFF_GA + D_MODEL
IN_WIDTH = OFF_GB + D_MODEL

kernel_name = 'hybrid_gmlp_diffattn_hmoe_deepnorm'


def _layer_norm(x, g, b):
    xf = x.astype(jnp.float32)
    mu = jnp.mean(xf, axis=-1, keepdims=True)
    var = jnp.mean(jnp.square(xf - mu), axis=-1, keepdims=True)
    return ((xf - mu) * lax.rsqrt(var + LN_EPS)).astype(x.dtype) * g + b


def _spatial_gating(u, v, ln_g, ln_b, w_s, b_s):
    B, S, _ = u.shape
    n_blk = S // GMLP_BLOCK
    v = _layer_norm(v, ln_g, ln_b)
    v = v.reshape(B, n_blk, GMLP_BLOCK, GMLP_GROUPS, GMLP_GROUP_DIM)
    chunk_id = jnp.arange(GMLP_BLOCK) // CHUNK
    causal = chunk_id[:, None] >= chunk_id[None, :]
    w = jnp.where(causal[None], w_s, 0)
    mixed = jnp.einsum('gts,bnsgc->bntgc', w, v) + b_s.T[:, :, None]
    return u * mixed.reshape(B, S, GMLP_WIDTH)


def _diff_attention(q, k, v, lam, subln_g, lambda_init):
    B, S = q.shape[0], q.shape[1]
    n_blk = S // Q_BLOCK
    q = jnp.transpose(q * (DIFF_QK_DIM ** -0.5), (0, 2, 3, 1, 4))
    k = jnp.transpose(k, (0, 2, 3, 1, 4))
    v = jnp.transpose(v, (0, 2, 1, 3))
    q_blocks = jnp.moveaxis(q.reshape(B, DIFF_HEADS, 2, n_blk, Q_BLOCK, DIFF_QK_DIM), 3, 0)
    slopes = 2.0 ** (-8.0 * jnp.arange(1, DIFF_HEADS + 1, dtype=jnp.float32) / DIFF_HEADS)
    key_pos = jnp.arange(S)
    lam = lam.astype(jnp.float32)

    def one_block(args):
        q_blk, i = args
        q_pos = i * Q_BLOCK + jnp.arange(Q_BLOCK)
        visible = (key_pos[None, :] // CHUNK) <= (q_pos[:, None] // CHUNK)
        dist = jnp.abs(q_pos[:, None] - key_pos[None, :]).astype(jnp.float32)
        alibi = -slopes[:, None, None] * dist
        s = jnp.einsum('bhcqd,bhckd->bhcqk', q_blk, k).astype(jnp.float32) + alibi[None, :, None]
        s = jnp.where(visible, s, -jnp.inf)
        p = jax.nn.softmax(s, axis=-1)
        a = p[:, :, 0] - lam * p[:, :, 1]
        return jnp.einsum('bhqk,bhkd->bhqd', a.astype(v.dtype), v)

    out = lax.map(one_block, (q_blocks, jnp.arange(n_blk)))
    out = jnp.transpose(out, (1, 0, 3, 2, 4)).reshape(B, S, DIFF_HEADS, DIFF_V_DIM)
    of = out.astype(jnp.float32)
    of = of * lax.rsqrt(jnp.mean(jnp.square(of), axis=-1, keepdims=True) + RMS_EPS)
    out = of.astype(v.dtype) * subln_g * (1.0 - lambda_init)
    return out.reshape(B, S, DIFF_WIDTH)


def _token_mixer(h, layer, w_in, b_in, sg_ln_g, sg_ln_b, sg_w, sg_b, w_branch_a,
                 lam_q1, lam_k1, lam_q2, lam_k2, subln_g, w_branch_b, w_out):
    B, S, _ = h.shape

    def proj(lo, hi):
        return h @ w_in[:, lo:hi] + b_in[lo:hi]

    u = jax.nn.gelu(proj(OFF_U, OFF_V), approximate=False)
    v = jax.nn.gelu(proj(OFF_V, OFF_Q), approximate=False)
    y_a = _spatial_gating(u, v, sg_ln_g, sg_ln_b, sg_w, sg_b) @ w_branch_a
    q = proj(OFF_Q, OFF_K).reshape(B, S, DIFF_HEADS, 2, DIFF_QK_DIM)
    k = proj(OFF_K, OFF_VAL).reshape(B, S, DIFF_HEADS, 2, DIFF_QK_DIM)
    val = proj(OFF_VAL, OFF_GA).reshape(B, S, DIFF_HEADS, DIFF_V_DIM)
    lambda_init = 0.8 - 0.6 * math.exp(-0.3 * layer)
    lam = (jnp.exp(jnp.sum((lam_q1 * lam_k1).astype(jnp.float32)))
           - jnp.exp(jnp.sum((lam_q2 * lam_k2).astype(jnp.float32))) + lambda_init)
    y_b = _diff_attention(q, k, val, lam, subln_g, lambda_init) @ w_branch_b
    g_a = jax.nn.sigmoid(proj(OFF_GA, OFF_GB))
    g_b = jax.nn.sigmoid(proj(OFF_GB, IN_WIDTH))
    return (g_a * y_a + g_b * y_b) @ w_out


def _hier_moe(x, w_group, b_group, w_expert, b_expert, w_gate, w_up, w_down):
    B, S, D = x.shape
    t = x.reshape(B * S, D)
    T = B * S
    g_logits = (t @ w_group).astype(jnp.float32) + b_group.astype(jnp.float32)
    g_sel = jnp.argmax(g_logits, axis=-1)
    g_w = jnp.take_along_axis(jax.nn.softmax(g_logits, axis=-1), g_sel[:, None], axis=-1)
    e_logits = ((t @ w_expert).astype(jnp.float32) + b_expert.astype(jnp.float32))
    e_logits = e_logits.reshape(T, N_GROUPS, EXPERTS_PER_GROUP)
    e_logits = jnp.take_along_axis(e_logits, g_sel[:, None, None], axis=1)[:, 0]
    top_v, top_i = lax.top_k(e_logits, EXPERT_TOPK)
    top_w = jax.nn.softmax(top_v, axis=-1) * g_w
    expert_id = (g_sel[:, None] * EXPERTS_PER_GROUP + top_i).astype(jnp.int32)
    A = T * EXPERT_TOPK
    flat_e = expert_id.reshape(A)
    flat_w = top_w.reshape(A)
    flat_tok = jnp.arange(A, dtype=jnp.int32) // EXPERT_TOPK
    order = jnp.argsort(flat_e)
    sorted_e = flat_e[order]
    counts = jnp.zeros((N_EXPERTS,), jnp.int32).at[flat_e].add(1)
    padded = (counts + DISPATCH_BLOCK - 1) // DISPATCH_BLOCK * DISPATCH_BLOCK
    start = jnp.cumsum(counts) - counts
    pad_end = jnp.cumsum(padded)
    pad_start = pad_end - padded
    dest = pad_start[sorted_e] + jnp.arange(A, dtype=jnp.int32) - start[sorted_e]
    P = A + N_EXPERTS * DISPATCH_BLOCK
    n_blocks = P // DISPATCH_BLOCK
    slot_tok = jnp.zeros((P,), jnp.int32).at[dest].set(flat_tok[order])
    slot_w = jnp.zeros((P,), jnp.float32).at[dest].set(flat_w[order])
    block_start = jnp.arange(n_blocks, dtype=jnp.int32) * DISPATCH_BLOCK
    block_e = jnp.minimum(jnp.searchsorted(pad_end, block_start, side='right'), N_EXPERTS - 1)

    def expert_block(args):
        tok, wgt, e = args
        xb = t[tok]
        hb = jax.nn.silu(xb @ w_gate[e]) * (xb @ w_up[e])
        return (hb @ w_down[e]) * wgt[:, None].astype(xb.dtype)

    out = lax.map(expert_block, (slot_tok.reshape(n_blocks, DISPATCH_BLOCK),
                                 slot_w.reshape(n_blocks, DISPATCH_BLOCK), block_e))
    y = jnp.zeros_like(t).at[slot_tok].add(out.reshape(P, D))
    return y.reshape(B, S, D)


def setup_inputs(seed: int = 0) -> dict:
    key = jax.random.key(seed)
    ks = jax.random.split(key, 32)
    L = DEPTH

    def nrm(k, shape, scale):
        return jax.random.normal(k, shape, jnp.float32) * scale

    col_scale = np.ones((IN_WIDTH,), np.float32)
    col_scale[OFF_VAL:OFF_GA] = BETA
    return {
        'x': nrm(ks[0], (BATCH, SEQ, D_MODEL), 1.0),
        'w_in': nrm(ks[1], (L, D_MODEL, IN_WIDTH), D_MODEL ** -0.5) * jnp.asarray(col_scale),
        'b_in': nrm(ks[2], (L, IN_WIDTH), 0.01),
        'sg_ln_g': 1.0 + nrm(ks[3], (L, GMLP_WIDTH), 0.05),
        'sg_ln_b': nrm(ks[4], (L, GMLP_WIDTH), 0.01),
        'sg_w': nrm(ks[5], (L, GMLP_GROUPS, GMLP_BLOCK, GMLP_BLOCK), GMLP_BLOCK ** -0.5),
        'sg_b': 1.0 + nrm(ks[6], (L, GMLP_GROUPS, GMLP_BLOCK), 0.1),
        'w_branch_a': nrm(ks[7], (L, GMLP_WIDTH, D_MODEL), BETA * GMLP_WIDTH ** -0.5),
        'lam_q1': nrm(ks[8], (L, DIFF_QK_DIM), 0.1),
        'lam_k1': nrm(ks[9], (L, DIFF_QK_DIM), 0.1),
        'lam_q2': nrm(ks[10], (L, DIFF_QK_DIM), 0.1),
        'lam_k2': nrm(ks[11], (L, DIFF_QK_DIM), 0.1),
        'subln_g': 1.0 + nrm(ks[12], (L, DIFF_V_DIM), 0.05),
        'w_branch_b': nrm(ks[13], (L, DIFF_WIDTH, D_MODEL), BETA * DIFF_WIDTH ** -0.5),
        'w_out': nrm(ks[14], (L, D_MODEL, D_MODEL), BETA * D_MODEL ** -0.5),
        'ln1_g': 1.0 + nrm(ks[15], (L, D_MODEL), 0.05),
        'ln1_b': nrm(ks[16], (L, D_MODEL), 0.01),
        'w_group': nrm(ks[17], (L, D_MODEL, N_GROUPS), D_MODEL ** -0.5),
        'b_group': nrm(ks[18], (L, N_GROUPS), 0.01),
        'w_expert': nrm(ks[19], (L, D_MODEL, N_EXPERTS), D_MODEL ** -0.5),
        'b_expert': nrm(ks[20], (L, N_EXPERTS), 0.01),
        'w_gate': nrm(ks[21], (L, N_EXPERTS, D_MODEL, EXPERT_HIDDEN), D_MODEL ** -0.5),
        'w_up': nrm(ks[22], (L, N_EXPERTS, D_MODEL, EXPERT_HIDDEN), D_MODEL ** -0.5),
        'w_down': nrm(ks[23], (L, N_EXPERTS, EXPERT_HIDDEN, D_MODEL), BETA * EXPERT_HIDDEN ** -0.5),
        'ln2_g': 1.0 + nrm(ks[24], (L, D_MODEL), 0.05),
        'ln2_b': nrm(ks[25], (L, D_MODEL), 0.01),
    }


def reference(x, w_in, b_in, sg_ln_g, sg_ln_b, sg_w, sg_b, w_branch_a,
              lam_q1, lam_k1, lam_q2, lam_k2, subln_g, w_branch_b, w_out,
              ln1_g, ln1_b, w_group, b_group, w_expert, b_expert,
              w_gate, w_up, w_down, ln2_g, ln2_b):
    h = x
    for layer in range(DEPTH):
        mix = _token_mixer(h, layer, w_in[layer], b_in[layer], sg_ln_g[layer], sg_ln_b[layer],
                           sg_w[layer], sg_b[layer], w_branch_a[layer],
                           lam_q1[layer], lam_k1[layer], lam_q2[layer], lam_k2[layer],
                           subln_g[layer], w_branch_b[layer], w_out[layer])
        h = _layer_norm(ALPHA * h + mix, ln1_g[layer], ln1_b[layer])
        ffn = _hier_moe(h, w_group[layer], b_group[layer], w_expert[layer], b_expert[layer],
                        w_gate[layer], w_up[layer], w_down[layer])
        h = _layer_norm(ALPHA * h + ffn, ln2_g[layer], ln2_b[layer])
    return h
```

```python
import functools
import math

import jax
import jax.numpy as jnp
import numpy as np
from jax import lax
from jax.experimental import pallas as pl
from jax.experimental.pallas import tpu as pltpu

D_MODEL = 1024
DEPTH = 1
CHUNK = 64
GMLP_GROUPS = 8
GMLP_GROUP_DIM = 64
GMLP_WIDTH = GMLP_GROUPS * GMLP_GROUP_DIM
GMLP_BLOCK = 128
DIFF_HEADS = 8
DIFF_QK_DIM = 64
DIFF_V_DIM = 2 * DIFF_QK_DIM
DIFF_QK_WIDTH = DIFF_HEADS * 2 * DIFF_QK_DIM
DIFF_WIDTH = DIFF_HEADS * DIFF_V_DIM
N_GROUPS = 4
EXPERTS_PER_GROUP = 8
N_EXPERTS = N_GROUPS * EXPERTS_PER_GROUP
EXPERT_TOPK = 2
EXPERT_HIDDEN = 512
DISPATCH_BLOCK = 256
ALPHA = (2.0 * DEPTH) ** 0.25
LN_EPS = 1e-5
RMS_EPS = 1e-5
OFF_U = 0
OFF_V = OFF_U + GMLP_WIDTH
OFF_Q = OFF_V + GMLP_WIDTH
OFF_K = OFF_Q + DIFF_QK_WIDTH
OFF_VAL = OFF_K + DIFF_QK_WIDTH
OFF_GA = OFF_VAL + DIFF_WIDTH
OFF_GB = OFF_GA + D_MODEL
IN_WIDTH = OFF_GB + D_MODEL

LANES = 128
HEAD_W = 2 * DIFF_QK_DIM
MASKED = -1e30
VMEM_LIMIT = 56 * 1024 * 1024

PROJ_TM = 512
ATT_T = 256
MIX_TM = 256
ROUTE_TM = 512
ROW_TM = 256

BF16 = jnp.bfloat16
F32 = jnp.float32


def _cparams(sem):
    return pltpu.CompilerParams(dimension_semantics=sem, vmem_limit_bytes=VMEM_LIMIT)


def _dot(a, b):
    return jnp.dot(a, b, preferred_element_type=F32)


def _dot_nt(a, b):
    return lax.dot_general(a, b, (((1,), (1,)), ((), ())), preferred_element_type=F32)


def _gelu(x):
    return 0.5 * x * (1.0 + lax.erf(x * (2.0 ** -0.5)))


def _sigmoid(x):
    return 1.0 / (1.0 + jnp.exp(-x))


def _layer_norm(x, g, b):
    mu = jnp.mean(x, axis=-1, keepdims=True)
    xc = x - mu
    var = jnp.mean(xc * xc, axis=-1, keepdims=True)
    return xc * lax.rsqrt(var + LN_EPS) * g + b


def _qkv_kernel(x_ref, wk_ref, bk_ref, wqt_ref, bq_ref, wvt_ref, bv_ref, k_ref, qt_ref, vt_ref):
    xb = x_ref[...].astype(BF16)
    k_ref[...] = (_dot(xb, wk_ref[...]) + bk_ref[...]).astype(BF16)
    qt = (_dot_nt(wqt_ref[...], xb) + bq_ref[...]) * (DIFF_QK_DIM ** -0.5)
    qt_ref[...] = qt.astype(BF16)
    vt = (_dot_nt(wvt_ref[...], xb) + bv_ref[...]).astype(BF16)
    for hh in range(DIFF_HEADS):
        for j in range(PROJ_TM // ATT_T):
            vt_ref[hh, j] = vt[hh * HEAD_W:(hh + 1) * HEAD_W, j * ATT_T:(j + 1) * ATT_T]


def _qkv_proj(x, wk, bk, wqt, bq, wvt, bv):
    B, S, D = x.shape
    nk = S // ATT_T
    jt = PROJ_TM // ATT_T
    const2 = lambda b, i: (0, 0)
    return pl.pallas_call(
        _qkv_kernel,
        grid=(B, S // PROJ_TM),
        in_specs=[
            pl.BlockSpec((None, PROJ_TM, D), lambda b, i: (b, i, 0)),
            pl.BlockSpec((D, DIFF_QK_WIDTH), const2),
            pl.BlockSpec((1, DIFF_QK_WIDTH), const2),
            pl.BlockSpec((DIFF_QK_WIDTH, D), const2),
            pl.BlockSpec((DIFF_QK_WIDTH, 1), const2),
            pl.BlockSpec((DIFF_WIDTH, D), const2),
            pl.BlockSpec((DIFF_WIDTH, 1), const2),
        ],
        out_specs=[
            pl.BlockSpec((None, PROJ_TM, DIFF_QK_WIDTH), lambda b, i: (b, i, 0)),
            pl.BlockSpec((None, DIFF_QK_WIDTH, PROJ_TM), lambda b, i: (b, 0, i)),
            pl.BlockSpec((None, DIFF_HEADS, jt, HEAD_W, ATT_T), lambda b, i: (b, 0, i, 0, 0)),
        ],
        out_shape=[
            jax.ShapeDtypeStruct((B, S, DIFF_QK_WIDTH), BF16),
            jax.ShapeDtypeStruct((B, DIFF_QK_WIDTH, S), BF16),
            jax.ShapeDtypeStruct((B, DIFF_HEADS, nk, HEAD_W, ATT_T), BF16),
        ],
        compiler_params=_cparams(("arbitrary", "arbitrary")),
        name="qkv_proj",
    )(x, wk, bk, wqt, bq, wvt, bv)


def _attn_kernel(slope_ref, qt_ref, k_ref, vt_ref, jb_ref, db_ref, lamv_ref, g_ref, o_ref, acc_ref,
                 *, lambda_init):
    h = pl.program_id(1)
    qi = pl.program_id(2)
    T = ATT_T
    slope = slope_ref[h]

    q = qt_ref[...]
    row = lax.broadcasted_iota(jnp.int32, q.shape, 0)
    zero = jnp.zeros_like(q)
    q12 = jnp.concatenate([jnp.where(row < DIFF_QK_DIM, q, zero),
                           jnp.where(row >= DIFF_QK_DIM, q, zero)], axis=1)

    acc_ref[...] = jnp.zeros_like(acc_ref)

    def tile(kj, bias, m, l):
        kt = k_ref[pl.ds(pl.multiple_of(kj * T, T), T), :]
        t = _dot(kt, q12) + bias
        c = slope * (kj * T).astype(F32)
        m_new = jnp.maximum(m, jnp.max(t, axis=0, keepdims=True) + c)
        p = jnp.exp(t - (m_new - c))
        a = jnp.exp(m - m_new)
        l_new = a * l + jnp.sum(p, axis=0, keepdims=True)
        acc_ref[...] = a * acc_ref[...] + _dot(vt_ref[kj], p.astype(BF16))
        return m_new, l_new

    m0 = jnp.full((1, 2 * T), MASKED, F32)
    l0 = jnp.zeros((1, 2 * T), F32)

    def body(kj, carry):
        return tile(kj, jb_ref[...], *carry)

    m, l = lax.fori_loop(0, qi, body, (m0, l0))
    m, l = tile(qi, db_ref[...], m, l)

    lv = lamv_ref[...]
    lam = (jnp.exp(jnp.sum(lv[0:1] * lv[1:2], axis=1, keepdims=True))
           - jnp.exp(jnp.sum(lv[2:3] * lv[3:4], axis=1, keepdims=True)) + lambda_init)
    o12 = acc_ref[...] * (1.0 / l)
    o = o12[:, :T] - lam * o12[:, T:]
    o = o * lax.rsqrt(jnp.mean(o * o, axis=0, keepdims=True) + RMS_EPS)
    o = o * g_ref[...] * (1.0 - lambda_init)
    o_ref[...] = o.T.astype(BF16)


def _diff_attention(qt, k, vt, slopes, jbias, dbias, lamv, gcol, lambda_init):
    B, S, _ = k.shape
    nq = S // ATT_T
    grid_spec = pltpu.PrefetchScalarGridSpec(
        num_scalar_prefetch=1,
        grid=(B, DIFF_HEADS, nq),
        in_specs=[
            pl.BlockSpec((None, HEAD_W, ATT_T), lambda b, h, i, s: (b, h, i)),
            pl.BlockSpec((None, S, HEAD_W), lambda b, h, i, s: (b, 0, h)),
            pl.BlockSpec((None, None, nq, HEAD_W, ATT_T), lambda b, h, i, s: (b, h, 0, 0, 0)),
            pl.BlockSpec((None, ATT_T, 2 * ATT_T), lambda b, h, i, s: (h, 0, 0)),
            pl.BlockSpec((None, ATT_T, 2 * ATT_T), lambda b, h, i, s: (h, 0, 0)),
            pl.BlockSpec((4, DIFF_QK_DIM), lambda b, h, i, s: (0, 0)),
            pl.BlockSpec((HEAD_W, 1), lambda b, h, i, s: (0, 0)),
        ],
        out_specs=pl.BlockSpec((None, ATT_T, HEAD_W), lambda b, h, i, s: (b, i, h)),
        scratch_shapes=[pltpu.VMEM((HEAD_W, 2 * ATT_T), F32)],
    )
    return pl.pallas_call(
        functools.partial(_attn_kernel, lambda_init=lambda_init),
        grid_spec=grid_spec,
        out_shape=jax.ShapeDtypeStruct((B, S, DIFF_WIDTH), BF16),
        compiler_params=_cparams(("arbitrary", "arbitrary", "arbitrary")),
        name="diff_attn",
    )(slopes, qt, k, vt, jbias, dbias, lamv, gcol)


def _mixer_kernel(x_ref, attn_ref, wuv_ref, buv_ref, wg_ref, bg_ref, lng_ref, lnb_ref, sgw_ref, sgb_ref,
                  wa_ref, wb_ref, wo_ref, g1_ref, b1_ref, h_ref):
    x = x_ref[...]
    xb = x.astype(BF16)
    uv = _gelu(_dot(xb, wuv_ref[...]) + buv_ref[...])
    u = uv[:, :GMLP_WIDTH]
    v = _layer_norm(uv[:, GMLP_WIDTH:], lng_ref[...], lnb_ref[...]).astype(BF16)

    ti = lax.broadcasted_iota(jnp.int32, (GMLP_BLOCK, GMLP_BLOCK), 0) // CHUNK
    si = lax.broadcasted_iota(jnp.int32, (GMLP_BLOCK, GMLP_BLOCK), 1) // CHUNK
    causal = ti >= si
    w = [jnp.where(causal, sgw_ref[g], 0.0).astype(BF16) for g in range(GMLP_GROUPS)]
    lane = lax.broadcasted_iota(jnp.int32, (GMLP_BLOCK, LANES), 1)
    lo = lane < GMLP_GROUP_DIM
    vzero = jnp.zeros((GMLP_BLOCK, LANES), BF16)

    blocks = []
    for n in range(MIX_TM // GMLP_BLOCK):
        slabs = []
        for p in range(GMLP_WIDTH // LANES):
            vs = v[n * GMLP_BLOCK:(n + 1) * GMLP_BLOCK, p * LANES:(p + 1) * LANES]
            mixed = (_dot(w[2 * p], jnp.where(lo, vs, vzero))
                     + _dot(w[2 * p + 1], jnp.where(lo, vzero, vs)) + sgb_ref[p])
            slabs.append(mixed)
        blocks.append(jnp.concatenate(slabs, axis=1))
    mixed = jnp.concatenate(blocks, axis=0)

    y_a = _dot((u * mixed).astype(BF16), wa_ref[...])
    y_b = _dot(attn_ref[...], wb_ref[...])
    gates = _sigmoid(_dot(xb, wg_ref[...]) + bg_ref[...])
    z = gates[:, :D_MODEL] * y_a + gates[:, D_MODEL:] * y_b
    mix = _dot(z.astype(BF16), wo_ref[...])
    h_ref[...] = _layer_norm(ALPHA * x + mix, g1_ref[...], b1_ref[...])


def _mixer(x2, attn2, wuv, buv, wg, bg, lng, lnb, sgw, sgb, wa, wb, wo, g1, b1):
    T, D = x2.shape
    c2 = lambda i: (0, 0)
    c3 = lambda i: (0, 0, 0)
    full = lambda a: pl.BlockSpec(a.shape, c2 if a.ndim == 2 else c3)
    return pl.pallas_call(
        _mixer_kernel,
        grid=(T // MIX_TM,),
        in_specs=[pl.BlockSpec((MIX_TM, D), lambda i: (i, 0)),
                  pl.BlockSpec((MIX_TM, DIFF_WIDTH), lambda i: (i, 0))]
                 + [full(a) for a in (wuv, buv, wg, bg, lng, lnb, sgw, sgb, wa, wb, wo, g1, b1)],
        out_specs=pl.BlockSpec((MIX_TM, D), lambda i: (i, 0)),
        out_shape=jax.ShapeDtypeStruct((T, D), F32),
        compiler_params=_cparams(("arbitrary",)),
        name="mixer",
    )(x2, attn2, wuv, buv, wg, bg, lng, lnb, sgw, sgb, wa, wb, wo, g1, b1)


ROUTER_ROWS = 8 + N_EXPERTS


def _router_kernel(h_ref, wr_hi_ref, wr_lo_ref, br_ref, eid_ref, wts_ref):
    hf = h_ref[...]
    hi = hf.astype(BF16)
    lo = (hf - hi.astype(F32)).astype(BF16)
    logits = (_dot_nt(wr_hi_ref[...], hi) + _dot_nt(wr_lo_ref[...], hi)
              + _dot_nt(wr_hi_ref[...], lo)) + br_ref[...]
    gidx = lax.broadcasted_iota(jnp.int32, (8, ROUTE_TM), 0)
    gl = jnp.where(gidx < N_GROUPS, logits[0:8], -jnp.inf)
    gmax = jnp.max(gl, axis=0, keepdims=True)
    g_sel = jnp.min(jnp.where(gl == gmax, gidx, N_GROUPS), axis=0, keepdims=True)
    g_w = 1.0 / jnp.sum(jnp.exp(gl - gmax), axis=0, keepdims=True)

    el = logits[8:8 + EXPERTS_PER_GROUP]
    for g in range(1, N_GROUPS):
        el = jnp.where(g_sel == g, logits[8 + g * EXPERTS_PER_GROUP:8 + (g + 1) * EXPERTS_PER_GROUP], el)
    eidx = lax.broadcasted_iota(jnp.int32, el.shape, 0)
    v1 = jnp.max(el, axis=0, keepdims=True)
    i1 = jnp.min(jnp.where(el == v1, eidx, EXPERTS_PER_GROUP), axis=0, keepdims=True)
    rest = jnp.where(eidx == i1, -jnp.inf, el)
    v2 = jnp.max(rest, axis=0, keepdims=True)
    i2 = jnp.min(jnp.where(rest == v2, eidx, EXPERTS_PER_GROUP), axis=0, keepdims=True)
    e21 = jnp.exp(v2 - v1)
    w1 = g_w / (1.0 + e21)
    w2 = g_w * e21 / (1.0 + e21)
    base = g_sel * EXPERTS_PER_GROUP
    eid_ref[0:1, :] = base + i1
    eid_ref[1:2, :] = base + i2
    wts_ref[0:1, :] = w1
    wts_ref[1:2, :] = w2


def _router(h1, wr_hi, wr_lo, br):
    T, D = h1.shape
    c2 = lambda i: (0, 0)
    return pl.pallas_call(
        _router_kernel,
        grid=(T // ROUTE_TM,),
        in_specs=[pl.BlockSpec((ROUTE_TM, D), lambda i: (i, 0)),
                  pl.BlockSpec(wr_hi.shape, c2), pl.BlockSpec(wr_lo.shape, c2), pl.BlockSpec(br.shape, c2)],
        out_specs=[pl.BlockSpec((EXPERT_TOPK, ROUTE_TM), lambda i: (0, i)),
                   pl.BlockSpec((EXPERT_TOPK, ROUTE_TM), lambda i: (0, i))],
        out_shape=[jax.ShapeDtypeStruct((EXPERT_TOPK, T), jnp.int32),
                   jax.ShapeDtypeStruct((EXPERT_TOPK, T), F32)],
        compiler_params=_cparams(("arbitrary",)),
        name="router",
    )(h1, wr_hi, wr_lo, br)


def _rank_kernel(eid_ref, tri_ref, rank_ref, cnt_ref, run_ref):
    @pl.when(pl.program_id(0) == 0)
    def _():
        run_ref[...] = jnp.zeros_like(run_ref)

    eid = eid_ref[...]
    eiota = lax.broadcasted_iota(jnp.int32, (N_EXPERTS, ROUTE_TM), 0)
    run = run_ref[...]
    ranks = []
    for c in range(EXPERT_TOPK):
        hot = eiota == eid[c:c + 1]
        hotf = hot.astype(F32)
        before = _dot(hotf.astype(BF16), tri_ref[...])
        ranks.append(jnp.sum(jnp.where(hot, run + before, 0.0), axis=0, keepdims=True))
        run = run + jnp.sum(hotf, axis=1, keepdims=True)
    run_ref[...] = run
    for c in range(EXPERT_TOPK):
        rank_ref[c:c + 1, :] = ranks[c].astype(jnp.int32)
    cnt_ref[...] = run.astype(jnp.int32)


def _ranks(eid, tri):
    _, T = eid.shape
    return pl.pallas_call(
        _rank_kernel,
        grid=(T // ROUTE_TM,),
        in_specs=[pl.BlockSpec((EXPERT_TOPK, ROUTE_TM), lambda i: (0, i)),
                  pl.BlockSpec(tri.shape, lambda i: (0, 0))],
        out_specs=[pl.BlockSpec((EXPERT_TOPK, ROUTE_TM), lambda i: (0, i)),
                   pl.BlockSpec((N_EXPERTS, 1), lambda i: (0, 0))],
        out_shape=[jax.ShapeDtypeStruct((EXPERT_TOPK, T), jnp.int32),
                   jax.ShapeDtypeStruct((N_EXPERTS, 1), jnp.int32)],
        scratch_shapes=[pltpu.VMEM((N_EXPERTS, 1), F32)],
        compiler_params=_cparams(("arbitrary",)),
        name="rank",
    )(eid, tri)


def _row_copies(pad_ref, eid_ref, rank_ref, make_copy):
    def issue(t, _):
        for c in range(EXPERT_TOPK):
            dest = pad_ref[eid_ref[0, c, t]] + rank_ref[0, c, t]
            make_copy(c, t, dest).start()
        return 0

    lax.fori_loop(0, ROW_TM, issue, 0)

    def drain(t, _):
        for c in range(EXPERT_TOPK):
            make_copy(c, t, 0).wait()
        return 0

    lax.fori_loop(0, ROW_TM, drain, 0)


def _dispatch_kernel(pad_ref, eid_ref, rank_ref, h_ref, xs_in_ref, xs_ref, sem):
    del xs_in_ref

    def make_copy(c, t, dest):
        return pltpu.make_async_copy(h_ref.at[pl.ds(t, 1)], xs_ref.at[pl.ds(dest, 1)], sem)

    _row_copies(pad_ref, eid_ref, rank_ref, make_copy)


def _dispatch(pad_start, eid3, rank3, h1, xs_zero):
    T, D = h1.shape
    grid_spec = pltpu.PrefetchScalarGridSpec(
        num_scalar_prefetch=1,
        grid=(T // ROW_TM,),
        in_specs=[
            pl.BlockSpec((1, EXPERT_TOPK, ROW_TM), lambda i, p: (i, 0, 0), memory_space=pltpu.SMEM),
            pl.BlockSpec((1, EXPERT_TOPK, ROW_TM), lambda i, p: (i, 0, 0), memory_space=pltpu.SMEM),
            pl.BlockSpec((ROW_TM, D), lambda i, p: (i, 0)),
            pl.BlockSpec(memory_space=pl.ANY),
        ],
        out_specs=pl.BlockSpec(memory_space=pl.ANY),
        scratch_shapes=[pltpu.SemaphoreType.DMA(())],
    )
    return pl.pallas_call(
        _dispatch_kernel,
        grid_spec=grid_spec,
        out_shape=jax.ShapeDtypeStruct(xs_zero.shape, xs_zero.dtype),
        input_output_aliases={4: 0},
        compiler_params=_cparams(("arbitrary",)),
        name="dispatch",
    )(pad_start, eid3, rank3, h1, xs_zero)


def _expert_kernel(be_ref, nb_ref, xs_ref, wg_ref, wu_ref, wd_ref, o_ref):
    i = pl.program_id(0)

    @pl.when(i < nb_ref[0])
    def _():
        xb = xs_ref[...].astype(BF16)
        gate = _dot(xb, wg_ref[...])
        up = _dot(xb, wu_ref[...])
        hb = (gate * _sigmoid(gate) * up).astype(BF16)
        o_ref[...] = _dot(hb, wd_ref[...])

    @pl.when(i >= nb_ref[0])
    def _():
        o_ref[...] = jnp.zeros_like(o_ref)


def _experts(block_e, n_used, xs, wg, wu, wd):
    P, D = xs.shape
    grid_spec = pltpu.PrefetchScalarGridSpec(
        num_scalar_prefetch=2,
        grid=(P // DISPATCH_BLOCK,),
        in_specs=[
            pl.BlockSpec((DISPATCH_BLOCK, D), lambda i, be, nb: (i, 0)),
            pl.BlockSpec((None, D, EXPERT_HIDDEN), lambda i, be, nb: (be[i], 0, 0)),
            pl.BlockSpec((None, D, EXPERT_HIDDEN), lambda i, be, nb: (be[i], 0, 0)),
            pl.BlockSpec((None, EXPERT_HIDDEN, D), lambda i, be, nb: (be[i], 0, 0)),
        ],
        out_specs=pl.BlockSpec((DISPATCH_BLOCK, D), lambda i, be, nb: (i, 0)),
    )
    return pl.pallas_call(
        _expert_kernel,
        grid_spec=grid_spec,
        out_shape=jax.ShapeDtypeStruct((P, D), F32),
        compiler_params=_cparams(("arbitrary",)),
        name="experts",
    )(block_e, n_used, xs, wg, wu, wd)


def _combine_kernel(pad_ref, eid_ref, rank_ref, h_ref, wts_ref, g2_ref, b2_ref, ys_ref, o_ref, rows_ref, sem):
    def make_copy(c, t, dest):
        return pltpu.make_async_copy(ys_ref.at[pl.ds(dest, 1)], rows_ref.at[c, pl.ds(t, 1)], sem)

    _row_copies(pad_ref, eid_ref, rank_ref, make_copy)
    w = wts_ref[...]
    ffn = w[:, 0:1] * rows_ref[0] + w[:, 1:2] * rows_ref[1]
    o_ref[...] = _layer_norm(ALPHA * h_ref[...] + ffn, g2_ref[...], b2_ref[...])


def _combine(pad_start, eid3, rank3, h1, wts_t, g2, b2, ys):
    T, D = h1.shape
    grid_spec = pltpu.PrefetchScalarGridSpec(
        num_scalar_prefetch=1,
        grid=(T // ROW_TM,),
        in_specs=[
            pl.BlockSpec((1, EXPERT_TOPK, ROW_TM), lambda i, p: (i, 0, 0), memory_space=pltpu.SMEM),
            pl.BlockSpec((1, EXPERT_TOPK, ROW_TM), lambda i, p: (i, 0, 0), memory_space=pltpu.SMEM),
            pl.BlockSpec((ROW_TM, D), lambda i, p: (i, 0)),
            pl.BlockSpec((ROW_TM, EXPERT_TOPK), lambda i, p: (i, 0)),
            pl.BlockSpec((1, D), lambda i, p: (0, 0)),
            pl.BlockSpec((1, D), lambda i, p: (0, 0)),
            pl.BlockSpec(memory_space=pl.ANY),
        ],
        out_specs=pl.BlockSpec((ROW_TM, D), lambda i, p: (i, 0)),
        scratch_shapes=[pltpu.VMEM((EXPERT_TOPK, ROW_TM, D), F32), pltpu.SemaphoreType.DMA(())],
    )
    return pl.pallas_call(
        _combine_kernel,
        grid_spec=grid_spec,
        out_shape=jax.ShapeDtypeStruct((T, D), F32),
        compiler_params=_cparams(("arbitrary",)),
        name="combine",
    )(pad_start, eid3, rank3, h1, wts_t, g2, b2, ys)


def _attn_bias_tables():
    dj = np.arange(ATT_T, dtype=np.float32)[:, None]
    di = np.tile(np.arange(ATT_T, dtype=np.float32), 2)[None, :]
    slopes = 2.0 ** (-8.0 * np.arange(1, DIFF_HEADS + 1, dtype=np.float32) / DIFF_HEADS)
    off_diag = np.broadcast_to(dj, (ATT_T, 2 * ATT_T))
    diag = di - np.abs(di - dj)
    visible = (dj // CHUNK) <= (di // CHUNK)
    jb = slopes[:, None, None] * off_diag[None]
    db = np.where(visible[None], slopes[:, None, None] * diag[None], MASKED)
    return jnp.asarray(slopes, F32), jnp.asarray(jb, F32), jnp.asarray(db, F32)


def _layer(h, layer, w_in, b_in, sg_ln_g, sg_ln_b, sg_w, sg_b, w_branch_a, lam_q1, lam_k1, lam_q2, lam_k2,
           subln_g, w_branch_b, w_out, ln1_g, ln1_b, w_group, b_group, w_expert, b_expert,
           w_gate, w_up, w_down, ln2_g, ln2_b):
    B, S, D = h.shape
    T = B * S
    row = lambda a: a.reshape(1, -1)
    col = lambda a: a.reshape(-1, 1)

    wk = w_in[:, OFF_K:OFF_VAL].astype(BF16)
    wqt = w_in[:, OFF_Q:OFF_K].T.astype(BF16)
    wvt = w_in[:, OFF_VAL:OFF_GA].T.astype(BF16)
    k, qt, vt = _qkv_proj(h, wk, row(b_in[OFF_K:OFF_VAL]), wqt, col(b_in[OFF_Q:OFF_K]),
                          wvt, col(b_in[OFF_VAL:OFF_GA]))
    slopes, jbias, dbias = _attn_bias_tables()
    lambda_init = 0.8 - 0.6 * math.exp(-0.3 * layer)
    lamv = jnp.stack([lam_q1, lam_k1, lam_q2, lam_k2]).astype(F32)
    attn = _diff_attention(qt, k, vt, slopes, jbias, dbias, lamv, col(subln_g), lambda_init)

    sgb = jnp.repeat(sg_b.reshape(GMLP_GROUPS // 2, 2, GMLP_BLOCK), GMLP_GROUP_DIM, axis=1)
    sgb = jnp.transpose(sgb, (0, 2, 1))
    h1 = _mixer(h.reshape(T, D), attn.reshape(T, DIFF_WIDTH),
                w_in[:, OFF_U:OFF_Q].astype(BF16), row(b_in[OFF_U:OFF_Q]),
                w_in[:, OFF_GA:IN_WIDTH].astype(BF16), row(b_in[OFF_GA:IN_WIDTH]),
                row(sg_ln_g), row(sg_ln_b), sg_w, sgb,
                w_branch_a.astype(BF16), w_branch_b.astype(BF16), w_out.astype(BF16),
                row(ln1_g), row(ln1_b))

    wr = jnp.zeros((LANES, D), F32)
    wr = wr.at[0:N_GROUPS].set(w_group.T).at[8:8 + N_EXPERTS].set(w_expert.T)
    wr_hi = wr.astype(BF16)
    wr_lo = (wr - wr_hi.astype(F32)).astype(BF16)
    br = jnp.zeros((LANES, 1), F32)
    br = br.at[0:N_GROUPS, 0].set(b_group.astype(F32)).at[8:8 + N_EXPERTS, 0].set(b_expert.astype(F32))
    eid, wts = _router(h1, wr_hi, wr_lo, br)

    tri = jnp.asarray(np.triu(np.ones((ROUTE_TM, ROUTE_TM), np.float32), k=1), BF16)
    rank, counts = _ranks(eid, tri)

    counts = counts[:, 0]
    padded = (counts + DISPATCH_BLOCK - 1) // DISPATCH_BLOCK * DISPATCH_BLOCK
    pad_end = jnp.cumsum(padded)
    pad_start = (pad_end - padded).astype(jnp.int32)
    P = T * EXPERT_TOPK + N_EXPERTS * DISPATCH_BLOCK
    n_blocks = P // DISPATCH_BLOCK
    block_start = jnp.arange(n_blocks, dtype=jnp.int32) * DISPATCH_BLOCK
    block_e = jnp.minimum(jnp.searchsorted(pad_end, block_start, side='right'), N_EXPERTS - 1).astype(jnp.int32)
    n_used = (pad_end[-1:] // DISPATCH_BLOCK).astype(jnp.int32)

    tile3 = lambda a: jnp.transpose(a.reshape(EXPERT_TOPK, T // ROW_TM, ROW_TM), (1, 0, 2))
    eid3, rank3 = tile3(eid), tile3(rank)
    xs = _dispatch(pad_start, eid3, rank3, h1, jnp.zeros((P, D), F32))
    ys = _experts(block_e, n_used, xs, w_gate.astype(BF16), w_up.astype(BF16), w_down.astype(BF16))
    out = _combine(pad_start, eid3, rank3, h1, wts.T, row(ln2_g), row(ln2_b), ys)
    return out.reshape(B, S, D)


def kernel(x, w_in, b_in, sg_ln_g, sg_ln_b, sg_w, sg_b, w_branch_a, lam_q1, lam_k1, lam_q2, lam_k2, subln_g,
           w_branch_b, w_out, ln1_g, ln1_b, w_group, b_group, w_expert, b_expert, w_gate, w_up, w_down,
           ln2_g, ln2_b):
    h = x
    for layer in range(DEPTH):
        h = _layer(h, layer, w_in[layer], b_in[layer], sg_ln_g[layer], sg_ln_b[layer], sg_w[layer],
                   sg_b[layer], w_branch_a[layer], lam_q1[layer], lam_k1[layer], lam_q2[layer],
                   lam_k2[layer], subln_g[layer], w_branch_b[layer], w_out[layer], ln1_g[layer],
                   ln1_b[layer], w_group[layer], b_group[layer], w_expert[layer], b_expert[layer],
                   w_gate[layer], w_up[layer], w_down[layer], ln2_g[layer], ln2_b[layer])
    return h
```

```python
import functools
import math

import jax
import jax.numpy as jnp
import numpy as np
from jax import lax
from jax.experimental import pallas as pl
from jax.experimental.pallas import tpu as pltpu

D_MODEL = 1024
DEPTH = 1
CHUNK = 64
GMLP_GROUPS = 8
GMLP_GROUP_DIM = 64
GMLP_WIDTH = GMLP_GROUPS * GMLP_GROUP_DIM
GMLP_BLOCK = 128
DIFF_HEADS = 8
DIFF_QK_DIM = 64
DIFF_V_DIM = 2 * DIFF_QK_DIM
DIFF_QK_WIDTH = DIFF_HEADS * 2 * DIFF_QK_DIM
DIFF_WIDTH = DIFF_HEADS * DIFF_V_DIM
N_GROUPS = 4
EXPERTS_PER_GROUP = 8
N_EXPERTS = N_GROUPS * EXPERTS_PER_GROUP
EXPERT_TOPK = 2
EXPERT_HIDDEN = 512
DISPATCH_BLOCK = 256
ALPHA = (2.0 * DEPTH) ** 0.25
LN_EPS = 1e-5
RMS_EPS = 1e-5
OFF_U = 0
OFF_V = OFF_U + GMLP_WIDTH
OFF_Q = OFF_V + GMLP_WIDTH
OFF_K = OFF_Q + DIFF_QK_WIDTH
OFF_VAL = OFF_K + DIFF_QK_WIDTH
OFF_GA = OFF_VAL + DIFF_WIDTH
OFF_GB = OFF_GA + D_MODEL
IN_WIDTH = OFF_GB + D_MODEL

LANES = 128
HEAD_W = 2 * DIFF_QK_DIM
MASKED = -1e30
VMEM_LIMIT = 56 * 1024 * 1024

PROJ_TM = 512
ATT_T = 256
ATT_TK = 2 * ATT_T
ATT_HB = 2
LOG2E = math.log2(math.e)
MIX_TM = 256
ROUTE_TM = 512
ROW_TM = 256

BF16 = jnp.bfloat16
F32 = jnp.float32


def _cparams(sem):
    return pltpu.CompilerParams(dimension_semantics=sem, vmem_limit_bytes=VMEM_LIMIT)


def _dot(a, b):
    return jnp.dot(a, b, preferred_element_type=F32)


def _dot_nt(a, b):
    return lax.dot_general(a, b, (((1,), (1,)), ((), ())), preferred_element_type=F32)


def _gelu(x):
    return 0.5 * x * (1.0 + lax.erf(x * (2.0 ** -0.5)))


def _sigmoid(x):
    return 1.0 / (1.0 + jnp.exp(-x))


def _layer_norm(x, g, b):
    mu = jnp.mean(x, axis=-1, keepdims=True)
    xc = x - mu
    var = jnp.mean(xc * xc, axis=-1, keepdims=True)
    return xc * lax.rsqrt(var + LN_EPS) * g + b


def _qkv_kernel(x_ref, wk_ref, bk_ref, wqt_ref, bq_ref, wvt_ref, bv_ref, k_ref, qt_ref, vt_ref):
    xb = x_ref[...].astype(BF16)
    k_ref[...] = (_dot(xb, wk_ref[...]) + bk_ref[...]).astype(BF16)
    qt = (_dot_nt(wqt_ref[...], xb) + bq_ref[...]) * (DIFF_QK_DIM ** -0.5 * LOG2E)
    qt_ref[...] = qt.astype(BF16)
    vt = (_dot_nt(wvt_ref[...], xb) + bv_ref[...]).astype(BF16)
    for hh in range(DIFF_HEADS):
        for j in range(PROJ_TM // ATT_TK):
            vt_ref[hh, j] = vt[hh * HEAD_W:(hh + 1) * HEAD_W, j * ATT_TK:(j + 1) * ATT_TK]


def _qkv_proj(x, wk, bk, wqt, bq, wvt, bv):
    B, S, D = x.shape
    nk = S // ATT_TK
    jt = PROJ_TM // ATT_TK
    const2 = lambda b, i: (0, 0)
    return pl.pallas_call(
        _qkv_kernel,
        grid=(B, S // PROJ_TM),
        in_specs=[
            pl.BlockSpec((None, PROJ_TM, D), lambda b, i: (b, i, 0)),
            pl.BlockSpec((D, DIFF_QK_WIDTH), const2),
            pl.BlockSpec((1, DIFF_QK_WIDTH), const2),
            pl.BlockSpec((DIFF_QK_WIDTH, D), const2),
            pl.BlockSpec((DIFF_QK_WIDTH, 1), const2),
            pl.BlockSpec((DIFF_WIDTH, D), const2),
            pl.BlockSpec((DIFF_WIDTH, 1), const2),
        ],
        out_specs=[
            pl.BlockSpec((None, PROJ_TM, DIFF_QK_WIDTH), lambda b, i: (b, i, 0)),
            pl.BlockSpec((None, DIFF_QK_WIDTH, PROJ_TM), lambda b, i: (b, 0, i)),
            pl.BlockSpec((None, DIFF_HEADS, jt, HEAD_W, ATT_TK), lambda b, i: (b, 0, i, 0, 0)),
        ],
        out_shape=[
            jax.ShapeDtypeStruct((B, S, DIFF_QK_WIDTH), BF16),
            jax.ShapeDtypeStruct((B, DIFF_QK_WIDTH, S), BF16),
            jax.ShapeDtypeStruct((B, DIFF_HEADS, nk, HEAD_W, ATT_TK), BF16),
        ],
        compiler_params=_cparams(("arbitrary", "arbitrary")),
        name="qkv_proj",
    )(x, wk, bk, wqt, bq, wvt, bv)


def _attn_kernel(slope_ref, qt_ref, k_ref, vt_ref, tab_ref, lamv_ref, g_ref, o_ref,
                 q12_ref, sa_ref, sb_ref, ta_ref, tb_ref, ml_ref, acc_ref, *, lambda_init):
    hg = pl.program_id(1)
    qi = pl.program_id(2)
    T = ATT_T
    n_groups = qi // 2 + 1

    row = lax.broadcasted_iota(jnp.int32, (HEAD_W, T), 0)
    zero = jnp.zeros((HEAD_W, T), BF16)
    for e in range(ATT_HB):
        q = qt_ref[e * HEAD_W:(e + 1) * HEAD_W, :]
        q12_ref[e, :, :T] = jnp.where(row < DIFF_QK_DIM, q, zero)
        q12_ref[e, :, T:] = jnp.where(row >= DIFF_QK_DIM, q, zero)
    acc_ref[...] = jnp.zeros_like(acc_ref)
    for e in range(ATT_HB):
        ml_ref[e, 0:1, :] = jnp.full((1, 2 * T), MASKED, F32)
        ml_ref[e, 1:2, :] = jnp.zeros((1, 2 * T), F32)

    def scores(g, s_ref, t_ref):
        k0 = 2 * g
        tab0 = jnp.where(k0 == qi, 1, 0)
        tab1 = jnp.where(k0 + 1 == qi, 1, jnp.where(k0 + 1 > qi, 2, 0))
        for e in range(ATT_HB):
            kt = k_ref[pl.ds(pl.multiple_of(k0 * T, 2 * T), 2 * T), e * HEAD_W:(e + 1) * HEAD_W]
            s = _dot(kt, q12_ref[e])
            t0 = s[:T] + tab_ref[e, tab0]
            t1 = s[T:] + tab_ref[e, tab1]
            s_ref[e, :T] = t0
            s_ref[e, T:] = t1
            t_ref[e, 0:1, :] = jnp.max(t0, axis=0, keepdims=True)
            t_ref[e, 1:2, :] = jnp.max(t1, axis=0, keepdims=True)

    def accumulate(g, s_ref, t_ref):
        j0 = (2 * g * T).astype(F32)
        for e in range(ATT_HB):
            slope = slope_ref[hg * ATT_HB + e]
            c0 = slope * j0
            c1 = slope * (j0 + T)
            m = ml_ref[e, 0:1, :]
            l = ml_ref[e, 1:2, :]
            m_new = jnp.maximum(m, jnp.maximum(t_ref[e, 0:1, :] + c0, t_ref[e, 1:2, :] + c1))
            p0 = jnp.exp2(s_ref[e, :T] - (m_new - c0))
            p1 = jnp.exp2(s_ref[e, T:] - (m_new - c1))
            a = jnp.exp2(m - m_new)
            ml_ref[e, 0:1, :] = m_new
            ml_ref[e, 1:2, :] = a * l + (jnp.sum(p0, axis=0, keepdims=True) + jnp.sum(p1, axis=0, keepdims=True))
            p = jnp.concatenate([p0.astype(BF16), p1.astype(BF16)], axis=0)
            acc_ref[e] = a * acc_ref[e] + _dot(vt_ref[e, g], p)

    scores(0, sa_ref, ta_ref)

    def pair(i, _):
        g = 2 * i
        scores(g + 1, sb_ref, tb_ref)
        accumulate(g, sa_ref, ta_ref)
        scores(jnp.minimum(g + 2, n_groups - 1), sa_ref, ta_ref)
        accumulate(g + 1, sb_ref, tb_ref)
        return 0

    lax.fori_loop(0, n_groups // 2, pair, 0)

    @pl.when(n_groups % 2 == 1)
    def _():
        accumulate(n_groups - 1, sa_ref, ta_ref)

    lv = lamv_ref[...]
    lam = (jnp.exp(jnp.sum(lv[0:1] * lv[1:2], axis=1, keepdims=True))
           - jnp.exp(jnp.sum(lv[2:3] * lv[3:4], axis=1, keepdims=True)) + lambda_init)
    for e in range(ATT_HB):
        o12 = acc_ref[e] * (1.0 / ml_ref[e, 1:2, :])
        o = o12[:, :T] - lam * o12[:, T:]
        o = o * lax.rsqrt(jnp.mean(o * o, axis=0, keepdims=True) + RMS_EPS)
        o = o * g_ref[...] * (1.0 - lambda_init)
        o_ref[:, e * HEAD_W:(e + 1) * HEAD_W] = o.T.astype(BF16)


def _diff_attention(qt, k, vt, slopes, tables, lamv, gcol, lambda_init):
    B, S, _ = k.shape
    nq = S // ATT_T
    gw = ATT_HB * HEAD_W
    grid_spec = pltpu.PrefetchScalarGridSpec(
        num_scalar_prefetch=1,
        grid=(B, DIFF_HEADS // ATT_HB, nq),
        in_specs=[
            pl.BlockSpec((None, gw, ATT_T), lambda b, h, i, s: (b, h, i)),
            pl.BlockSpec((None, S, gw), lambda b, h, i, s: (b, 0, h)),
            pl.BlockSpec((None, ATT_HB, S // ATT_TK, HEAD_W, ATT_TK), lambda b, h, i, s: (b, h, 0, 0, 0)),
            pl.BlockSpec((ATT_HB, 3, ATT_T, 2 * ATT_T), lambda b, h, i, s: (h, 0, 0, 0)),
            pl.BlockSpec((4, DIFF_QK_DIM), lambda b, h, i, s: (0, 0)),
            pl.BlockSpec((HEAD_W, 1), lambda b, h, i, s: (0, 0)),
        ],
        out_specs=pl.BlockSpec((None, ATT_T, gw), lambda b, h, i, s: (b, i, h)),
        scratch_shapes=[
            pltpu.VMEM((ATT_HB, HEAD_W, 2 * ATT_T), BF16),
            pltpu.VMEM((ATT_HB, ATT_TK, 2 * ATT_T), F32),
            pltpu.VMEM((ATT_HB, ATT_TK, 2 * ATT_T), F32),
            pltpu.VMEM((ATT_HB, 2, 2 * ATT_T), F32),
            pltpu.VMEM((ATT_HB, 2, 2 * ATT_T), F32),
            pltpu.VMEM((ATT_HB, 2, 2 * ATT_T), F32),
            pltpu.VMEM((ATT_HB, HEAD_W, 2 * ATT_T), F32),
        ],
    )
    return pl.pallas_call(
        functools.partial(_attn_kernel, lambda_init=lambda_init),
        grid_spec=grid_spec,
        out_shape=jax.ShapeDtypeStruct((B, S, DIFF_WIDTH), BF16),
        compiler_params=_cparams(("arbitrary", "arbitrary", "arbitrary")),
        name="diff_attn",
    )(slopes, qt, k, vt, tables, lamv, gcol)


def _mixer_kernel(x_ref, attn_ref, wuv_ref, buv_ref, wg_ref, bg_ref, lng_ref, lnb_ref, sgw_ref, sgb_ref,
                  wa_ref, wb_ref, wo_ref, g1_ref, b1_ref, h_ref):
    x = x_ref[...]
    xb = x.astype(BF16)
    uv = _gelu(_dot(xb, wuv_ref[...]) + buv_ref[...])
    u = uv[:, :GMLP_WIDTH]
    v = _layer_norm(uv[:, GMLP_WIDTH:], lng_ref[...], lnb_ref[...]).astype(BF16)

    ti = lax.broadcasted_iota(jnp.int32, (GMLP_BLOCK, GMLP_BLOCK), 0) // CHUNK
    si = lax.broadcasted_iota(jnp.int32, (GMLP_BLOCK, GMLP_BLOCK), 1) // CHUNK
    causal = ti >= si
    w = [jnp.where(causal, sgw_ref[g], 0.0).astype(BF16) for g in range(GMLP_GROUPS)]
    lane = lax.broadcasted_iota(jnp.int32, (GMLP_BLOCK, LANES), 1)
    lo = lane < GMLP_GROUP_DIM
    vzero = jnp.zeros((GMLP_BLOCK, LANES), BF16)

    blocks = []
    for n in range(MIX_TM // GMLP_BLOCK):
        slabs = []
        for p in range(GMLP_WIDTH // LANES):
            vs = v[n * GMLP_BLOCK:(n + 1) * GMLP_BLOCK, p * LANES:(p + 1) * LANES]
            mixed = (_dot(w[2 * p], jnp.where(lo, vs, vzero))
                     + _dot(w[2 * p + 1], jnp.where(lo, vzero, vs)) + sgb_ref[p])
            slabs.append(mixed)
        blocks.append(jnp.concatenate(slabs, axis=1))
    mixed = jnp.concatenate(blocks, axis=0)

    y_a = _dot((u * mixed).astype(BF16), wa_ref[...])
    y_b = _dot(attn_ref[...], wb_ref[...])
    gates = _sigmoid(_dot(xb, wg_ref[...]) + bg_ref[...])
    z = gates[:, :D_MODEL] * y_a + gates[:, D_MODEL:] * y_b
    mix = _dot(z.astype(BF16), wo_ref[...])
    h_ref[...] = _layer_norm(ALPHA * x + mix, g1_ref[...], b1_ref[...])


def _mixer(x2, attn2, wuv, buv, wg, bg, lng, lnb, sgw, sgb, wa, wb, wo, g1, b1):
    T, D = x2.shape
    c2 = lambda i: (0, 0)
    c3 = lambda i: (0, 0, 0)
    full = lambda a: pl.BlockSpec(a.shape, c2 if a.ndim == 2 else c3)
    return pl.pallas_call(
        _mixer_kernel,
        grid=(T // MIX_TM,),
        in_specs=[pl.BlockSpec((MIX_TM, D), lambda i: (i, 0)),
                  pl.BlockSpec((MIX_TM, DIFF_WIDTH), lambda i: (i, 0))]
                 + [full(a) for a in (wuv, buv, wg, bg, lng, lnb, sgw, sgb, wa, wb, wo, g1, b1)],
        out_specs=pl.BlockSpec((MIX_TM, D), lambda i: (i, 0)),
        out_shape=jax.ShapeDtypeStruct((T, D), F32),
        compiler_params=_cparams(("arbitrary",)),
        name="mixer",
    )(x2, attn2, wuv, buv, wg, bg, lng, lnb, sgw, sgb, wa, wb, wo, g1, b1)


def _router_kernel(h_ref, wr_hi_ref, wr_lo_ref, br_ref, eid_ref, wts_ref):
    hf = h_ref[...]
    hi = hf.astype(BF16)
    lo = (hf - hi.astype(F32)).astype(BF16)
    logits = (_dot_nt(wr_hi_ref[...], hi) + _dot_nt(wr_lo_ref[...], hi)
              + _dot_nt(wr_hi_ref[...], lo)) + br_ref[...]
    gidx = lax.broadcasted_iota(jnp.int32, (8, ROUTE_TM), 0)
    gl = jnp.where(gidx < N_GROUPS, logits[0:8], -jnp.inf)
    gmax = jnp.max(gl, axis=0, keepdims=True)
    g_sel = jnp.min(jnp.where(gl == gmax, gidx, N_GROUPS), axis=0, keepdims=True)
    g_w = 1.0 / jnp.sum(jnp.exp(gl - gmax), axis=0, keepdims=True)

    el = logits[8:8 + EXPERTS_PER_GROUP]
    for g in range(1, N_GROUPS):
        el = jnp.where(g_sel == g, logits[8 + g * EXPERTS_PER_GROUP:8 + (g + 1) * EXPERTS_PER_GROUP], el)
    eidx = lax.broadcasted_iota(jnp.int32, el.shape, 0)
    v1 = jnp.max(el, axis=0, keepdims=True)
    i1 = jnp.min(jnp.where(el == v1, eidx, EXPERTS_PER_GROUP), axis=0, keepdims=True)
    rest = jnp.where(eidx == i1, -jnp.inf, el)
    v2 = jnp.max(rest, axis=0, keepdims=True)
    i2 = jnp.min(jnp.where(rest == v2, eidx, EXPERTS_PER_GROUP), axis=0, keepdims=True)
    e21 = jnp.exp(v2 - v1)
    w1 = g_w / (1.0 + e21)
    w2 = g_w * e21 / (1.0 + e21)
    base = g_sel * EXPERTS_PER_GROUP
    eid_ref[0:1, :] = base + i1
    eid_ref[1:2, :] = base + i2
    wts_ref[0:1, :] = w1
    wts_ref[1:2, :] = w2


def _router(h1, wr_hi, wr_lo, br):
    T, D = h1.shape
    c2 = lambda i: (0, 0)
    return pl.pallas_call(
        _router_kernel,
        grid=(T // ROUTE_TM,),
        in_specs=[pl.BlockSpec((ROUTE_TM, D), lambda i: (i, 0)),
                  pl.BlockSpec(wr_hi.shape, c2), pl.BlockSpec(wr_lo.shape, c2), pl.BlockSpec(br.shape, c2)],
        out_specs=[pl.BlockSpec((EXPERT_TOPK, ROUTE_TM), lambda i: (0, i)),
                   pl.BlockSpec((EXPERT_TOPK, ROUTE_TM), lambda i: (0, i))],
        out_shape=[jax.ShapeDtypeStruct((EXPERT_TOPK, T), jnp.int32),
                   jax.ShapeDtypeStruct((EXPERT_TOPK, T), F32)],
        compiler_params=_cparams(("arbitrary",)),
        name="router",
    )(h1, wr_hi, wr_lo, br)


def _rank_kernel(eid_ref, tri_ref, rank_ref, cnt_ref, run_ref):
    @pl.when(pl.program_id(0) == 0)
    def _():
        run_ref[...] = jnp.zeros_like(run_ref)

    eid = eid_ref[...]
    eiota = lax.broadcasted_iota(jnp.int32, (N_EXPERTS, ROUTE_TM), 0)
    run = run_ref[...]
    ranks = []
    for c in range(EXPERT_TOPK):
        hot = eiota == eid[c:c + 1]
        hotf = hot.astype(F32)
        before = _dot(hotf.astype(BF16), tri_ref[...])
        ranks.append(jnp.sum(jnp.where(hot, run + before, 0.0), axis=0, keepdims=True))
        run = run + jnp.sum(hotf, axis=1, keepdims=True)
    run_ref[...] = run
    for c in range(EXPERT_TOPK):
        rank_ref[c:c + 1, :] = ranks[c].astype(jnp.int32)
    cnt_ref[...] = run.astype(jnp.int32)


def _ranks(eid, tri):
    _, T = eid.shape
    return pl.pallas_call(
        _rank_kernel,
        grid=(T // ROUTE_TM,),
        in_specs=[pl.BlockSpec((EXPERT_TOPK, ROUTE_TM), lambda i: (0, i)),
                  pl.BlockSpec(tri.shape, lambda i: (0, 0))],
        out_specs=[pl.BlockSpec((EXPERT_TOPK, ROUTE_TM), lambda i: (0, i)),
                   pl.BlockSpec((N_EXPERTS, 1), lambda i: (0, 0))],
        out_shape=[jax.ShapeDtypeStruct((EXPERT_TOPK, T), jnp.int32),
                   jax.ShapeDtypeStruct((N_EXPERTS, 1), jnp.int32)],
        scratch_shapes=[pltpu.VMEM((N_EXPERTS, 1), F32)],
        compiler_params=_cparams(("arbitrary",)),
        name="rank",
    )(eid, tri)


def _row_copies(pad_ref, eid_ref, rank_ref, make_copy):
    def issue(t, _):
        for c in range(EXPERT_TOPK):
            dest = pad_ref[eid_ref[0, c, t]] + rank_ref[0, c, t]
            make_copy(c, t, dest).start()
        return 0

    lax.fori_loop(0, ROW_TM, issue, 0)

    def drain(t, _):
        for c in range(EXPERT_TOPK):
            make_copy(c, t, 0).wait()
        return 0

    lax.fori_loop(0, ROW_TM, drain, 0)


def _dispatch_kernel(pad_ref, eid_ref, rank_ref, h_ref, xs_in_ref, xs_ref, sem):
    del xs_in_ref

    def make_copy(c, t, dest):
        return pltpu.make_async_copy(h_ref.at[pl.ds(t, 1)], xs_ref.at[pl.ds(dest, 1)], sem)

    _row_copies(pad_ref, eid_ref, rank_ref, make_copy)


def _dispatch(pad_start, eid3, rank3, h1, xs_zero):
    T, D = h1.shape
    grid_spec = pltpu.PrefetchScalarGridSpec(
        num_scalar_prefetch=1,
        grid=(T // ROW_TM,),
        in_specs=[
            pl.BlockSpec((1, EXPERT_TOPK, ROW_TM), lambda i, p: (i, 0, 0), memory_space=pltpu.SMEM),
            pl.BlockSpec((1, EXPERT_TOPK, ROW_TM), lambda i, p: (i, 0, 0), memory_space=pltpu.SMEM),
            pl.BlockSpec((ROW_TM, D), lambda i, p: (i, 0)),
            pl.BlockSpec(memory_space=pl.ANY),
        ],
        out_specs=pl.BlockSpec(memory_space=pl.ANY),
        scratch_shapes=[pltpu.SemaphoreType.DMA(())],
    )
    return pl.pallas_call(
        _dispatch_kernel,
        grid_spec=grid_spec,
        out_shape=jax.ShapeDtypeStruct(xs_zero.shape, xs_zero.dtype),
        input_output_aliases={4: 0},
        compiler_params=_cparams(("arbitrary",)),
        name="dispatch",
    )(pad_start, eid3, rank3, h1, xs_zero)


def _expert_kernel(be_ref, nb_ref, xs_ref, wg_ref, wu_ref, wd_ref, o_ref):
    i = pl.program_id(0)

    @pl.when(i < nb_ref[0])
    def _():
        xb = xs_ref[...].astype(BF16)
        gate = _dot(xb, wg_ref[...])
        up = _dot(xb, wu_ref[...])
        hb = (gate * _sigmoid(gate) * up).astype(BF16)
        o_ref[...] = _dot(hb, wd_ref[...])

    @pl.when(i >= nb_ref[0])
    def _():
        o_ref[...] = jnp.zeros_like(o_ref)


def _experts(block_e, n_used, xs, wg, wu, wd):
    P, D = xs.shape
    grid_spec = pltpu.PrefetchScalarGridSpec(
        num_scalar_prefetch=2,
        grid=(P // DISPATCH_BLOCK,),
        in_specs=[
            pl.BlockSpec((DISPATCH_BLOCK, D), lambda i, be, nb: (i, 0)),
            pl.BlockSpec((None, D, EXPERT_HIDDEN), lambda i, be, nb: (be[i], 0, 0)),
            pl.BlockSpec((None, D, EXPERT_HIDDEN), lambda i, be, nb: (be[i], 0, 0)),
            pl.BlockSpec((None, EXPERT_HIDDEN, D), lambda i, be, nb: (be[i], 0, 0)),
        ],
        out_specs=pl.BlockSpec((DISPATCH_BLOCK, D), lambda i, be, nb: (i, 0)),
    )
    return pl.pallas_call(
        _expert_kernel,
        grid_spec=grid_spec,
        out_shape=jax.ShapeDtypeStruct((P, D), F32),
        compiler_params=_cparams(("arbitrary",)),
        name="experts",
    )(block_e, n_used, xs, wg, wu, wd)


def _combine_kernel(pad_ref, eid_ref, rank_ref, h_ref, wts_ref, g2_ref, b2_ref, ys_ref, o_ref, rows_ref, sem):
    def make_copy(c, t, dest):
        return pltpu.make_async_copy(ys_ref.at[pl.ds(dest, 1)], rows_ref.at[c, pl.ds(t, 1)], sem)

    _row_copies(pad_ref, eid_ref, rank_ref, make_copy)
    w = wts_ref[...]
    ffn = w[:, 0:1] * rows_ref[0] + w[:, 1:2] * rows_ref[1]
    o_ref[...] = _layer_norm(ALPHA * h_ref[...] + ffn, g2_ref[...], b2_ref[...])


def _combine(pad_start, eid3, rank3, h1, wts_t, g2, b2, ys):
    T, D = h1.shape
    grid_spec = pltpu.PrefetchScalarGridSpec(
        num_scalar_prefetch=1,
        grid=(T // ROW_TM,),
        in_specs=[
            pl.BlockSpec((1, EXPERT_TOPK, ROW_TM), lambda i, p: (i, 0, 0), memory_space=pltpu.SMEM),
            pl.BlockSpec((1, EXPERT_TOPK, ROW_TM), lambda i, p: (i, 0, 0), memory_space=pltpu.SMEM),
            pl.BlockSpec((ROW_TM, D), lambda i, p: (i, 0)),
            pl.BlockSpec((ROW_TM, EXPERT_TOPK), lambda i, p: (i, 0)),
            pl.BlockSpec((1, D), lambda i, p: (0, 0)),
            pl.BlockSpec((1, D), lambda i, p: (0, 0)),
            pl.BlockSpec(memory_space=pl.ANY),
        ],
        out_specs=pl.BlockSpec((ROW_TM, D), lambda i, p: (i, 0)),
        scratch_shapes=[pltpu.VMEM((EXPERT_TOPK, ROW_TM, D), F32), pltpu.SemaphoreType.DMA(())],
    )
    return pl.pallas_call(
        _combine_kernel,
        grid_spec=grid_spec,
        out_shape=jax.ShapeDtypeStruct((T, D), F32),
        compiler_params=_cparams(("arbitrary",)),
        name="combine",
    )(pad_start, eid3, rank3, h1, wts_t, g2, b2, ys)


def _attn_bias_tables():
    dj = np.arange(ATT_T, dtype=np.float64)[:, None]
    di = np.tile(np.arange(ATT_T, dtype=np.float64), 2)[None, :]
    slopes = (2.0 ** (-8.0 * np.arange(1, DIFF_HEADS + 1, dtype=np.float64) / DIFF_HEADS)) * LOG2E
    off_diag = np.broadcast_to(dj, (ATT_T, 2 * ATT_T))
    diag = di - np.abs(di - dj)
    visible = (dj // CHUNK) <= (di // CHUNK)
    sl = slopes[:, None, None]
    tables = np.stack([sl * off_diag[None], np.where(visible[None], sl * diag[None], MASKED),
                       np.full((DIFF_HEADS, ATT_T, 2 * ATT_T), MASKED)], axis=1)
    return jnp.asarray(slopes, F32), jnp.asarray(tables, F32)


def _layer(h, layer, w_in, b_in, sg_ln_g, sg_ln_b, sg_w, sg_b, w_branch_a, lam_q1, lam_k1, lam_q2, lam_k2,
           subln_g, w_branch_b, w_out, ln1_g, ln1_b, w_group, b_group, w_expert, b_expert,
           w_gate, w_up, w_down, ln2_g, ln2_b):
    B, S, D = h.shape
    T = B * S
    row = lambda a: a.reshape(1, -1)
    col = lambda a: a.reshape(-1, 1)

    wk = w_in[:, OFF_K:OFF_VAL].astype(BF16)
    wqt = w_in[:, OFF_Q:OFF_K].T.astype(BF16)
    wvt = w_in[:, OFF_VAL:OFF_GA].T.astype(BF16)
    k, qt, vt = _qkv_proj(h, wk, row(b_in[OFF_K:OFF_VAL]), wqt, col(b_in[OFF_Q:OFF_K]),
                          wvt, col(b_in[OFF_VAL:OFF_GA]))
    slopes, tables = _attn_bias_tables()
    lambda_init = 0.8 - 0.6 * math.exp(-0.3 * layer)
    lamv = jnp.stack([lam_q1, lam_k1, lam_q2, lam_k2]).astype(F32)
    attn = _diff_attention(qt, k, vt, slopes, tables, lamv, col(subln_g), lambda_init)

    sgb = jnp.repeat(sg_b.reshape(GMLP_GROUPS // 2, 2, GMLP_BLOCK), GMLP_GROUP_DIM, axis=1)
    sgb = jnp.transpose(sgb, (0, 2, 1))
    h1 = _mixer(h.reshape(T, D), attn.reshape(T, DIFF_WIDTH),
                w_in[:, OFF_U:OFF_Q].astype(BF16), row(b_in[OFF_U:OFF_Q]),
                w_in[:, OFF_GA:IN_WIDTH].astype(BF16), row(b_in[OFF_GA:IN_WIDTH]),
                row(sg_ln_g), row(sg_ln_b), sg_w, sgb,
                w_branch_a.astype(BF16), w_branch_b.astype(BF16), w_out.astype(BF16),
                row(ln1_g), row(ln1_b))

    wr = jnp.zeros((LANES, D), F32)
    wr = wr.at[0:N_GROUPS].set(w_group.T).at[8:8 + N_EXPERTS].set(w_expert.T)
    wr_hi = wr.astype(BF16)
    wr_lo = (wr - wr_hi.astype(F32)).astype(BF16)
    br = jnp.zeros((LANES, 1), F32)
    br = br.at[0:N_GROUPS, 0].set(b_group.astype(F32)).at[8:8 + N_EXPERTS, 0].set(b_expert.astype(F32))
    eid, wts = _router(h1, wr_hi, wr_lo, br)

    tri = jnp.asarray(np.triu(np.ones((ROUTE_TM, ROUTE_TM), np.float32), k=1), BF16)
    rank, counts = _ranks(eid, tri)

    counts = counts[:, 0]
    padded = (counts + DISPATCH_BLOCK - 1) // DISPATCH_BLOCK * DISPATCH_BLOCK
    pad_end = jnp.cumsum(padded)
    pad_start = (pad_end - padded).astype(jnp.int32)
    P = T * EXPERT_TOPK + N_EXPERTS * DISPATCH_BLOCK
    n_blocks = P // DISPATCH_BLOCK
    block_start = jnp.arange(n_blocks, dtype=jnp.int32) * DISPATCH_BLOCK
    block_e = jnp.minimum(jnp.sum(pad_end[None, :] <= block_start[:, None], axis=1), N_EXPERTS - 1).astype(jnp.int32)
    n_used = (pad_end[-1:] // DISPATCH_BLOCK).astype(jnp.int32)

    tile3 = lambda a: jnp.transpose(a.reshape(EXPERT_TOPK, T // ROW_TM, ROW_TM), (1, 0, 2))
    eid3, rank3 = tile3(eid), tile3(rank)
    xs = _dispatch(pad_start, eid3, rank3, h1, jnp.zeros((P, D), F32))
    ys = _experts(block_e, n_used, xs, w_gate.astype(BF16), w_up.astype(BF16), w_down.astype(BF16))
    out = _combine(pad_start, eid3, rank3, h1, wts.T, row(ln2_g), row(ln2_b), ys)
    return out.reshape(B, S, D)


def kernel(x, w_in, b_in, sg_ln_g, sg_ln_b, sg_w, sg_b, w_branch_a, lam_q1, lam_k1, lam_q2, lam_k2, subln_g,
           w_branch_b, w_out, ln1_g, ln1_b, w_group, b_group, w_expert, b_expert, w_gate, w_up, w_down,
           ln2_g, ln2_b):
    h = x
    for layer in range(DEPTH):
        h = _layer(h, layer, w_in[layer], b_in[layer], sg_ln_g[layer], sg_ln_b[layer], sg_w[layer],
                   sg_b[layer], w_branch_a[layer], lam_q1[layer], lam_k1[layer], lam_q2[layer],
                   lam_k2[layer], subln_g[layer], w_branch_b[layer], w_out[layer], ln1_g[layer],
                   ln1_b[layer], w_group[layer], b_group[layer], w_expert[layer], b_expert[layer],
                   w_gate[layer], w_up[layer], w_down[layer], ln2_g[layer], ln2_b[layer])
    return h
```

```python
import functools
import math

import jax
import jax.numpy as jnp
import numpy as np
from jax import lax
from jax.experimental import pallas as pl
from jax.experimental.pallas import tpu as pltpu

D_MODEL = 1024
DEPTH = 1
CHUNK = 64
GMLP_GROUPS = 8
GMLP_GROUP_DIM = 64
GMLP_WIDTH = GMLP_GROUPS * GMLP_GROUP_DIM
GMLP_BLOCK = 128
DIFF_HEADS = 8
DIFF_QK_DIM = 64
DIFF_V_DIM = 2 * DIFF_QK_DIM
DIFF_QK_WIDTH = DIFF_HEADS * 2 * DIFF_QK_DIM
DIFF_WIDTH = DIFF_HEADS * DIFF_V_DIM
N_GROUPS = 4
EXPERTS_PER_GROUP = 8
N_EXPERTS = N_GROUPS * EXPERTS_PER_GROUP
EXPERT_TOPK = 2
EXPERT_HIDDEN = 512
DISPATCH_BLOCK = 256
ALPHA = (2.0 * DEPTH) ** 0.25
LN_EPS = 1e-5
RMS_EPS = 1e-5
OFF_U = 0
OFF_V = OFF_U + GMLP_WIDTH
OFF_Q = OFF_V + GMLP_WIDTH
OFF_K = OFF_Q + DIFF_QK_WIDTH
OFF_VAL = OFF_K + DIFF_QK_WIDTH
OFF_GA = OFF_VAL + DIFF_WIDTH
OFF_GB = OFF_GA + D_MODEL
IN_WIDTH = OFF_GB + D_MODEL

LANES = 128
HEAD_W = 2 * DIFF_QK_DIM
MASKED = -1e30
VMEM_LIMIT = 56 * 1024 * 1024

PROJ_TM = 512
ATT_T = 256
ATT_TK = 2 * ATT_T
ATT_HB = 2
LOG2E = math.log2(math.e)
VT_ROWS = HEAD_W + 16
MIX_TM = 256
ROUTE_TM = 512
ROW_TM = 256
ROW_UNROLL = 8

BF16 = jnp.bfloat16
F32 = jnp.float32


def _cparams(sem):
    return pltpu.CompilerParams(dimension_semantics=sem, vmem_limit_bytes=VMEM_LIMIT)


def _dot(a, b):
    return jnp.dot(a, b, preferred_element_type=F32)


def _dot_nt(a, b):
    return lax.dot_general(a, b, (((1,), (1,)), ((), ())), preferred_element_type=F32)


def _gelu(x):
    return 0.5 * x * (1.0 + lax.erf(x * (2.0 ** -0.5)))


def _sigmoid(x):
    return 1.0 / (1.0 + jnp.exp(-x))


def _layer_norm(x, g, b):
    mu = jnp.mean(x, axis=-1, keepdims=True)
    xc = x - mu
    var = jnp.mean(xc * xc, axis=-1, keepdims=True)
    return xc * lax.rsqrt(var + LN_EPS) * g + b


def _qkv_kernel(x_ref, wk_ref, bk_ref, wqt_ref, bq_ref, wvt_ref, bv_ref, k_ref, qt_ref, vt_ref):
    xb = x_ref[...].astype(BF16)
    k_ref[...] = (_dot(xb, wk_ref[...]) + bk_ref[...]).astype(BF16)
    qt = (_dot_nt(wqt_ref[...], xb) + bq_ref[...]) * (DIFF_QK_DIM ** -0.5 * LOG2E)
    qt_ref[...] = qt.astype(BF16)
    vt = (_dot_nt(wvt_ref[...], xb) + bv_ref[...]).astype(BF16)
    ones_row = (lax.broadcasted_iota(jnp.int32, (VT_ROWS - HEAD_W, ATT_TK), 0) == 0).astype(BF16)
    for hh in range(DIFF_HEADS):
        for j in range(PROJ_TM // ATT_TK):
            vt_ref[hh, j, :HEAD_W] = vt[hh * HEAD_W:(hh + 1) * HEAD_W, j * ATT_TK:(j + 1) * ATT_TK]
            vt_ref[hh, j, HEAD_W:] = ones_row


def _qkv_proj(x, wk, bk, wqt, bq, wvt, bv):
    B, S, D = x.shape
    nk = S // ATT_TK
    jt = PROJ_TM // ATT_TK
    const2 = lambda b, i: (0, 0)
    return pl.pallas_call(
        _qkv_kernel,
        grid=(B, S // PROJ_TM),
        in_specs=[
            pl.BlockSpec((None, PROJ_TM, D), lambda b, i: (b, i, 0)),
            pl.BlockSpec((D, DIFF_QK_WIDTH), const2),
            pl.BlockSpec((1, DIFF_QK_WIDTH), const2),
            pl.BlockSpec((DIFF_QK_WIDTH, D), const2),
            pl.BlockSpec((DIFF_QK_WIDTH, 1), const2),
            pl.BlockSpec((DIFF_WIDTH, D), const2),
            pl.BlockSpec((DIFF_WIDTH, 1), const2),
        ],
        out_specs=[
            pl.BlockSpec((None, PROJ_TM, DIFF_QK_WIDTH), lambda b, i: (b, i, 0)),
            pl.BlockSpec((None, DIFF_QK_WIDTH, PROJ_TM), lambda b, i: (b, 0, i)),
            pl.BlockSpec((None, DIFF_HEADS, jt, VT_ROWS, ATT_TK), lambda b, i: (b, 0, i, 0, 0)),
        ],
        out_shape=[
            jax.ShapeDtypeStruct((B, S, DIFF_QK_WIDTH), BF16),
            jax.ShapeDtypeStruct((B, DIFF_QK_WIDTH, S), BF16),
            jax.ShapeDtypeStruct((B, DIFF_HEADS, nk, VT_ROWS, ATT_TK), BF16),
        ],
        compiler_params=_cparams(("arbitrary", "arbitrary")),
        name="qkv_proj",
    )(x, wk, bk, wqt, bq, wvt, bv)


def _attn_kernel(slope_ref, qt_ref, k_ref, vt_ref, tab_ref, lamv_ref, g_ref, o_ref,
                 q12_ref, sa_ref, sb_ref, ta_ref, tb_ref, m_ref, acc_ref, *, lambda_init):
    hg = pl.program_id(1)
    qi = pl.program_id(2)
    T = ATT_T
    n_groups = qi // 2 + 1

    row = lax.broadcasted_iota(jnp.int32, (HEAD_W, T), 0)
    zero = jnp.zeros((HEAD_W, T), BF16)
    for e in range(ATT_HB):
        q = qt_ref[e * HEAD_W:(e + 1) * HEAD_W, :]
        q12_ref[e, :, :T] = jnp.where(row < DIFF_QK_DIM, q, zero)
        q12_ref[e, :, T:] = jnp.where(row >= DIFF_QK_DIM, q, zero)
    acc_ref[...] = jnp.zeros_like(acc_ref)
    m_ref[...] = jnp.full(m_ref.shape, MASKED, F32)

    def scores(g, s_ref, t_ref):
        k0 = 2 * g
        tab0 = jnp.where(k0 == qi, 1, 0)
        tab1 = jnp.where(k0 + 1 == qi, 1, jnp.where(k0 + 1 > qi, 2, 0))
        for e in range(ATT_HB):
            kt = k_ref[pl.ds(pl.multiple_of(k0 * T, 2 * T), 2 * T), e * HEAD_W:(e + 1) * HEAD_W]
            s = _dot(kt, q12_ref[e])
            t0 = s[:T] + tab_ref[e, tab0]
            t1 = s[T:] + tab_ref[e, tab1]
            s_ref[e, :T] = t0
            s_ref[e, T:] = t1
            t_ref[e, 0:1, :] = jnp.max(t0, axis=0, keepdims=True)
            t_ref[e, 1:2, :] = jnp.max(t1, axis=0, keepdims=True)

    def accumulate(g, s_ref, t_ref):
        j0 = (2 * g * T).astype(F32)
        for e in range(ATT_HB):
            slope = slope_ref[hg * ATT_HB + e]
            c0 = slope * j0
            c1 = slope * (j0 + T)
            m = m_ref[e]
            m_new = jnp.maximum(m, jnp.maximum(t_ref[e, 0:1, :] + c0, t_ref[e, 1:2, :] + c1))
            p0 = jnp.exp2(s_ref[e, :T] - (m_new - c0)).astype(BF16)
            p1 = jnp.exp2(s_ref[e, T:] - (m_new - c1)).astype(BF16)
            m_ref[e] = m_new
            p = jnp.concatenate([p0, p1], axis=0)
            acc_ref[e] = jnp.exp2(m - m_new) * acc_ref[e] + _dot(vt_ref[e, g], p)

    scores(0, sa_ref, ta_ref)

    def pair(i, _):
        g = 2 * i
        scores(g + 1, sb_ref, tb_ref)
        accumulate(g, sa_ref, ta_ref)
        scores(jnp.minimum(g + 2, n_groups - 1), sa_ref, ta_ref)
        accumulate(g + 1, sb_ref, tb_ref)
        return 0

    lax.fori_loop(0, n_groups // 2, pair, 0)

    @pl.when(n_groups % 2 == 1)
    def _():
        accumulate(n_groups - 1, sa_ref, ta_ref)

    lv = lamv_ref[...]
    lam = (jnp.exp(jnp.sum(lv[0:1] * lv[1:2], axis=1, keepdims=True))
           - jnp.exp(jnp.sum(lv[2:3] * lv[3:4], axis=1, keepdims=True)) + lambda_init)
    for e in range(ATT_HB):
        o12 = acc_ref[e, :HEAD_W] * (1.0 / acc_ref[e, HEAD_W:HEAD_W + 1])
        o = o12[:, :T] - lam * o12[:, T:]
        o = o * lax.rsqrt(jnp.mean(o * o, axis=0, keepdims=True) + RMS_EPS)
        o = o * g_ref[...] * (1.0 - lambda_init)
        o_ref[:, e * HEAD_W:(e + 1) * HEAD_W] = o.T.astype(BF16)


def _diff_attention(qt, k, vt, slopes, tables, lamv, gcol, lambda_init):
    B, S, _ = k.shape
    nq = S // ATT_T
    gw = ATT_HB * HEAD_W
    grid_spec = pltpu.PrefetchScalarGridSpec(
        num_scalar_prefetch=1,
        grid=(B, DIFF_HEADS // ATT_HB, nq),
        in_specs=[
            pl.BlockSpec((None, gw, ATT_T), lambda b, h, i, s: (b, h, i)),
            pl.BlockSpec((None, S, gw), lambda b, h, i, s: (b, 0, h)),
            pl.BlockSpec((None, ATT_HB, S // ATT_TK, VT_ROWS, ATT_TK), lambda b, h, i, s: (b, h, 0, 0, 0)),
            pl.BlockSpec((ATT_HB, 3, ATT_T, 2 * ATT_T), lambda b, h, i, s: (h, 0, 0, 0)),
            pl.BlockSpec((4, DIFF_QK_DIM), lambda b, h, i, s: (0, 0)),
            pl.BlockSpec((HEAD_W, 1), lambda b, h, i, s: (0, 0)),
        ],
        out_specs=pl.BlockSpec((None, ATT_T, gw), lambda b, h, i, s: (b, i, h)),
        scratch_shapes=[
            pltpu.VMEM((ATT_HB, HEAD_W, 2 * ATT_T), BF16),
            pltpu.VMEM((ATT_HB, ATT_TK, 2 * ATT_T), F32),
            pltpu.VMEM((ATT_HB, ATT_TK, 2 * ATT_T), F32),
            pltpu.VMEM((ATT_HB, 2, 2 * ATT_T), F32),
            pltpu.VMEM((ATT_HB, 2, 2 * ATT_T), F32),
            pltpu.VMEM((ATT_HB, 1, 2 * ATT_T), F32),
            pltpu.VMEM((ATT_HB, VT_ROWS, 2 * ATT_T), F32),
        ],
    )
    return pl.pallas_call(
        functools.partial(_attn_kernel, lambda_init=lambda_init),
        grid_spec=grid_spec,
        out_shape=jax.ShapeDtypeStruct((B, S, DIFF_WIDTH), BF16),
        compiler_params=_cparams(("arbitrary", "arbitrary", "arbitrary")),
        name="diff_attn",
    )(slopes, qt, k, vt, tables, lamv, gcol)


def _mixer_kernel(x_ref, attn_ref, wuv_ref, buv_ref, wg_ref, bg_ref, lng_ref, lnb_ref, sgw_ref, sgb_ref,
                  wa_ref, wb_ref, wo_ref, g1_ref, b1_ref, h_ref):
    x = x_ref[...]
    xb = x.astype(BF16)
    uv = _gelu(_dot(xb, wuv_ref[...]) + buv_ref[...])
    u = uv[:, :GMLP_WIDTH]
    v = _layer_norm(uv[:, GMLP_WIDTH:], lng_ref[...], lnb_ref[...]).astype(BF16)

    ti = lax.broadcasted_iota(jnp.int32, (GMLP_BLOCK, GMLP_BLOCK), 0) // CHUNK
    si = lax.broadcasted_iota(jnp.int32, (GMLP_BLOCK, GMLP_BLOCK), 1) // CHUNK
    causal = ti >= si
    w = [jnp.where(causal, sgw_ref[g], 0.0).astype(BF16) for g in range(GMLP_GROUPS)]
    lane = lax.broadcasted_iota(jnp.int32, (GMLP_BLOCK, LANES), 1)
    lo = lane < GMLP_GROUP_DIM
    vzero = jnp.zeros((GMLP_BLOCK, LANES), BF16)

    blocks = []
    for n in range(MIX_TM // GMLP_BLOCK):
        slabs = []
        for p in range(GMLP_WIDTH // LANES):
            vs = v[n * GMLP_BLOCK:(n + 1) * GMLP_BLOCK, p * LANES:(p + 1) * LANES]
            mixed = (_dot(w[2 * p], jnp.where(lo, vs, vzero))
                     + _dot(w[2 * p + 1], jnp.where(lo, vzero, vs)) + sgb_ref[p])
            slabs.append(mixed)
        blocks.append(jnp.concatenate(slabs, axis=1))
    mixed = jnp.concatenate(blocks, axis=0)

    y_a = _dot((u * mixed).astype(BF16), wa_ref[...])
    y_b = _dot(attn_ref[...], wb_ref[...])
    gates = _sigmoid(_dot(xb, wg_ref[...]) + bg_ref[...])
    z = gates[:, :D_MODEL] * y_a + gates[:, D_MODEL:] * y_b
    mix = _dot(z.astype(BF16), wo_ref[...])
    h_ref[...] = _layer_norm(ALPHA * x + mix, g1_ref[...], b1_ref[...])


def _mixer(x2, attn2, wuv, buv, wg, bg, lng, lnb, sgw, sgb, wa, wb, wo, g1, b1):
    T, D = x2.shape
    c2 = lambda i: (0, 0)
    c3 = lambda i: (0, 0, 0)
    full = lambda a: pl.BlockSpec(a.shape, c2 if a.ndim == 2 else c3)
    return pl.pallas_call(
        _mixer_kernel,
        grid=(T // MIX_TM,),
        in_specs=[pl.BlockSpec((MIX_TM, D), lambda i: (i, 0)),
                  pl.BlockSpec((MIX_TM, DIFF_WIDTH), lambda i: (i, 0))]
                 + [full(a) for a in (wuv, buv, wg, bg, lng, lnb, sgw, sgb, wa, wb, wo, g1, b1)],
        out_specs=pl.BlockSpec((MIX_TM, D), lambda i: (i, 0)),
        out_shape=jax.ShapeDtypeStruct((T, D), F32),
        compiler_params=_cparams(("arbitrary",)),
        name="mixer",
    )(x2, attn2, wuv, buv, wg, bg, lng, lnb, sgw, sgb, wa, wb, wo, g1, b1)


def _router_kernel(h_ref, wr_hi_ref, wr_lo_ref, br_ref, eid_ref, wts_ref):
    hf = h_ref[...]
    hi = hf.astype(BF16)
    lo = (hf - hi.astype(F32)).astype(BF16)
    logits = (_dot_nt(wr_hi_ref[...], hi) + _dot_nt(wr_lo_ref[...], hi)
              + _dot_nt(wr_hi_ref[...], lo)) + br_ref[...]
    gidx = lax.broadcasted_iota(jnp.int32, (8, ROUTE_TM), 0)
    gl = jnp.where(gidx < N_GROUPS, logits[0:8], -jnp.inf)
    gmax = jnp.max(gl, axis=0, keepdims=True)
    g_sel = jnp.min(jnp.where(gl == gmax, gidx, N_GROUPS), axis=0, keepdims=True)
    g_w = 1.0 / jnp.sum(jnp.exp(gl - gmax), axis=0, keepdims=True)

    el = logits[8:8 + EXPERTS_PER_GROUP]
    for g in range(1, N_GROUPS):
        el = jnp.where(g_sel == g, logits[8 + g * EXPERTS_PER_GROUP:8 + (g + 1) * EXPERTS_PER_GROUP], el)
    eidx = lax.broadcasted_iota(jnp.int32, el.shape, 0)
    v1 = jnp.max(el, axis=0, keepdims=True)
    i1 = jnp.min(jnp.where(el == v1, eidx, EXPERTS_PER_GROUP), axis=0, keepdims=True)
    rest = jnp.where(eidx == i1, -jnp.inf, el)
    v2 = jnp.max(rest, axis=0, keepdims=True)
    i2 = jnp.min(jnp.where(rest == v2, eidx, EXPERTS_PER_GROUP), axis=0, keepdims=True)
    e21 = jnp.exp(v2 - v1)
    w1 = g_w / (1.0 + e21)
    w2 = g_w * e21 / (1.0 + e21)
    base = g_sel * EXPERTS_PER_GROUP
    eid_ref[0:1, :] = base + i1
    eid_ref[1:2, :] = base + i2
    wts_ref[0:1, :] = w1
    wts_ref[1:2, :] = w2


def _router(h1, wr_hi, wr_lo, br):
    T, D = h1.shape
    c2 = lambda i: (0, 0)
    return pl.pallas_call(
        _router_kernel,
        grid=(T // ROUTE_TM,),
        in_specs=[pl.BlockSpec((ROUTE_TM, D), lambda i: (i, 0)),
                  pl.BlockSpec(wr_hi.shape, c2), pl.BlockSpec(wr_lo.shape, c2), pl.BlockSpec(br.shape, c2)],
        out_specs=[pl.BlockSpec((EXPERT_TOPK, ROUTE_TM), lambda i: (0, i)),
                   pl.BlockSpec((EXPERT_TOPK, ROUTE_TM), lambda i: (0, i))],
        out_shape=[jax.ShapeDtypeStruct((EXPERT_TOPK, T), jnp.int32),
                   jax.ShapeDtypeStruct((EXPERT_TOPK, T), F32)],
        compiler_params=_cparams(("arbitrary",)),
        name="router",
    )(h1, wr_hi, wr_lo, br)


def _rank_kernel(eid_ref, tri_ref, rank_ref, cnt_ref, run_ref):
    @pl.when(pl.program_id(0) == 0)
    def _():
        run_ref[...] = jnp.zeros_like(run_ref)

    eid = eid_ref[...]
    eiota = lax.broadcasted_iota(jnp.int32, (N_EXPERTS, ROUTE_TM), 0)
    run = run_ref[...]
    ranks = []
    for c in range(EXPERT_TOPK):
        hot = eiota == eid[c:c + 1]
        hotf = hot.astype(F32)
        before = _dot(hotf.astype(BF16), tri_ref[...])
        ranks.append(jnp.sum(jnp.where(hot, run + before, 0.0), axis=0, keepdims=True))
        run = run + jnp.sum(hotf, axis=1, keepdims=True)
    run_ref[...] = run
    for c in range(EXPERT_TOPK):
        rank_ref[c:c + 1, :] = ranks[c].astype(jnp.int32)
    cnt_ref[...] = run.astype(jnp.int32)


def _ranks(eid, tri):
    _, T = eid.shape
    return pl.pallas_call(
        _rank_kernel,
        grid=(T // ROUTE_TM,),
        in_specs=[pl.BlockSpec((EXPERT_TOPK, ROUTE_TM), lambda i: (0, i)),
                  pl.BlockSpec(tri.shape, lambda i: (0, 0))],
        out_specs=[pl.BlockSpec((EXPERT_TOPK, ROUTE_TM), lambda i: (0, i)),
                   pl.BlockSpec((N_EXPERTS, 1), lambda i: (0, 0))],
        out_shape=[jax.ShapeDtypeStruct((EXPERT_TOPK, T), jnp.int32),
                   jax.ShapeDtypeStruct((N_EXPERTS, 1), jnp.int32)],
        scratch_shapes=[pltpu.VMEM((N_EXPERTS, 1), F32)],
        compiler_params=_cparams(("arbitrary",)),
        name="rank",
    )(eid, tri)


def _row_copies(pad_ref, eid_ref, rank_ref, make_copy):
    def issue(t, _):
        for c in range(EXPERT_TOPK):
            dest = pad_ref[eid_ref[0, c, t]] + rank_ref[0, c, t]
            make_copy(c, t, dest).start(priority=c)
        return 0

    lax.fori_loop(0, ROW_TM, issue, 0, unroll=ROW_UNROLL)

    def drain(t, _):
        for c in range(EXPERT_TOPK):
            make_copy(c, t, 0).wait()
        return 0

    lax.fori_loop(0, ROW_TM, drain, 0, unroll=ROW_UNROLL)


def _dispatch_kernel(pad_ref, eid_ref, rank_ref, h_ref, xs_in_ref, xs_ref, sem):
    del xs_in_ref

    def make_copy(c, t, dest):
        return pltpu.make_async_copy(h_ref.at[pl.ds(t, 1)], xs_ref.at[pl.ds(dest, 1)], sem)

    _row_copies(pad_ref, eid_ref, rank_ref, make_copy)


def _dispatch(pad_start, eid3, rank3, h1, xs_zero):
    T, D = h1.shape
    grid_spec = pltpu.PrefetchScalarGridSpec(
        num_scalar_prefetch=1,
        grid=(T // ROW_TM,),
        in_specs=[
            pl.BlockSpec((1, EXPERT_TOPK, ROW_TM), lambda i, p: (i, 0, 0), memory_space=pltpu.SMEM),
            pl.BlockSpec((1, EXPERT_TOPK, ROW_TM), lambda i, p: (i, 0, 0), memory_space=pltpu.SMEM),
            pl.BlockSpec((ROW_TM, D), lambda i, p: (i, 0)),
            pl.BlockSpec(memory_space=pl.ANY),
        ],
        out_specs=pl.BlockSpec(memory_space=pl.ANY),
        scratch_shapes=[pltpu.SemaphoreType.DMA(())],
    )
    return pl.pallas_call(
        _dispatch_kernel,
        grid_spec=grid_spec,
        out_shape=jax.ShapeDtypeStruct(xs_zero.shape, xs_zero.dtype),
        input_output_aliases={4: 0},
        compiler_params=_cparams(("arbitrary",)),
        name="dispatch",
    )(pad_start, eid3, rank3, h1, xs_zero)


def _expert_kernel(be_ref, nb_ref, xs_ref, wg_ref, wu_ref, wd_ref, o_ref):
    i = pl.program_id(0)

    @pl.when(i < nb_ref[0])
    def _():
        xb = xs_ref[...].astype(BF16)
        gate = _dot(xb, wg_ref[...])
        up = _dot(xb, wu_ref[...])
        hb = (gate * _sigmoid(gate) * up).astype(BF16)
        o_ref[...] = _dot(hb, wd_ref[...])

    @pl.when(i >= nb_ref[0])
    def _():
        o_ref[...] = jnp.zeros_like(o_ref)


def _experts(block_e, n_used, xs, wg, wu, wd):
    P, D = xs.shape
    grid_spec = pltpu.PrefetchScalarGridSpec(
        num_scalar_prefetch=2,
        grid=(P // DISPATCH_BLOCK,),
        in_specs=[
            pl.BlockSpec((DISPATCH_BLOCK, D), lambda i, be, nb: (i, 0)),
            pl.BlockSpec((None, D, EXPERT_HIDDEN), lambda i, be, nb: (be[i], 0, 0)),
            pl.BlockSpec((None, D, EXPERT_HIDDEN), lambda i, be, nb: (be[i], 0, 0)),
            pl.BlockSpec((None, EXPERT_HIDDEN, D), lambda i, be, nb: (be[i], 0, 0)),
        ],
        out_specs=pl.BlockSpec((DISPATCH_BLOCK, D), lambda i, be, nb: (i, 0)),
    )
    return pl.pallas_call(
        _expert_kernel,
        grid_spec=grid_spec,
        out_shape=jax.ShapeDtypeStruct((P, D), F32),
        compiler_params=_cparams(("arbitrary",)),
        name="experts",
    )(block_e, n_used, xs, wg, wu, wd)


def _combine_kernel(pad_ref, eid_ref, rank_ref, h_ref, wts_ref, g2_ref, b2_ref, ys_ref, o_ref, rows_ref, sem):
    def make_copy(c, t, dest):
        return pltpu.make_async_copy(ys_ref.at[pl.ds(dest, 1)], rows_ref.at[c, pl.ds(t, 1)], sem)

    _row_copies(pad_ref, eid_ref, rank_ref, make_copy)
    w = wts_ref[...]
    ffn = w[:, 0:1] * rows_ref[0] + w[:, 1:2] * rows_ref[1]
    o_ref[...] = _layer_norm(ALPHA * h_ref[...] + ffn, g2_ref[...], b2_ref[...])


def _combine(pad_start, eid3, rank3, h1, wts_t, g2, b2, ys):
    T, D = h1.shape
    grid_spec = pltpu.PrefetchScalarGridSpec(
        num_scalar_prefetch=1,
        grid=(T // ROW_TM,),
        in_specs=[
            pl.BlockSpec((1, EXPERT_TOPK, ROW_TM), lambda i, p: (i, 0, 0), memory_space=pltpu.SMEM),
            pl.BlockSpec((1, EXPERT_TOPK, ROW_TM), lambda i, p: (i, 0, 0), memory_space=pltpu.SMEM),
            pl.BlockSpec((ROW_TM, D), lambda i, p: (i, 0)),
            pl.BlockSpec((ROW_TM, EXPERT_TOPK), lambda i, p: (i, 0)),
            pl.BlockSpec((1, D), lambda i, p: (0, 0)),
            pl.BlockSpec((1, D), lambda i, p: (0, 0)),
            pl.BlockSpec(memory_space=pl.ANY),
        ],
        out_specs=pl.BlockSpec((ROW_TM, D), lambda i, p: (i, 0)),
        scratch_shapes=[pltpu.VMEM((EXPERT_TOPK, ROW_TM, D), F32), pltpu.SemaphoreType.DMA(())],
    )
    return pl.pallas_call(
        _combine_kernel,
        grid_spec=grid_spec,
        out_shape=jax.ShapeDtypeStruct((T, D), F32),
        compiler_params=_cparams(("arbitrary",)),
        name="combine",
    )(pad_start, eid3, rank3, h1, wts_t, g2, b2, ys)


def _attn_bias_tables():
    dj = np.arange(ATT_T, dtype=np.float64)[:, None]
    di = np.tile(np.arange(ATT_T, dtype=np.float64), 2)[None, :]
    slopes = (2.0 ** (-8.0 * np.arange(1, DIFF_HEADS + 1, dtype=np.float64) / DIFF_HEADS)) * LOG2E
    off_diag = np.broadcast_to(dj, (ATT_T, 2 * ATT_T))
    diag = di - np.abs(di - dj)
    visible = (dj // CHUNK) <= (di // CHUNK)
    sl = slopes[:, None, None]
    tables = np.stack([sl * off_diag[None], np.where(visible[None], sl * diag[None], MASKED),
                       np.full((DIFF_HEADS, ATT_T, 2 * ATT_T), MASKED)], axis=1)
    return jnp.asarray(slopes, F32), jnp.asarray(tables, F32)


def _layer(h, layer, w_in, b_in, sg_ln_g, sg_ln_b, sg_w, sg_b, w_branch_a, lam_q1, lam_k1, lam_q2, lam_k2,
           subln_g, w_branch_b, w_out, ln1_g, ln1_b, w_group, b_group, w_expert, b_expert,
           w_gate, w_up, w_down, ln2_g, ln2_b):
    B, S, D = h.shape
    T = B * S
    row = lambda a: a.reshape(1, -1)
    col = lambda a: a.reshape(-1, 1)

    wk = w_in[:, OFF_K:OFF_VAL].astype(BF16)
    wqt = w_in[:, OFF_Q:OFF_K].T.astype(BF16)
    wvt = w_in[:, OFF_VAL:OFF_GA].T.astype(BF16)
    k, qt, vt = _qkv_proj(h, wk, row(b_in[OFF_K:OFF_VAL]), wqt, col(b_in[OFF_Q:OFF_K]),
                          wvt, col(b_in[OFF_VAL:OFF_GA]))
    slopes, tables = _attn_bias_tables()
    lambda_init = 0.8 - 0.6 * math.exp(-0.3 * layer)
    lamv = jnp.stack([lam_q1, lam_k1, lam_q2, lam_k2]).astype(F32)
    attn = _diff_attention(qt, k, vt, slopes, tables, lamv, col(subln_g), lambda_init)

    sgb = jnp.repeat(sg_b.reshape(GMLP_GROUPS // 2, 2, GMLP_BLOCK), GMLP_GROUP_DIM, axis=1)
    sgb = jnp.transpose(sgb, (0, 2, 1))
    h1 = _mixer(h.reshape(T, D), attn.reshape(T, DIFF_WIDTH),
                w_in[:, OFF_U:OFF_Q].astype(BF16), row(b_in[OFF_U:OFF_Q]),
                w_in[:, OFF_GA:IN_WIDTH].astype(BF16), row(b_in[OFF_GA:IN_WIDTH]),
                row(sg_ln_g), row(sg_ln_b), sg_w, sgb,
                w_branch_a.astype(BF16), w_branch_b.astype(BF16), w_out.astype(BF16),
                row(ln1_g), row(ln1_b))

    wr = jnp.zeros((LANES, D), F32)
    wr = wr.at[0:N_GROUPS].set(w_group.T).at[8:8 + N_EXPERTS].set(w_expert.T)
    wr_hi = wr.astype(BF16)
    wr_lo = (wr - wr_hi.astype(F32)).astype(BF16)
    br = jnp.zeros((LANES, 1), F32)
    br = br.at[0:N_GROUPS, 0].set(b_group.astype(F32)).at[8:8 + N_EXPERTS, 0].set(b_expert.astype(F32))
    eid, wts = _router(h1, wr_hi, wr_lo, br)

    tri = jnp.asarray(np.triu(np.ones((ROUTE_TM, ROUTE_TM), np.float32), k=1), BF16)
    rank, counts = _ranks(eid, tri)

    counts = counts[:, 0]
    padded = (counts + DISPATCH_BLOCK - 1) // DISPATCH_BLOCK * DISPATCH_BLOCK
    pad_end = jnp.cumsum(padded)
    pad_start = (pad_end - padded).astype(jnp.int32)
    P = T * EXPERT_TOPK + N_EXPERTS * DISPATCH_BLOCK
    n_blocks = P // DISPATCH_BLOCK
    block_start = jnp.arange(n_blocks, dtype=jnp.int32) * DISPATCH_BLOCK
    block_e = jnp.minimum(jnp.sum(pad_end[None, :] <= block_start[:, None], axis=1), N_EXPERTS - 1).astype(jnp.int32)
    n_used = (pad_end[-1:] // DISPATCH_BLOCK).astype(jnp.int32)

    tile3 = lambda a: jnp.transpose(a.reshape(EXPERT_TOPK, T // ROW_TM, ROW_TM), (1, 0, 2))
    eid3, rank3 = tile3(eid), tile3(rank)
    xs = _dispatch(pad_start, eid3, rank3, h1, jnp.zeros((P, D), F32))
    ys = _experts(block_e, n_used, xs, w_gate.astype(BF16), w_up.astype(BF16), w_down.astype(BF16))
    out = _combine(pad_start, eid3, rank3, h1, wts.T, row(ln2_g), row(ln2_b), ys)
    return out.reshape(B, S, D)


def kernel(x, w_in, b_in, sg_ln_g, sg_ln_b, sg_w, sg_b, w_branch_a, lam_q1, lam_k1, lam_q2, lam_k2, subln_g,
           w_branch_b, w_out, ln1_g, ln1_b, w_group, b_group, w_expert, b_expert, w_gate, w_up, w_down,
           ln2_g, ln2_b):
    h = x
    for layer in range(DEPTH):
        h = _layer(h, layer, w_in[layer], b_in[layer], sg_ln_g[layer], sg_ln_b[layer], sg_w[layer],
                   sg_b[layer], w_branch_a[layer], lam_q1[layer], lam_k1[layer], lam_q2[layer],
                   lam_k2[layer], subln_g[layer], w_branch_b[layer], w_out[layer], ln1_g[layer],
                   ln1_b[layer], w_group[layer], b_group[layer], w_expert[layer], b_expert[layer],
                   w_gate[layer], w_up[layer], w_down[layer], ln2_g[layer], ln2_b[layer])
    return h
```

```python
import functools
import math

import jax
import jax.numpy as jnp
import numpy as np
from jax import lax
from jax.experimental import pallas as pl
from jax.experimental.pallas import tpu as pltpu

D_MODEL = 1024
DEPTH = 1
CHUNK = 64
GMLP_GROUPS = 8
GMLP_GROUP_DIM = 64
GMLP_WIDTH = GMLP_GROUPS * GMLP_GROUP_DIM
GMLP_BLOCK = 128
DIFF_HEADS = 8
DIFF_QK_DIM = 64
DIFF_V_DIM = 2 * DIFF_QK_DIM
DIFF_QK_WIDTH = DIFF_HEADS * 2 * DIFF_QK_DIM
DIFF_WIDTH = DIFF_HEADS * DIFF_V_DIM
N_GROUPS = 4
EXPERTS_PER_GROUP = 8
N_EXPERTS = N_GROUPS * EXPERTS_PER_GROUP
EXPERT_TOPK = 2
EXPERT_HIDDEN = 512
DISPATCH_BLOCK = 256
ALPHA = (2.0 * DEPTH) ** 0.25
LN_EPS = 1e-5
RMS_EPS = 1e-5
OFF_U = 0
OFF_V = OFF_U + GMLP_WIDTH
OFF_Q = OFF_V + GMLP_WIDTH
OFF_K = OFF_Q + DIFF_QK_WIDTH
OFF_VAL = OFF_K + DIFF_QK_WIDTH
OFF_GA = OFF_VAL + DIFF_WIDTH
OFF_GB = OFF_GA + D_MODEL
IN_WIDTH = OFF_GB + D_MODEL

LANES = 128
HEAD_W = 2 * DIFF_QK_DIM
MASKED = -1e30
VMEM_LIMIT = 56 * 1024 * 1024

PROJ_TM = 512
ATT_T = 256
ATT_TK = 2 * ATT_T
ATT_TQ = ATT_TK
ATT_HB = 2
LOG2E = math.log2(math.e)
VT_ROWS = HEAD_W + 16
MIX_TM = 256
ROUTE_TM = 512
ROW_TM = 256
ROW_UNROLL = 8

BF16 = jnp.bfloat16
F32 = jnp.float32


def _cparams(sem):
    return pltpu.CompilerParams(dimension_semantics=sem, vmem_limit_bytes=VMEM_LIMIT)


def _dot(a, b):
    return jnp.dot(a, b, preferred_element_type=F32)


def _dot_nt(a, b):
    return lax.dot_general(a, b, (((1,), (1,)), ((), ())), preferred_element_type=F32)


def _gelu(x):
    return 0.5 * x * (1.0 + lax.erf(x * (2.0 ** -0.5)))


def _sigmoid(x):
    return 1.0 / (1.0 + jnp.exp(-x))


def _layer_norm(x, g, b):
    mu = jnp.mean(x, axis=-1, keepdims=True)
    xc = x - mu
    var = jnp.mean(xc * xc, axis=-1, keepdims=True)
    return xc * lax.rsqrt(var + LN_EPS) * g + b


def _qkv_kernel(x_ref, wk_ref, bk_ref, wqt_ref, bq_ref, wvt_ref, bv_ref, k_ref, qt_ref, vt_ref):
    xb = x_ref[...].astype(BF16)
    k_ref[...] = (_dot(xb, wk_ref[...]) + bk_ref[...]).astype(BF16)
    qt = (_dot_nt(wqt_ref[...], xb) + bq_ref[...]) * (DIFF_QK_DIM ** -0.5 * LOG2E)
    qt_ref[...] = qt.astype(BF16)
    vt = (_dot_nt(wvt_ref[...], xb) + bv_ref[...]).astype(BF16)
    ones_row = (lax.broadcasted_iota(jnp.int32, (VT_ROWS - HEAD_W, ATT_TK), 0) == 0).astype(BF16)
    for hh in range(DIFF_HEADS):
        for j in range(PROJ_TM // ATT_TK):
            vt_ref[hh, j, :HEAD_W] = vt[hh * HEAD_W:(hh + 1) * HEAD_W, j * ATT_TK:(j + 1) * ATT_TK]
            vt_ref[hh, j, HEAD_W:] = ones_row


def _qkv_proj(x, wk, bk, wqt, bq, wvt, bv):
    B, S, D = x.shape
    nk = S // ATT_TK
    jt = PROJ_TM // ATT_TK
    const2 = lambda b, i: (0, 0)
    return pl.pallas_call(
        _qkv_kernel,
        grid=(B, S // PROJ_TM),
        in_specs=[
            pl.BlockSpec((None, PROJ_TM, D), lambda b, i: (b, i, 0)),
            pl.BlockSpec((D, DIFF_QK_WIDTH), const2),
            pl.BlockSpec((1, DIFF_QK_WIDTH), const2),
            pl.BlockSpec((DIFF_QK_WIDTH, D), const2),
            pl.BlockSpec((DIFF_QK_WIDTH, 1), const2),
            pl.BlockSpec((DIFF_WIDTH, D), const2),
            pl.BlockSpec((DIFF_WIDTH, 1), const2),
        ],
        out_specs=[
            pl.BlockSpec((None, PROJ_TM, DIFF_QK_WIDTH), lambda b, i: (b, i, 0)),
            pl.BlockSpec((None, DIFF_QK_WIDTH, PROJ_TM), lambda b, i: (b, 0, i)),
            pl.BlockSpec((None, DIFF_HEADS, jt, VT_ROWS, ATT_TK), lambda b, i: (b, 0, i, 0, 0)),
        ],
        out_shape=[
            jax.ShapeDtypeStruct((B, S, DIFF_QK_WIDTH), BF16),
            jax.ShapeDtypeStruct((B, DIFF_QK_WIDTH, S), BF16),
            jax.ShapeDtypeStruct((B, DIFF_HEADS, nk, VT_ROWS, ATT_TK), BF16),
        ],
        compiler_params=_cparams(("arbitrary", "arbitrary")),
        name="qkv_proj",
    )(x, wk, bk, wqt, bq, wvt, bv)


def _attn_kernel(slope_ref, qt_ref, k_ref, vt_ref, tab_ref, lamv_ref, g_ref, o_ref,
                 q12_ref, sa_ref, sb_ref, ta_ref, tb_ref, m_ref, acc_ref, *, lambda_init):
    hg = pl.program_id(1)
    qi = pl.program_id(2)
    T = ATT_T
    TQ = ATT_TQ
    n_groups = qi + 1

    row = lax.broadcasted_iota(jnp.int32, (HEAD_W, TQ), 0)
    zero = jnp.zeros((HEAD_W, TQ), BF16)
    for e in range(ATT_HB):
        q = qt_ref[e * HEAD_W:(e + 1) * HEAD_W, :]
        q12_ref[e, :, :TQ] = jnp.where(row < DIFF_QK_DIM, q, zero)
        q12_ref[e, :, TQ:] = jnp.where(row >= DIFF_QK_DIM, q, zero)
    acc_ref[...] = jnp.zeros_like(acc_ref)
    m_ref[...] = jnp.full(m_ref.shape, MASKED, F32)

    def scores(g, s_ref, t_ref):
        k0 = 2 * g
        tab0 = jnp.where(g == qi, 1, 0)
        tab1 = jnp.where(g == qi, 2, 0)
        for e in range(ATT_HB):
            kt = k_ref[pl.ds(pl.multiple_of(k0 * T, 2 * T), 2 * T), e * HEAD_W:(e + 1) * HEAD_W]
            s = _dot(kt, q12_ref[e])
            t0 = s[:T] + tab_ref[e, tab0]
            t1 = s[T:] + tab_ref[e, tab1]
            s_ref[e, :T] = t0
            s_ref[e, T:] = t1
            t_ref[e, 0:1, :] = jnp.max(t0, axis=0, keepdims=True)
            t_ref[e, 1:2, :] = jnp.max(t1, axis=0, keepdims=True)

    def accumulate(g, s_ref, t_ref):
        j0 = (2 * g * T).astype(F32)
        for e in range(ATT_HB):
            slope = slope_ref[hg * ATT_HB + e]
            c0 = slope * j0
            c1 = slope * (j0 + T)
            m = m_ref[e]
            m_new = jnp.maximum(m, jnp.maximum(t_ref[e, 0:1, :] + c0, t_ref[e, 1:2, :] + c1))
            p0 = jnp.exp2(s_ref[e, :T] - (m_new - c0)).astype(BF16)
            p1 = jnp.exp2(s_ref[e, T:] - (m_new - c1)).astype(BF16)
            m_ref[e] = m_new
            p = jnp.concatenate([p0, p1], axis=0)
            acc_ref[e] = jnp.exp2(m - m_new) * acc_ref[e] + _dot(vt_ref[e, g], p)

    scores(0, sa_ref, ta_ref)

    def pair(i, _):
        g = 2 * i
        scores(g + 1, sb_ref, tb_ref)
        accumulate(g, sa_ref, ta_ref)
        scores(jnp.minimum(g + 2, n_groups - 1), sa_ref, ta_ref)
        accumulate(g + 1, sb_ref, tb_ref)
        return 0

    lax.fori_loop(0, n_groups // 2, pair, 0)

    @pl.when(n_groups % 2 == 1)
    def _():
        accumulate(n_groups - 1, sa_ref, ta_ref)

    lv = lamv_ref[...]
    lam = (jnp.exp(jnp.sum(lv[0:1] * lv[1:2], axis=1, keepdims=True))
           - jnp.exp(jnp.sum(lv[2:3] * lv[3:4], axis=1, keepdims=True)) + lambda_init)
    for e in range(ATT_HB):
        o12 = acc_ref[e, :HEAD_W] * (1.0 / acc_ref[e, HEAD_W:HEAD_W + 1])
        o = o12[:, :TQ] - lam * o12[:, TQ:]
        o = o * lax.rsqrt(jnp.mean(o * o, axis=0, keepdims=True) + RMS_EPS)
        o = o * g_ref[...] * (1.0 - lambda_init)
        o_ref[:, e * HEAD_W:(e + 1) * HEAD_W] = o.T.astype(BF16)


def _diff_attention(qt, k, vt, slopes, tables, lamv, gcol, lambda_init):
    B, S, _ = k.shape
    nq = S // ATT_TQ
    gw = ATT_HB * HEAD_W
    lanes = 2 * ATT_TQ
    grid_spec = pltpu.PrefetchScalarGridSpec(
        num_scalar_prefetch=1,
        grid=(B, DIFF_HEADS // ATT_HB, nq),
        in_specs=[
            pl.BlockSpec((None, gw, ATT_TQ), lambda b, h, i, s: (b, h, i)),
            pl.BlockSpec((None, S, gw), lambda b, h, i, s: (b, 0, h)),
            pl.BlockSpec((None, ATT_HB, S // ATT_TK, VT_ROWS, ATT_TK), lambda b, h, i, s: (b, h, 0, 0, 0)),
            pl.BlockSpec((ATT_HB, 3, ATT_T, lanes), lambda b, h, i, s: (h, 0, 0, 0)),
            pl.BlockSpec((4, DIFF_QK_DIM), lambda b, h, i, s: (0, 0)),
            pl.BlockSpec((HEAD_W, 1), lambda b, h, i, s: (0, 0)),
        ],
        out_specs=pl.BlockSpec((None, ATT_TQ, gw), lambda b, h, i, s: (b, i, h)),
        scratch_shapes=[
            pltpu.VMEM((ATT_HB, HEAD_W, lanes), BF16),
            pltpu.VMEM((ATT_HB, ATT_TK, lanes), F32),
            pltpu.VMEM((ATT_HB, ATT_TK, lanes), F32),
            pltpu.VMEM((ATT_HB, 2, lanes), F32),
            pltpu.VMEM((ATT_HB, 2, lanes), F32),
            pltpu.VMEM((ATT_HB, 1, lanes), F32),
            pltpu.VMEM((ATT_HB, VT_ROWS, lanes), F32),
        ],
    )
    return pl.pallas_call(
        functools.partial(_attn_kernel, lambda_init=lambda_init),
        grid_spec=grid_spec,
        out_shape=jax.ShapeDtypeStruct((B, S, DIFF_WIDTH), BF16),
        compiler_params=_cparams(("arbitrary", "arbitrary", "arbitrary")),
        name="diff_attn",
    )(slopes, qt, k, vt, tables, lamv, gcol)


def _mixer_kernel(x_ref, attn_ref, wuv_ref, buv_ref, wg_ref, bg_ref, lng_ref, lnb_ref, sgw_ref, sgb_ref,
                  wa_ref, wb_ref, wo_ref, g1_ref, b1_ref, h_ref):
    x = x_ref[...]
    xb = x.astype(BF16)
    uv = _gelu(_dot(xb, wuv_ref[...]) + buv_ref[...])
    u = uv[:, :GMLP_WIDTH]
    v = _layer_norm(uv[:, GMLP_WIDTH:], lng_ref[...], lnb_ref[...]).astype(BF16)

    ti = lax.broadcasted_iota(jnp.int32, (GMLP_BLOCK, GMLP_BLOCK), 0) // CHUNK
    si = lax.broadcasted_iota(jnp.int32, (GMLP_BLOCK, GMLP_BLOCK), 1) // CHUNK
    causal = ti >= si
    w = [jnp.where(causal, sgw_ref[g], 0.0).astype(BF16) for g in range(GMLP_GROUPS)]
    lane = lax.broadcasted_iota(jnp.int32, (GMLP_BLOCK, LANES), 1)
    lo = lane < GMLP_GROUP_DIM
    vzero = jnp.zeros((GMLP_BLOCK, LANES), BF16)

    blocks = []
    for n in range(MIX_TM // GMLP_BLOCK):
        slabs = []
        for p in range(GMLP_WIDTH // LANES):
            vs = v[n * GMLP_BLOCK:(n + 1) * GMLP_BLOCK, p * LANES:(p + 1) * LANES]
            mixed = (_dot(w[2 * p], jnp.where(lo, vs, vzero))
                     + _dot(w[2 * p + 1], jnp.where(lo, vzero, vs)) + sgb_ref[p])
            slabs.append(mixed)
        blocks.append(jnp.concatenate(slabs, axis=1))
    mixed = jnp.concatenate(blocks, axis=0)

    y_a = _dot((u * mixed).astype(BF16), wa_ref[...])
    y_b = _dot(attn_ref[...], wb_ref[...])
    gates = _sigmoid(_dot(xb, wg_ref[...]) + bg_ref[...])
    z = gates[:, :D_MODEL] * y_a + gates[:, D_MODEL:] * y_b
    mix = _dot(z.astype(BF16), wo_ref[...])
    h_ref[...] = _layer_norm(ALPHA * x + mix, g1_ref[...], b1_ref[...])


def _mixer(x2, attn2, wuv, buv, wg, bg, lng, lnb, sgw, sgb, wa, wb, wo, g1, b1):
    T, D = x2.shape
    c2 = lambda i: (0, 0)
    c3 = lambda i: (0, 0, 0)
    full = lambda a: pl.BlockSpec(a.shape, c2 if a.ndim == 2 else c3)
    return pl.pallas_call(
        _mixer_kernel,
        grid=(T // MIX_TM,),
        in_specs=[pl.BlockSpec((MIX_TM, D), lambda i: (i, 0)),
                  pl.BlockSpec((MIX_TM, DIFF_WIDTH), lambda i: (i, 0))]
                 + [full(a) for a in (wuv, buv, wg, bg, lng, lnb, sgw, sgb, wa, wb, wo, g1, b1)],
        out_specs=pl.BlockSpec((MIX_TM, D), lambda i: (i, 0)),
        out_shape=jax.ShapeDtypeStruct((T, D), F32),
        compiler_params=_cparams(("arbitrary",)),
        name="mixer",
    )(x2, attn2, wuv, buv, wg, bg, lng, lnb, sgw, sgb, wa, wb, wo, g1, b1)


def _router_kernel(h_ref, wr_hi_ref, wr_lo_ref, br_ref, eid_ref, wts_ref):
    hf = h_ref[...]
    hi = hf.astype(BF16)
    lo = (hf - hi.astype(F32)).astype(BF16)
    logits = (_dot_nt(wr_hi_ref[...], hi) + _dot_nt(wr_lo_ref[...], hi)
              + _dot_nt(wr_hi_ref[...], lo)) + br_ref[...]
    gidx = lax.broadcasted_iota(jnp.int32, (8, ROUTE_TM), 0)
    gl = jnp.where(gidx < N_GROUPS, logits[0:8], -jnp.inf)
    gmax = jnp.max(gl, axis=0, keepdims=True)
    g_sel = jnp.min(jnp.where(gl == gmax, gidx, N_GROUPS), axis=0, keepdims=True)
    g_w = 1.0 / jnp.sum(jnp.exp(gl - gmax), axis=0, keepdims=True)

    el = logits[8:8 + EXPERTS_PER_GROUP]
    for g in range(1, N_GROUPS):
        el = jnp.where(g_sel == g, logits[8 + g * EXPERTS_PER_GROUP:8 + (g + 1) * EXPERTS_PER_GROUP], el)
    eidx = lax.broadcasted_iota(jnp.int32, el.shape, 0)
    v1 = jnp.max(el, axis=0, keepdims=True)
    i1 = jnp.min(jnp.where(el == v1, eidx, EXPERTS_PER_GROUP), axis=0, keepdims=True)
    rest = jnp.where(eidx == i1, -jnp.inf, el)
    v2 = jnp.max(rest, axis=0, keepdims=True)
    i2 = jnp.min(jnp.where(rest == v2, eidx, EXPERTS_PER_GROUP), axis=0, keepdims=True)
    e21 = jnp.exp(v2 - v1)
    w1 = g_w / (1.0 + e21)
    w2 = g_w * e21 / (1.0 + e21)
    base = g_sel * EXPERTS_PER_GROUP
    eid_ref[0:1, :] = base + i1
    eid_ref[1:2, :] = base + i2
    wts_ref[0:1, :] = w1
    wts_ref[1:2, :] = w2


def _router(h1, wr_hi, wr_lo, br):
    T, D = h1.shape
    c2 = lambda i: (0, 0)
    return pl.pallas_call(
        _router_kernel,
        grid=(T // ROUTE_TM,),
        in_specs=[pl.BlockSpec((ROUTE_TM, D), lambda i: (i, 0)),
                  pl.BlockSpec(wr_hi.shape, c2), pl.BlockSpec(wr_lo.shape, c2), pl.BlockSpec(br.shape, c2)],
        out_specs=[pl.BlockSpec((EXPERT_TOPK, ROUTE_TM), lambda i: (0, i)),
                   pl.BlockSpec((EXPERT_TOPK, ROUTE_TM), lambda i: (0, i))],
        out_shape=[jax.ShapeDtypeStruct((EXPERT_TOPK, T), jnp.int32),
                   jax.ShapeDtypeStruct((EXPERT_TOPK, T), F32)],
        compiler_params=_cparams(("arbitrary",)),
        name="router",
    )(h1, wr_hi, wr_lo, br)


def _rank_kernel(eid_ref, tri_ref, rank_ref, cnt_ref, run_ref):
    @pl.when(pl.program_id(0) == 0)
    def _():
        run_ref[...] = jnp.zeros_like(run_ref)

    eid = eid_ref[...]
    eiota = lax.broadcasted_iota(jnp.int32, (N_EXPERTS, ROUTE_TM), 0)
    run = run_ref[...]
    ranks = []
    for c in range(EXPERT_TOPK):
        hot = eiota == eid[c:c + 1]
        hotf = hot.astype(F32)
        before = _dot(hotf.astype(BF16), tri_ref[...])
        ranks.append(jnp.sum(jnp.where(hot, run + before, 0.0), axis=0, keepdims=True))
        run = run + jnp.sum(hotf, axis=1, keepdims=True)
    run_ref[...] = run
    for c in range(EXPERT_TOPK):
        rank_ref[c:c + 1, :] = ranks[c].astype(jnp.int32)
    cnt_ref[...] = run.astype(jnp.int32)


def _ranks(eid, tri):
    _, T = eid.shape
    return pl.pallas_call(
        _rank_kernel,
        grid=(T // ROUTE_TM,),
        in_specs=[pl.BlockSpec((EXPERT_TOPK, ROUTE_TM), lambda i: (0, i)),
                  pl.BlockSpec(tri.shape, lambda i: (0, 0))],
        out_specs=[pl.BlockSpec((EXPERT_TOPK, ROUTE_TM), lambda i: (0, i)),
                   pl.BlockSpec((N_EXPERTS, 1), lambda i: (0, 0))],
        out_shape=[jax.ShapeDtypeStruct((EXPERT_TOPK, T), jnp.int32),
                   jax.ShapeDtypeStruct((N_EXPERTS, 1), jnp.int32)],
        scratch_shapes=[pltpu.VMEM((N_EXPERTS, 1), F32)],
        compiler_params=_cparams(("arbitrary",)),
        name="rank",
    )(eid, tri)


ROW_SUB = D_MODEL // LANES


def _row_copies(dest_refs, make_copy):
    def issue(t, _):
        for c in range(EXPERT_TOPK):
            make_copy(c, t, dest_refs[c][t]).start(priority=c)
        return 0

    lax.fori_loop(0, ROW_TM, issue, 0, unroll=ROW_UNROLL)

    def drain(t, _):
        for c in range(EXPERT_TOPK):
            make_copy(c, t, 0).wait()
        return 0

    lax.fori_loop(0, ROW_TM, drain, 0, unroll=ROW_UNROLL)


def _dispatch_kernel(cnt_ref, pad_ref, d0_ref, d1_ref, h_ref, xs_ref, rows_ref, zero_ref, sem, zsem):
    @pl.when(pl.program_id(0) == 0)
    def _():
        zero_ref[...] = jnp.zeros_like(zero_ref)

        def zero_copy(r):
            return pltpu.make_async_copy(zero_ref, xs_ref.at[r], zsem)

        def per_expert(e, n):
            lo = pad_ref[e] + cnt_ref[e]
            hi = pad_ref[e] + (cnt_ref[e] + DISPATCH_BLOCK - 1) // DISPATCH_BLOCK * DISPATCH_BLOCK

            def start(r, _):
                zero_copy(r).start()
                return 0

            lax.fori_loop(lo, hi, start, 0)
            return n + (hi - lo)

        n_pad = lax.fori_loop(0, N_EXPERTS, per_expert, 0)

        def wait(_, c):
            zero_copy(0).wait()
            return c

        lax.fori_loop(0, n_pad, wait, 0)

    rows_ref[...] = h_ref[...].reshape(ROW_TM, ROW_SUB, LANES)

    def make_copy(c, t, dest):
        return pltpu.make_async_copy(rows_ref.at[t], xs_ref.at[dest], sem)

    _row_copies((d0_ref, d1_ref), make_copy)


def _dispatch(counts, pad_start, dest, h1, n_rows):
    T, D = h1.shape
    grid_spec = pltpu.PrefetchScalarGridSpec(
        num_scalar_prefetch=2,
        grid=(T // ROW_TM,),
        in_specs=[
            pl.BlockSpec((ROW_TM,), lambda i, c, p: (i,), memory_space=pltpu.SMEM),
            pl.BlockSpec((ROW_TM,), lambda i, c, p: (i,), memory_space=pltpu.SMEM),
            pl.BlockSpec((ROW_TM, D), lambda i, c, p: (i, 0)),
        ],
        out_specs=pl.BlockSpec(memory_space=pl.ANY),
        scratch_shapes=[pltpu.VMEM((ROW_TM, ROW_SUB, LANES), F32), pltpu.VMEM((ROW_SUB, LANES), F32),
                        pltpu.SemaphoreType.DMA(()), pltpu.SemaphoreType.DMA(())],
    )
    return pl.pallas_call(
        _dispatch_kernel,
        grid_spec=grid_spec,
        out_shape=jax.ShapeDtypeStruct((n_rows, ROW_SUB, LANES), F32),
        compiler_params=_cparams(("arbitrary",)),
        name="dispatch",
    )(counts, pad_start, dest[0], dest[1], h1)


def _expert_kernel(be_ref, nb_ref, xs_ref, wg_ref, wu_ref, wd_ref, o_ref):
    i = pl.program_id(0)

    @pl.when(i < nb_ref[0])
    def _():
        xb = xs_ref[...].reshape(DISPATCH_BLOCK, D_MODEL).astype(BF16)
        gate = _dot(xb, wg_ref[...])
        up = _dot(xb, wu_ref[...])
        hb = (gate * _sigmoid(gate) * up).astype(BF16)
        o_ref[...] = _dot(hb, wd_ref[...]).reshape(DISPATCH_BLOCK, ROW_SUB, LANES)

    @pl.when(i >= nb_ref[0])
    def _():
        o_ref[...] = jnp.zeros_like(o_ref)


def _experts(block_e, n_used, xs, wg, wu, wd):
    P = xs.shape[0]
    D = D_MODEL
    rows = pl.BlockSpec((DISPATCH_BLOCK, ROW_SUB, LANES), lambda i, be, nb: (i, 0, 0))
    grid_spec = pltpu.PrefetchScalarGridSpec(
        num_scalar_prefetch=2,
        grid=(P // DISPATCH_BLOCK,),
        in_specs=[
            rows,
            pl.BlockSpec((None, D, EXPERT_HIDDEN), lambda i, be, nb: (be[i], 0, 0)),
            pl.BlockSpec((None, D, EXPERT_HIDDEN), lambda i, be, nb: (be[i], 0, 0)),
            pl.BlockSpec((None, EXPERT_HIDDEN, D), lambda i, be, nb: (be[i], 0, 0)),
        ],
        out_specs=rows,
    )
    return pl.pallas_call(
        _expert_kernel,
        grid_spec=grid_spec,
        out_shape=jax.ShapeDtypeStruct(xs.shape, F32),
        compiler_params=_cparams(("arbitrary",)),
        name="experts",
    )(block_e, n_used, xs, wg, wu, wd)


def _combine_kernel(d0_ref, d1_ref, h_ref, wts_ref, g2_ref, b2_ref, ys_ref, o_ref, rows_ref, sem):
    def make_copy(c, t, dest):
        return pltpu.make_async_copy(ys_ref.at[dest], rows_ref.at[c, t], sem)

    _row_copies((d0_ref, d1_ref), make_copy)
    w = wts_ref[...]
    ffn = (w[:, 0:1] * rows_ref[0].reshape(ROW_TM, D_MODEL)
           + w[:, 1:2] * rows_ref[1].reshape(ROW_TM, D_MODEL))
    o_ref[...] = _layer_norm(ALPHA * h_ref[...] + ffn, g2_ref[...], b2_ref[...])


def _combine(dest, h1, wts_t, g2, b2, ys):
    T, D = h1.shape
    return pl.pallas_call(
        _combine_kernel,
        grid=(T // ROW_TM,),
        in_specs=[
            pl.BlockSpec((ROW_TM,), lambda i: (i,), memory_space=pltpu.SMEM),
            pl.BlockSpec((ROW_TM,), lambda i: (i,), memory_space=pltpu.SMEM),
            pl.BlockSpec((ROW_TM, D), lambda i: (i, 0)),
            pl.BlockSpec((ROW_TM, EXPERT_TOPK), lambda i: (i, 0)),
            pl.BlockSpec((1, D), lambda i: (0, 0)),
            pl.BlockSpec((1, D), lambda i: (0, 0)),
            pl.BlockSpec(memory_space=pl.ANY),
        ],
        out_specs=pl.BlockSpec((ROW_TM, D), lambda i: (i, 0)),
        scratch_shapes=[pltpu.VMEM((EXPERT_TOPK, ROW_TM, ROW_SUB, LANES), F32), pltpu.SemaphoreType.DMA(())],
        out_shape=jax.ShapeDtypeStruct((T, D), F32),
        compiler_params=_cparams(("arbitrary",)),
        name="combine",
    )(dest[0], dest[1], h1, wts_t, g2, b2, ys)


def _attn_bias_tables():
    dj = np.arange(ATT_T, dtype=np.float64)[:, None]
    qi = np.tile(np.arange(ATT_TQ, dtype=np.float64), 2)[None, :]
    slopes = (2.0 ** (-8.0 * np.arange(1, DIFF_HEADS + 1, dtype=np.float64) / DIFF_HEADS)) * LOG2E
    sl = slopes[:, None, None]
    tables = [sl * np.broadcast_to(dj, (ATT_T, 2 * ATT_TQ))[None]]
    for kt in range(ATT_TK // ATT_T):
        kj = dj + kt * ATT_T
        bias = qi - np.abs(qi - kj) - kt * ATT_T
        visible = (kj // CHUNK) <= (qi // CHUNK)
        tables.append(np.where(visible[None], sl * bias[None], MASKED))
    return jnp.asarray(slopes, F32), jnp.asarray(np.stack(tables, axis=1), F32)


def _layer(h, layer, w_in, b_in, sg_ln_g, sg_ln_b, sg_w, sg_b, w_branch_a, lam_q1, lam_k1, lam_q2, lam_k2,
           subln_g, w_branch_b, w_out, ln1_g, ln1_b, w_group, b_group, w_expert, b_expert,
           w_gate, w_up, w_down, ln2_g, ln2_b):
    B, S, D = h.shape
    T = B * S
    row = lambda a: a.reshape(1, -1)
    col = lambda a: a.reshape(-1, 1)

    wk = w_in[:, OFF_K:OFF_VAL].astype(BF16)
    wqt = w_in[:, OFF_Q:OFF_K].T.astype(BF16)
    wvt = w_in[:, OFF_VAL:OFF_GA].T.astype(BF16)
    k, qt, vt = _qkv_proj(h, wk, row(b_in[OFF_K:OFF_VAL]), wqt, col(b_in[OFF_Q:OFF_K]),
                          wvt, col(b_in[OFF_VAL:OFF_GA]))
    slopes, tables = _attn_bias_tables()
    lambda_init = 0.8 - 0.6 * math.exp(-0.3 * layer)
    lamv = jnp.stack([lam_q1, lam_k1, lam_q2, lam_k2]).astype(F32)
    attn = _diff_attention(qt, k, vt, slopes, tables, lamv, col(subln_g), lambda_init)

    sgb = jnp.repeat(sg_b.reshape(GMLP_GROUPS // 2, 2, GMLP_BLOCK), GMLP_GROUP_DIM, axis=1)
    sgb = jnp.transpose(sgb, (0, 2, 1))
    h1 = _mixer(h.reshape(T, D), attn.reshape(T, DIFF_WIDTH),
                w_in[:, OFF_U:OFF_Q].astype(BF16), row(b_in[OFF_U:OFF_Q]),
                w_in[:, OFF_GA:IN_WIDTH].astype(BF16), row(b_in[OFF_GA:IN_WIDTH]),
                row(sg_ln_g), row(sg_ln_b), sg_w, sgb,
                w_branch_a.astype(BF16), w_branch_b.astype(BF16), w_out.astype(BF16),
                row(ln1_g), row(ln1_b))

    wr = jnp.zeros((LANES, D), F32)
    wr = wr.at[0:N_GROUPS].set(w_group.T).at[8:8 + N_EXPERTS].set(w_expert.T)
    wr_hi = wr.astype(BF16)
    wr_lo = (wr - wr_hi.astype(F32)).astype(BF16)
    br = jnp.zeros((LANES, 1), F32)
    br = br.at[0:N_GROUPS, 0].set(b_group.astype(F32)).at[8:8 + N_EXPERTS, 0].set(b_expert.astype(F32))
    eid, wts = _router(h1, wr_hi, wr_lo, br)

    tri = jnp.asarray(np.triu(np.ones((ROUTE_TM, ROUTE_TM), np.float32), k=1), BF16)
    rank, counts = _ranks(eid, tri)

    counts = counts[:, 0]
    padded = (counts + DISPATCH_BLOCK - 1) // DISPATCH_BLOCK * DISPATCH_BLOCK
    pad_end = jnp.cumsum(padded)
    pad_start = (pad_end - padded).astype(jnp.int32)
    P = T * EXPERT_TOPK + N_EXPERTS * DISPATCH_BLOCK
    n_blocks = P // DISPATCH_BLOCK
    block_start = jnp.arange(n_blocks, dtype=jnp.int32) * DISPATCH_BLOCK
    block_e = jnp.minimum(jnp.sum(pad_end[None, :] <= block_start[:, None], axis=1), N_EXPERTS - 1).astype(jnp.int32)
    n_used = (pad_end[-1:] // DISPATCH_BLOCK).astype(jnp.int32)

    dest = pad_start[eid] + rank
    xs = _dispatch(counts, pad_start, dest, h1, P)
    ys = _experts(block_e, n_used, xs, w_gate.astype(BF16), w_up.astype(BF16), w_down.astype(BF16))
    out = _combine(dest, h1, wts.T, row(ln2_g), row(ln2_b), ys)
    return out.reshape(B, S, D)


def kernel(x, w_in, b_in, sg_ln_g, sg_ln_b, sg_w, sg_b, w_branch_a, lam_q1, lam_k1, lam_q2, lam_k2, subln_g,
           w_branch_b, w_out, ln1_g, ln1_b, w_group, b_group, w_expert, b_expert, w_gate, w_up, w_down,
           ln2_g, ln2_b):
    h = x
    for layer in range(DEPTH):
        h = _layer(h, layer, w_in[layer], b_in[layer], sg_ln_g[layer], sg_ln_b[layer], sg_w[layer],
                   sg_b[layer], w_branch_a[layer], lam_q1[layer], lam_k1[layer], lam_q2[layer],
                   lam_k2[layer], subln_g[layer], w_branch_b[layer], w_out[layer], ln1_g[layer],
                   ln1_b[layer], w_group[layer], b_group[layer], w_expert[layer], b_expert[layer],
                   w_gate[layer], w_up[layer], w_down[layer], ln2_g[layer], ln2_b[layer])
    return h
```

```python
import functools
import math

import jax
import jax.numpy as jnp
import numpy as np
from jax import lax
from jax.experimental import pallas as pl
from jax.experimental.pallas import tpu as pltpu

D_MODEL = 1024
DEPTH = 1
CHUNK = 64
GMLP_GROUPS = 8
GMLP_GROUP_DIM = 64
GMLP_WIDTH = GMLP_GROUPS * GMLP_GROUP_DIM
GMLP_BLOCK = 128
DIFF_HEADS = 8
DIFF_QK_DIM = 64
DIFF_V_DIM = 2 * DIFF_QK_DIM
DIFF_QK_WIDTH = DIFF_HEADS * 2 * DIFF_QK_DIM
DIFF_WIDTH = DIFF_HEADS * DIFF_V_DIM
N_GROUPS = 4
EXPERTS_PER_GROUP = 8
N_EXPERTS = N_GROUPS * EXPERTS_PER_GROUP
EXPERT_TOPK = 2
EXPERT_HIDDEN = 512
DISPATCH_BLOCK = 256
ALPHA = (2.0 * DEPTH) ** 0.25
LN_EPS = 1e-5
RMS_EPS = 1e-5
OFF_U = 0
OFF_V = OFF_U + GMLP_WIDTH
OFF_Q = OFF_V + GMLP_WIDTH
OFF_K = OFF_Q + DIFF_QK_WIDTH
OFF_VAL = OFF_K + DIFF_QK_WIDTH
OFF_GA = OFF_VAL + DIFF_WIDTH
OFF_GB = OFF_GA + D_MODEL
IN_WIDTH = OFF_GB + D_MODEL

LANES = 128
HEAD_W = 2 * DIFF_QK_DIM
MASKED = -1e30
VMEM_LIMIT = 56 * 1024 * 1024

PROJ_TM = 512
ATT_T = 256
ATT_TK = 2 * ATT_T
ATT_TQ = ATT_TK
ATT_HB = 2
LOG2E = math.log2(math.e)
VT_ROWS = HEAD_W + 16
MIX_TM = 512
ROUTE_TM = 512
ROW_TM = 256
ROW_UNROLL = 8

BF16 = jnp.bfloat16
F32 = jnp.float32


def _cparams(sem):
    return pltpu.CompilerParams(dimension_semantics=sem, vmem_limit_bytes=VMEM_LIMIT)


def _dot(a, b):
    return jnp.dot(a, b, preferred_element_type=F32)


def _dot_nt(a, b):
    return lax.dot_general(a, b, (((1,), (1,)), ((), ())), preferred_element_type=F32)


def _gelu(x):
    return 0.5 * x * (1.0 + lax.erf(x * (2.0 ** -0.5)))


def _sigmoid(x):
    return 1.0 / (1.0 + jnp.exp(-x))


def _layer_norm(x, g, b):
    mu = jnp.mean(x, axis=-1, keepdims=True)
    xc = x - mu
    var = jnp.mean(xc * xc, axis=-1, keepdims=True)
    return xc * lax.rsqrt(var + LN_EPS) * g + b


def _qkv_kernel(x_ref, wk_ref, bk_ref, wqt_ref, bq_ref, wvt_ref, bv_ref, k_ref, qt_ref, vt_ref):
    xb = x_ref[...].astype(BF16)
    k_ref[...] = (_dot(xb, wk_ref[...]) + bk_ref[...]).astype(BF16)
    qt = (_dot_nt(wqt_ref[...], xb) + bq_ref[...]) * (DIFF_QK_DIM ** -0.5 * LOG2E)
    qt_ref[...] = qt.astype(BF16)
    vt = (_dot_nt(wvt_ref[...], xb) + bv_ref[...]).astype(BF16)
    ones_row = (lax.broadcasted_iota(jnp.int32, (VT_ROWS - HEAD_W, ATT_TK), 0) == 0).astype(BF16)
    for hh in range(DIFF_HEADS):
        for j in range(PROJ_TM // ATT_TK):
            vt_ref[hh, j, :HEAD_W] = vt[hh * HEAD_W:(hh + 1) * HEAD_W, j * ATT_TK:(j + 1) * ATT_TK]
            vt_ref[hh, j, HEAD_W:] = ones_row


def _qkv_proj(x, wk, bk, wqt, bq, wvt, bv):
    B, S, D = x.shape
    nk = S // ATT_TK
    jt = PROJ_TM // ATT_TK
    const2 = lambda b, i: (0, 0)
    return pl.pallas_call(
        _qkv_kernel,
        grid=(B, S // PROJ_TM),
        in_specs=[
            pl.BlockSpec((None, PROJ_TM, D), lambda b, i: (b, i, 0)),
            pl.BlockSpec((D, DIFF_QK_WIDTH), const2),
            pl.BlockSpec((1, DIFF_QK_WIDTH), const2),
            pl.BlockSpec((DIFF_QK_WIDTH, D), const2),
            pl.BlockSpec((DIFF_QK_WIDTH, 1), const2),
            pl.BlockSpec((DIFF_WIDTH, D), const2),
            pl.BlockSpec((DIFF_WIDTH, 1), const2),
        ],
        out_specs=[
            pl.BlockSpec((None, PROJ_TM, DIFF_QK_WIDTH), lambda b, i: (b, i, 0)),
            pl.BlockSpec((None, DIFF_QK_WIDTH, PROJ_TM), lambda b, i: (b, 0, i)),
            pl.BlockSpec((None, DIFF_HEADS, jt, VT_ROWS, ATT_TK), lambda b, i: (b, 0, i, 0, 0)),
        ],
        out_shape=[
            jax.ShapeDtypeStruct((B, S, DIFF_QK_WIDTH), BF16),
            jax.ShapeDtypeStruct((B, DIFF_QK_WIDTH, S), BF16),
            jax.ShapeDtypeStruct((B, DIFF_HEADS, nk, VT_ROWS, ATT_TK), BF16),
        ],
        compiler_params=_cparams(("arbitrary", "arbitrary")),
        name="qkv_proj",
    )(x, wk, bk, wqt, bq, wvt, bv)


def _attn_kernel(slope_ref, qt_ref, k_ref, vt_ref, tab_ref, lamv_ref, g_ref, o_ref,
                 q12_ref, sa_ref, sb_ref, ta_ref, tb_ref, m_ref, acc_ref, *, lambda_init):
    hg = pl.program_id(1)
    qi = pl.program_id(2)
    T = ATT_T
    TQ = ATT_TQ
    n_groups = qi + 1

    row = lax.broadcasted_iota(jnp.int32, (HEAD_W, TQ), 0)
    zero = jnp.zeros((HEAD_W, TQ), BF16)
    for e in range(ATT_HB):
        q = qt_ref[e * HEAD_W:(e + 1) * HEAD_W, :]
        q12_ref[e, :, :TQ] = jnp.where(row < DIFF_QK_DIM, q, zero)
        q12_ref[e, :, TQ:] = jnp.where(row >= DIFF_QK_DIM, q, zero)
    acc_ref[...] = jnp.zeros_like(acc_ref)
    m_ref[...] = jnp.full(m_ref.shape, MASKED, F32)

    def scores(g, s_ref, t_ref):
        k0 = 2 * g
        tab0 = jnp.where(g == qi, 1, 0)
        tab1 = jnp.where(g == qi, 2, 0)
        for e in range(ATT_HB):
            kt = k_ref[pl.ds(pl.multiple_of(k0 * T, 2 * T), 2 * T), e * HEAD_W:(e + 1) * HEAD_W]
            s = _dot(kt, q12_ref[e])
            t0 = s[:T] + tab_ref[e, tab0]
            t1 = s[T:] + tab_ref[e, tab1]
            s_ref[e, :T] = t0
            s_ref[e, T:] = t1
            t_ref[e, 0:1, :] = jnp.max(t0, axis=0, keepdims=True)
            t_ref[e, 1:2, :] = jnp.max(t1, axis=0, keepdims=True)

    def accumulate(g, s_ref, t_ref):
        j0 = (2 * g * T).astype(F32)
        for e in range(ATT_HB):
            slope = slope_ref[hg * ATT_HB + e]
            c0 = slope * j0
            c1 = slope * (j0 + T)
            m = m_ref[e]
            m_new = jnp.maximum(m, jnp.maximum(t_ref[e, 0:1, :] + c0, t_ref[e, 1:2, :] + c1))
            p0 = jnp.exp2(s_ref[e, :T] - (m_new - c0)).astype(BF16)
            p1 = jnp.exp2(s_ref[e, T:] - (m_new - c1)).astype(BF16)
            m_ref[e] = m_new
            p = jnp.concatenate([p0, p1], axis=0)
            acc_ref[e] = jnp.exp2(m - m_new) * acc_ref[e] + _dot(vt_ref[e, g], p)

    scores(0, sa_ref, ta_ref)

    def pair(i, _):
        g = 2 * i
        scores(g + 1, sb_ref, tb_ref)
        accumulate(g, sa_ref, ta_ref)
        scores(jnp.minimum(g + 2, n_groups - 1), sa_ref, ta_ref)
        accumulate(g + 1, sb_ref, tb_ref)
        return 0

    lax.fori_loop(0, n_groups // 2, pair, 0)

    @pl.when(n_groups % 2 == 1)
    def _():
        accumulate(n_groups - 1, sa_ref, ta_ref)

    lv = lamv_ref[...]
    lam = (jnp.exp(jnp.sum(lv[0:1] * lv[1:2], axis=1, keepdims=True))
           - jnp.exp(jnp.sum(lv[2:3] * lv[3:4], axis=1, keepdims=True)) + lambda_init)
    for e in range(ATT_HB):
        o12 = acc_ref[e, :HEAD_W] * (1.0 / acc_ref[e, HEAD_W:HEAD_W + 1])
        o = o12[:, :TQ] - lam * o12[:, TQ:]
        o = o * lax.rsqrt(jnp.mean(o * o, axis=0, keepdims=True) + RMS_EPS)
        o = o * g_ref[...] * (1.0 - lambda_init)
        o_ref[:, e * HEAD_W:(e + 1) * HEAD_W] = o.T.astype(BF16)


def _diff_attention(qt, k, vt, slopes, tables, lamv, gcol, lambda_init):
    B, S, _ = k.shape
    nq = S // ATT_TQ
    gw = ATT_HB * HEAD_W
    lanes = 2 * ATT_TQ
    grid_spec = pltpu.PrefetchScalarGridSpec(
        num_scalar_prefetch=1,
        grid=(B, DIFF_HEADS // ATT_HB, nq),
        in_specs=[
            pl.BlockSpec((None, gw, ATT_TQ), lambda b, h, i, s: (b, h, i)),
            pl.BlockSpec((None, S, gw), lambda b, h, i, s: (b, 0, h)),
            pl.BlockSpec((None, ATT_HB, S // ATT_TK, VT_ROWS, ATT_TK), lambda b, h, i, s: (b, h, 0, 0, 0)),
            pl.BlockSpec((ATT_HB, 3, ATT_T, lanes), lambda b, h, i, s: (h, 0, 0, 0)),
            pl.BlockSpec((4, DIFF_QK_DIM), lambda b, h, i, s: (0, 0)),
            pl.BlockSpec((HEAD_W, 1), lambda b, h, i, s: (0, 0)),
        ],
        out_specs=pl.BlockSpec((None, ATT_TQ, gw), lambda b, h, i, s: (b, i, h)),
        scratch_shapes=[
            pltpu.VMEM((ATT_HB, HEAD_W, lanes), BF16),
            pltpu.VMEM((ATT_HB, ATT_TK, lanes), F32),
            pltpu.VMEM((ATT_HB, ATT_TK, lanes), F32),
            pltpu.VMEM((ATT_HB, 2, lanes), F32),
            pltpu.VMEM((ATT_HB, 2, lanes), F32),
            pltpu.VMEM((ATT_HB, 1, lanes), F32),
            pltpu.VMEM((ATT_HB, VT_ROWS, lanes), F32),
        ],
    )
    return pl.pallas_call(
        functools.partial(_attn_kernel, lambda_init=lambda_init),
        grid_spec=grid_spec,
        out_shape=jax.ShapeDtypeStruct((B, S, DIFF_WIDTH), BF16),
        compiler_params=_cparams(("arbitrary", "arbitrary", "arbitrary")),
        name="diff_attn",
    )(slopes, qt, k, vt, tables, lamv, gcol)


def _mixer_kernel(x_ref, attn_ref, wuv_ref, buv_ref, wg_ref, bg_ref, lng_ref, lnb_ref, sgw_ref, sgb_ref,
                  wa_ref, wb_ref, wo_ref, g1_ref, b1_ref, h_ref):
    x = x_ref[...]
    xb = x.astype(BF16)
    uv = _gelu(_dot(xb, wuv_ref[...]) + buv_ref[...])
    u = uv[:, :GMLP_WIDTH]
    v = _layer_norm(uv[:, GMLP_WIDTH:], lng_ref[...], lnb_ref[...]).astype(BF16)

    ti = lax.broadcasted_iota(jnp.int32, (GMLP_BLOCK, GMLP_BLOCK), 0) // CHUNK
    si = lax.broadcasted_iota(jnp.int32, (GMLP_BLOCK, GMLP_BLOCK), 1) // CHUNK
    causal = ti >= si
    w = [jnp.where(causal, sgw_ref[g], 0.0).astype(BF16) for g in range(GMLP_GROUPS)]
    lane = lax.broadcasted_iota(jnp.int32, (GMLP_BLOCK, LANES), 1)
    lo = lane < GMLP_GROUP_DIM
    vzero = jnp.zeros((GMLP_BLOCK, LANES), BF16)

    blocks = []
    for n in range(MIX_TM // GMLP_BLOCK):
        slabs = []
        for p in range(GMLP_WIDTH // LANES):
            vs = v[n * GMLP_BLOCK:(n + 1) * GMLP_BLOCK, p * LANES:(p + 1) * LANES]
            mixed = (_dot(w[2 * p], jnp.where(lo, vs, vzero))
                     + _dot(w[2 * p + 1], jnp.where(lo, vzero, vs)) + sgb_ref[p])
            slabs.append(mixed)
        blocks.append(jnp.concatenate(slabs, axis=1))
    mixed = jnp.concatenate(blocks, axis=0)

    y_a = _dot((u * mixed).astype(BF16), wa_ref[...])
    y_b = _dot(attn_ref[...], wb_ref[...])
    gates = _sigmoid(_dot(xb, wg_ref[...]) + bg_ref[...])
    z = gates[:, :D_MODEL] * y_a + gates[:, D_MODEL:] * y_b
    mix = _dot(z.astype(BF16), wo_ref[...])
    h_ref[...] = _layer_norm(ALPHA * x + mix, g1_ref[...], b1_ref[...])


def _mixer(x2, attn2, wuv, buv, wg, bg, lng, lnb, sgw, sgb, wa, wb, wo, g1, b1):
    T, D = x2.shape
    c2 = lambda i: (0, 0)
    c3 = lambda i: (0, 0, 0)
    full = lambda a: pl.BlockSpec(a.shape, c2 if a.ndim == 2 else c3)
    return pl.pallas_call(
        _mixer_kernel,
        grid=(T // MIX_TM,),
        in_specs=[pl.BlockSpec((MIX_TM, D), lambda i: (i, 0)),
                  pl.BlockSpec((MIX_TM, DIFF_WIDTH), lambda i: (i, 0))]
                 + [full(a) for a in (wuv, buv, wg, bg, lng, lnb, sgw, sgb, wa, wb, wo, g1, b1)],
        out_specs=pl.BlockSpec((MIX_TM, D), lambda i: (i, 0)),
        out_shape=jax.ShapeDtypeStruct((T, D), F32),
        compiler_params=_cparams(("arbitrary",)),
        name="mixer",
    )(x2, attn2, wuv, buv, wg, bg, lng, lnb, sgw, sgb, wa, wb, wo, g1, b1)


def _router_kernel(h_ref, wr_hi_ref, wr_lo_ref, br_ref, eid_ref, wts_ref):
    hf = h_ref[...]
    hi = hf.astype(BF16)
    lo = (hf - hi.astype(F32)).astype(BF16)
    logits = (_dot_nt(wr_hi_ref[...], hi) + _dot_nt(wr_lo_ref[...], hi)
              + _dot_nt(wr_hi_ref[...], lo)) + br_ref[...]
    gidx = lax.broadcasted_iota(jnp.int32, (8, ROUTE_TM), 0)
    gl = jnp.where(gidx < N_GROUPS, logits[0:8], -jnp.inf)
    gmax = jnp.max(gl, axis=0, keepdims=True)
    g_sel = jnp.min(jnp.where(gl == gmax, gidx, N_GROUPS), axis=0, keepdims=True)
    g_w = 1.0 / jnp.sum(jnp.exp(gl - gmax), axis=0, keepdims=True)

    el = logits[8:8 + EXPERTS_PER_GROUP]
    for g in range(1, N_GROUPS):
        el = jnp.where(g_sel == g, logits[8 + g * EXPERTS_PER_GROUP:8 + (g + 1) * EXPERTS_PER_GROUP], el)
    eidx = lax.broadcasted_iota(jnp.int32, el.shape, 0)
    v1 = jnp.max(el, axis=0, keepdims=True)
    i1 = jnp.min(jnp.where(el == v1, eidx, EXPERTS_PER_GROUP), axis=0, keepdims=True)
    rest = jnp.where(eidx == i1, -jnp.inf, el)
    v2 = jnp.max(rest, axis=0, keepdims=True)
    i2 = jnp.min(jnp.where(rest == v2, eidx, EXPERTS_PER_GROUP), axis=0, keepdims=True)
    e21 = jnp.exp(v2 - v1)
    w1 = g_w / (1.0 + e21)
    w2 = g_w * e21 / (1.0 + e21)
    base = g_sel * EXPERTS_PER_GROUP
    eid_ref[0:1, :] = base + i1
    eid_ref[1:2, :] = base + i2
    wts_ref[0:1, :] = w1
    wts_ref[1:2, :] = w2


def _router(h1, wr_hi, wr_lo, br):
    T, D = h1.shape
    c2 = lambda i: (0, 0)
    return pl.pallas_call(
        _router_kernel,
        grid=(T // ROUTE_TM,),
        in_specs=[pl.BlockSpec((ROUTE_TM, D), lambda i: (i, 0)),
                  pl.BlockSpec(wr_hi.shape, c2), pl.BlockSpec(wr_lo.shape, c2), pl.BlockSpec(br.shape, c2)],
        out_specs=[pl.BlockSpec((EXPERT_TOPK, ROUTE_TM), lambda i: (0, i)),
                   pl.BlockSpec((EXPERT_TOPK, ROUTE_TM), lambda i: (0, i))],
        out_shape=[jax.ShapeDtypeStruct((EXPERT_TOPK, T), jnp.int32),
                   jax.ShapeDtypeStruct((EXPERT_TOPK, T), F32)],
        compiler_params=_cparams(("arbitrary",)),
        name="router",
    )(h1, wr_hi, wr_lo, br)


def _rank_kernel(eid_ref, tri_ref, rank_ref, cnt_ref, run_ref):
    @pl.when(pl.program_id(0) == 0)
    def _():
        run_ref[...] = jnp.zeros_like(run_ref)

    eid = eid_ref[...]
    eiota = lax.broadcasted_iota(jnp.int32, (N_EXPERTS, ROUTE_TM), 0)
    run = run_ref[...]
    ranks = []
    for c in range(EXPERT_TOPK):
        hot = eiota == eid[c:c + 1]
        hotf = hot.astype(F32)
        before = _dot(hotf.astype(BF16), tri_ref[...])
        ranks.append(jnp.sum(jnp.where(hot, run + before, 0.0), axis=0, keepdims=True))
        run = run + jnp.sum(hotf, axis=1, keepdims=True)
    run_ref[...] = run
    for c in range(EXPERT_TOPK):
        rank_ref[c:c + 1, :] = ranks[c].astype(jnp.int32)
    cnt_ref[...] = run.astype(jnp.int32)


def _ranks(eid, tri):
    _, T = eid.shape
    return pl.pallas_call(
        _rank_kernel,
        grid=(T // ROUTE_TM,),
        in_specs=[pl.BlockSpec((EXPERT_TOPK, ROUTE_TM), lambda i: (0, i)),
                  pl.BlockSpec(tri.shape, lambda i: (0, 0))],
        out_specs=[pl.BlockSpec((EXPERT_TOPK, ROUTE_TM), lambda i: (0, i)),
                   pl.BlockSpec((N_EXPERTS, 1), lambda i: (0, 0))],
        out_shape=[jax.ShapeDtypeStruct((EXPERT_TOPK, T), jnp.int32),
                   jax.ShapeDtypeStruct((N_EXPERTS, 1), jnp.int32)],
        scratch_shapes=[pltpu.VMEM((N_EXPERTS, 1), F32)],
        compiler_params=_cparams(("arbitrary",)),
        name="rank",
    )(eid, tri)


ROW_SUB = D_MODEL // LANES


def _row_copies(dest_refs, make_copy):
    def issue(t, _):
        for c in range(EXPERT_TOPK):
            make_copy(c, t, dest_refs[c][t]).start(priority=c)
        return 0

    lax.fori_loop(0, ROW_TM, issue, 0, unroll=ROW_UNROLL)

    def drain(t, _):
        for c in range(EXPERT_TOPK):
            make_copy(c, t, 0).wait()
        return 0

    lax.fori_loop(0, ROW_TM, drain, 0, unroll=ROW_UNROLL)


def _dispatch_kernel(cnt_ref, pad_ref, d0_ref, d1_ref, h_ref, xs_ref, rows_ref, zero_ref, sem, zsem):
    @pl.when(pl.program_id(0) == 0)
    def _():
        zero_ref[...] = jnp.zeros_like(zero_ref)

        def zero_copy(r):
            return pltpu.make_async_copy(zero_ref, xs_ref.at[r], zsem)

        def start(r, _):
            zero_copy(r).start()
            return 0

        def per_expert(e, carry):
            n, _ = carry
            lo = pad_ref[e] + cnt_ref[e]
            hi = pad_ref[e] + (cnt_ref[e] + DISPATCH_BLOCK - 1) // DISPATCH_BLOCK * DISPATCH_BLOCK
            lax.fori_loop(lo, hi, start, 0)
            return n + (hi - lo), hi

        n_pad, used = lax.fori_loop(0, N_EXPERTS, per_expert, (0, 0))
        lax.fori_loop(used, xs_ref.shape[0], start, 0)
        n_pad = n_pad + (xs_ref.shape[0] - used)

        def wait(_, c):
            zero_copy(0).wait()
            return c

        lax.fori_loop(0, n_pad, wait, 0)

    rows_ref[...] = h_ref[...].reshape(ROW_TM, ROW_SUB, LANES)

    def make_copy(c, t, dest):
        return pltpu.make_async_copy(rows_ref.at[t], xs_ref.at[dest], sem)

    _row_copies((d0_ref, d1_ref), make_copy)


def _dispatch(counts, pad_start, dest, h1, n_rows):
    T, D = h1.shape
    grid_spec = pltpu.PrefetchScalarGridSpec(
        num_scalar_prefetch=2,
        grid=(T // ROW_TM,),
        in_specs=[
            pl.BlockSpec((ROW_TM,), lambda i, c, p: (i,), memory_space=pltpu.SMEM),
            pl.BlockSpec((ROW_TM,), lambda i, c, p: (i,), memory_space=pltpu.SMEM),
            pl.BlockSpec((ROW_TM, D), lambda i, c, p: (i, 0)),
        ],
        out_specs=pl.BlockSpec(memory_space=pl.ANY),
        scratch_shapes=[pltpu.VMEM((ROW_TM, ROW_SUB, LANES), F32), pltpu.VMEM((ROW_SUB, LANES), F32),
                        pltpu.SemaphoreType.DMA(()), pltpu.SemaphoreType.DMA(())],
    )
    return pl.pallas_call(
        _dispatch_kernel,
        grid_spec=grid_spec,
        out_shape=jax.ShapeDtypeStruct((n_rows, ROW_SUB, LANES), F32),
        compiler_params=_cparams(("arbitrary",)),
        name="dispatch",
    )(counts, pad_start, dest[0], dest[1], h1)


def _expert_kernel(be_ref, nb_ref, xs_ref, wg_ref, wu_ref, wd_ref, o_ref):
    i = pl.program_id(0)

    @pl.when(i < nb_ref[0])
    def _():
        xb = xs_ref[...].reshape(DISPATCH_BLOCK, D_MODEL).astype(BF16)
        gate = _dot(xb, wg_ref[...])
        up = _dot(xb, wu_ref[...])
        hb = (gate * _sigmoid(gate) * up).astype(BF16)
        o_ref[...] = _dot(hb, wd_ref[...]).reshape(DISPATCH_BLOCK, ROW_SUB, LANES)

    @pl.when(i >= nb_ref[0])
    def _():
        o_ref[...] = jnp.zeros_like(o_ref)


def _experts(block_e, n_used, xs, wg, wu, wd):
    P = xs.shape[0]
    D = D_MODEL
    rows = pl.BlockSpec((DISPATCH_BLOCK, ROW_SUB, LANES), lambda i, be, nb: (i, 0, 0))
    grid_spec = pltpu.PrefetchScalarGridSpec(
        num_scalar_prefetch=2,
        grid=(P // DISPATCH_BLOCK,),
        in_specs=[
            rows,
            pl.BlockSpec((None, D, EXPERT_HIDDEN), lambda i, be, nb: (be[i], 0, 0)),
            pl.BlockSpec((None, D, EXPERT_HIDDEN), lambda i, be, nb: (be[i], 0, 0)),
            pl.BlockSpec((None, EXPERT_HIDDEN, D), lambda i, be, nb: (be[i], 0, 0)),
        ],
        out_specs=rows,
    )
    return pl.pallas_call(
        _expert_kernel,
        grid_spec=grid_spec,
        out_shape=jax.ShapeDtypeStruct(xs.shape, F32),
        compiler_params=_cparams(("arbitrary",)),
        name="experts",
    )(block_e, n_used, xs, wg, wu, wd)


def _combine_kernel(d0_ref, d1_ref, n0_ref, n1_ref, h_ref, wts_ref, g2_ref, b2_ref, ys_ref, o_ref,
                    rows_ref, sems):
    i = pl.program_id(0)
    n = pl.num_programs(0)
    slot = i % 2

    def gather(dest_refs, buf):
        def issue(t, _):
            for c in range(EXPERT_TOPK):
                pltpu.make_async_copy(ys_ref.at[dest_refs[c][t]], rows_ref.at[buf, c, t],
                                      sems.at[buf]).start(priority=c)
            return 0

        lax.fori_loop(0, ROW_TM, issue, 0, unroll=ROW_UNROLL)

    @pl.when(i == 0)
    def _():
        gather((d0_ref, d1_ref), 0)

    @pl.when(i + 1 < n)
    def _():
        gather((n0_ref, n1_ref), 1 - slot)

    def drain(t, _):
        for c in range(EXPERT_TOPK):
            pltpu.make_async_copy(ys_ref.at[0], rows_ref.at[slot, c, t], sems.at[slot]).wait()
        return 0

    lax.fori_loop(0, ROW_TM, drain, 0, unroll=ROW_UNROLL)
    w = wts_ref[...]
    ffn = (w[:, 0:1] * rows_ref[slot, 0].reshape(ROW_TM, D_MODEL)
           + w[:, 1:2] * rows_ref[slot, 1].reshape(ROW_TM, D_MODEL))
    o_ref[...] = _layer_norm(ALPHA * h_ref[...] + ffn, g2_ref[...], b2_ref[...])


def _combine(dest, h1, wts_t, g2, b2, ys):
    T, D = h1.shape
    n = T // ROW_TM
    cur = pl.BlockSpec((ROW_TM,), lambda i: (i,), memory_space=pltpu.SMEM)
    nxt = pl.BlockSpec((ROW_TM,), lambda i: (jnp.minimum(i + 1, n - 1),), memory_space=pltpu.SMEM)
    return pl.pallas_call(
        _combine_kernel,
        grid=(n,),
        in_specs=[
            cur, cur, nxt, nxt,
            pl.BlockSpec((ROW_TM, D), lambda i: (i, 0)),
            pl.BlockSpec((ROW_TM, EXPERT_TOPK), lambda i: (i, 0)),
            pl.BlockSpec((1, D), lambda i: (0, 0)),
            pl.BlockSpec((1, D), lambda i: (0, 0)),
            pl.BlockSpec(memory_space=pl.ANY),
        ],
        out_specs=pl.BlockSpec((ROW_TM, D), lambda i: (i, 0)),
        scratch_shapes=[pltpu.VMEM((2, EXPERT_TOPK, ROW_TM, ROW_SUB, LANES), F32),
                        pltpu.SemaphoreType.DMA((2,))],
        out_shape=jax.ShapeDtypeStruct((T, D), F32),
        compiler_params=_cparams(("arbitrary",)),
        name="combine",
    )(dest[0], dest[1], dest[0], dest[1], h1, wts_t, g2, b2, ys)


def _attn_bias_tables():
    dj = np.arange(ATT_T, dtype=np.float64)[:, None]
    qi = np.tile(np.arange(ATT_TQ, dtype=np.float64), 2)[None, :]
    slopes = (2.0 ** (-8.0 * np.arange(1, DIFF_HEADS + 1, dtype=np.float64) / DIFF_HEADS)) * LOG2E
    sl = slopes[:, None, None]
    tables = [sl * np.broadcast_to(dj, (ATT_T, 2 * ATT_TQ))[None]]
    for kt in range(ATT_TK // ATT_T):
        kj = dj + kt * ATT_T
        bias = qi - np.abs(qi - kj) - kt * ATT_T
        visible = (kj // CHUNK) <= (qi // CHUNK)
        tables.append(np.where(visible[None], sl * bias[None], MASKED))
    return jnp.asarray(slopes, F32), jnp.asarray(np.stack(tables, axis=1), F32)


def _layer(h, layer, w_in, b_in, sg_ln_g, sg_ln_b, sg_w, sg_b, w_branch_a, lam_q1, lam_k1, lam_q2, lam_k2,
           subln_g, w_branch_b, w_out, ln1_g, ln1_b, w_group, b_group, w_expert, b_expert,
           w_gate, w_up, w_down, ln2_g, ln2_b):
    B, S, D = h.shape
    T = B * S
    row = lambda a: a.reshape(1, -1)
    col = lambda a: a.reshape(-1, 1)

    wk = w_in[:, OFF_K:OFF_VAL].astype(BF16)
    wqt = w_in[:, OFF_Q:OFF_K].T.astype(BF16)
    wvt = w_in[:, OFF_VAL:OFF_GA].T.astype(BF16)
    k, qt, vt = _qkv_proj(h, wk, row(b_in[OFF_K:OFF_VAL]), wqt, col(b_in[OFF_Q:OFF_K]),
                          wvt, col(b_in[OFF_VAL:OFF_GA]))
    slopes, tables = _attn_bias_tables()
    lambda_init = 0.8 - 0.6 * math.exp(-0.3 * layer)
    lamv = jnp.stack([lam_q1, lam_k1, lam_q2, lam_k2]).astype(F32)
    attn = _diff_attention(qt, k, vt, slopes, tables, lamv, col(subln_g), lambda_init)

    sgb = jnp.repeat(sg_b.reshape(GMLP_GROUPS // 2, 2, GMLP_BLOCK), GMLP_GROUP_DIM, axis=1)
    sgb = jnp.transpose(sgb, (0, 2, 1))
    h1 = _mixer(h.reshape(T, D), attn.reshape(T, DIFF_WIDTH),
                w_in[:, OFF_U:OFF_Q].astype(BF16), row(b_in[OFF_U:OFF_Q]),
                w_in[:, OFF_GA:IN_WIDTH].astype(BF16), row(b_in[OFF_GA:IN_WIDTH]),
                row(sg_ln_g), row(sg_ln_b), sg_w, sgb,
                w_branch_a.astype(BF16), w_branch_b.astype(BF16), w_out.astype(BF16),
                row(ln1_g), row(ln1_b))

    wr = jnp.zeros((LANES, D), F32)
    wr = wr.at[0:N_GROUPS].set(w_group.T).at[8:8 + N_EXPERTS].set(w_expert.T)
    wr_hi = wr.astype(BF16)
    wr_lo = (wr - wr_hi.astype(F32)).astype(BF16)
    br = jnp.zeros((LANES, 1), F32)
    br = br.at[0:N_GROUPS, 0].set(b_group.astype(F32)).at[8:8 + N_EXPERTS, 0].set(b_expert.astype(F32))
    eid, wts = _router(h1, wr_hi, wr_lo, br)

    tri = jnp.asarray(np.triu(np.ones((ROUTE_TM, ROUTE_TM), np.float32), k=1), BF16)
    rank, counts = _ranks(eid, tri)

    counts = counts[:, 0]
    padded = (counts + DISPATCH_BLOCK - 1) // DISPATCH_BLOCK * DISPATCH_BLOCK
    pad_end = jnp.cumsum(padded)
    pad_start = (pad_end - padded).astype(jnp.int32)
    P = T * EXPERT_TOPK + N_EXPERTS * DISPATCH_BLOCK
    n_blocks = P // DISPATCH_BLOCK
    block_start = jnp.arange(n_blocks, dtype=jnp.int32) * DISPATCH_BLOCK
    block_e = jnp.minimum(jnp.sum(pad_end[None, :] <= block_start[:, None], axis=1), N_EXPERTS - 1).astype(jnp.int32)
    n_used = (pad_end[-1:] // DISPATCH_BLOCK).astype(jnp.int32)

    hot = eid[:, :, None] == jnp.arange(N_EXPERTS, dtype=jnp.int32)
    dest = jnp.sum(jnp.where(hot, pad_start, 0), axis=-1) + rank
    xs = _dispatch(counts, pad_start, dest, h1, P)
    ys = _experts(block_e, n_used, xs, w_gate.astype(BF16), w_up.astype(BF16), w_down.astype(BF16))
    out = _combine(dest, h1, wts.T, row(ln2_g), row(ln2_b), ys)
    return out.reshape(B, S, D)


def kernel(x, w_in, b_in, sg_ln_g, sg_ln_b, sg_w, sg_b, w_branch_a, lam_q1, lam_k1, lam_q2, lam_k2, subln_g,
           w_branch_b, w_out, ln1_g, ln1_b, w_group, b_group, w_expert, b_expert, w_gate, w_up, w_down,
           ln2_g, ln2_b):
    h = x
    for layer in range(DEPTH):
        h = _layer(h, layer, w_in[layer], b_in[layer], sg_ln_g[layer], sg_ln_b[layer], sg_w[layer],
                   sg_b[layer], w_branch_a[layer], lam_q1[layer], lam_k1[layer], lam_q2[layer],
                   lam_k2[layer], subln_g[layer], w_branch_b[layer], w_out[layer], ln1_g[layer],
                   ln1_b[layer], w_group[layer], b_group[layer], w_expert[layer], b_expert[layer],
                   w_gate[layer], w_up[layer], w_down[layer], ln2_g[layer], ln2_b[layer])
    return h
```

```python
import functools
import math

import jax
import jax.numpy as jnp
import numpy as np
from jax import lax
from jax.experimental import pallas as pl
from jax.experimental.pallas import tpu as pltpu

D_MODEL = 1024
DEPTH = 1
CHUNK = 64
GMLP_GROUPS = 8
GMLP_GROUP_DIM = 64
GMLP_WIDTH = GMLP_GROUPS * GMLP_GROUP_DIM
GMLP_BLOCK = 128
DIFF_HEADS = 8
DIFF_QK_DIM = 64
DIFF_V_DIM = 2 * DIFF_QK_DIM
DIFF_QK_WIDTH = DIFF_HEADS * 2 * DIFF_QK_DIM
DIFF_WIDTH = DIFF_HEADS * DIFF_V_DIM
N_GROUPS = 4
EXPERTS_PER_GROUP = 8
N_EXPERTS = N_GROUPS * EXPERTS_PER_GROUP
EXPERT_TOPK = 2
EXPERT_HIDDEN = 512
DISPATCH_BLOCK = 256
ALPHA = (2.0 * DEPTH) ** 0.25
LN_EPS = 1e-5
RMS_EPS = 1e-5
OFF_U = 0
OFF_V = OFF_U + GMLP_WIDTH
OFF_Q = OFF_V + GMLP_WIDTH
OFF_K = OFF_Q + DIFF_QK_WIDTH
OFF_VAL = OFF_K + DIFF_QK_WIDTH
OFF_GA = OFF_VAL + DIFF_WIDTH
OFF_GB = OFF_GA + D_MODEL
IN_WIDTH = OFF_GB + D_MODEL

LANES = 128
HEAD_W = 2 * DIFF_QK_DIM
MASKED = -1e30
VMEM_LIMIT = 56 * 1024 * 1024

PROJ_TM = 512
ATT_T = 256
ATT_TK = 2 * ATT_T
ATT_TQ = ATT_TK
ATT_HB = 2
LOG2E = math.log2(math.e)
VT_ROWS = HEAD_W + 16
MIX_TM = 512
ROUTE_TM = 512
ROW_TM = 256
ROW_UNROLL = 8

BF16 = jnp.bfloat16
F32 = jnp.float32


def _cparams(sem):
    return pltpu.CompilerParams(dimension_semantics=sem, vmem_limit_bytes=VMEM_LIMIT)


def _dot(a, b):
    return jnp.dot(a, b, preferred_element_type=F32)


def _dot_nt(a, b):
    return lax.dot_general(a, b, (((1,), (1,)), ((), ())), preferred_element_type=F32)


def _gelu(x):
    return 0.5 * x * (1.0 + lax.erf(x * (2.0 ** -0.5)))


def _sigmoid(x):
    return 1.0 / (1.0 + jnp.exp(-x))


def _layer_norm(x, g, b):
    mu = jnp.mean(x, axis=-1, keepdims=True)
    xc = x - mu
    var = jnp.mean(xc * xc, axis=-1, keepdims=True)
    return xc * lax.rsqrt(var + LN_EPS) * g + b


def _qkv_kernel(x_ref, wk_ref, bk_ref, wqt_ref, bq_ref, wvt_ref, bv_ref, k_ref, qt_ref, vt_ref):
    xb = x_ref[...].astype(BF16)
    k_ref[...] = (_dot(xb, wk_ref[...]) + bk_ref[...]).astype(BF16)
    qt = (_dot_nt(wqt_ref[...], xb) + bq_ref[...]) * (DIFF_QK_DIM ** -0.5 * LOG2E)
    qt_ref[...] = qt.astype(BF16)
    vt = (_dot_nt(wvt_ref[...], xb) + bv_ref[...]).astype(BF16)
    ones_row = (lax.broadcasted_iota(jnp.int32, (VT_ROWS - HEAD_W, ATT_TK), 0) == 0).astype(BF16)
    for hh in range(DIFF_HEADS):
        for j in range(PROJ_TM // ATT_TK):
            vt_ref[hh, j, :HEAD_W] = vt[hh * HEAD_W:(hh + 1) * HEAD_W, j * ATT_TK:(j + 1) * ATT_TK]
            vt_ref[hh, j, HEAD_W:] = ones_row


def _qkv_proj(x, wk, bk, wqt, bq, wvt, bv):
    B, S, D = x.shape
    nk = S // ATT_TK
    jt = PROJ_TM // ATT_TK
    const2 = lambda b, i: (0, 0)
    return pl.pallas_call(
        _qkv_kernel,
        grid=(B, S // PROJ_TM),
        in_specs=[
            pl.BlockSpec((None, PROJ_TM, D), lambda b, i: (b, i, 0)),
            pl.BlockSpec((D, DIFF_QK_WIDTH), const2),
            pl.BlockSpec((1, DIFF_QK_WIDTH), const2),
            pl.BlockSpec((DIFF_QK_WIDTH, D), const2),
            pl.BlockSpec((DIFF_QK_WIDTH, 1), const2),
            pl.BlockSpec((DIFF_WIDTH, D), const2),
            pl.BlockSpec((DIFF_WIDTH, 1), const2),
        ],
        out_specs=[
            pl.BlockSpec((None, PROJ_TM, DIFF_QK_WIDTH), lambda b, i: (b, i, 0)),
            pl.BlockSpec((None, DIFF_QK_WIDTH, PROJ_TM), lambda b, i: (b, 0, i)),
            pl.BlockSpec((None, DIFF_HEADS, jt, VT_ROWS, ATT_TK), lambda b, i: (b, 0, i, 0, 0)),
        ],
        out_shape=[
            jax.ShapeDtypeStruct((B, S, DIFF_QK_WIDTH), BF16),
            jax.ShapeDtypeStruct((B, DIFF_QK_WIDTH, S), BF16),
            jax.ShapeDtypeStruct((B, DIFF_HEADS, nk, VT_ROWS, ATT_TK), BF16),
        ],
        compiler_params=_cparams(("arbitrary", "arbitrary")),
        name="qkv_proj",
    )(x, wk, bk, wqt, bq, wvt, bv)


def _attn_kernel(qt_ref, k_ref, pos_ref, vt_ref, qf_ref, corr_ref, lamv_ref, g_ref, o_ref,
                 q12_ref, sa_ref, sb_ref, ta_ref, tb_ref, m_ref, acc_ref, *, lambda_init):
    qi = pl.program_id(2)
    T = ATT_T
    TQ = ATT_TQ
    n = qi + 1

    row = lax.broadcasted_iota(jnp.int32, (HEAD_W, TQ), 0)
    zero = jnp.zeros((HEAD_W, TQ), BF16)
    for e in range(ATT_HB):
        q = qt_ref[e * HEAD_W:(e + 1) * HEAD_W, :]
        q12_ref[e, :HEAD_W, :TQ] = jnp.where(row < DIFF_QK_DIM, q, zero)
        q12_ref[e, :HEAD_W, TQ:] = jnp.where(row >= DIFF_QK_DIM, q, zero)
        q12_ref[e, HEAD_W:, :] = jnp.broadcast_to(qf_ref[e], (LANES, 2 * TQ))
    acc_ref[...] = jnp.zeros_like(acc_ref)
    m_ref[...] = jnp.full(m_ref.shape, MASKED, F32)

    def scores(g, s_ref, t_ref, diagonal):
        rows = pl.ds(pl.multiple_of(g * 2 * T, 2 * T), 2 * T)
        for e in range(ATT_HB):
            kx = jnp.concatenate([k_ref[rows, e * HEAD_W:(e + 1) * HEAD_W], pos_ref[rows, :]], axis=1)
            t = _dot(kx, q12_ref[e])
            if diagonal:
                t = t + corr_ref[e]
            s_ref[e] = t
            t_ref[e] = jnp.max(t, axis=0, keepdims=True)

    def accumulate(g, s_ref, t_ref):
        for e in range(ATT_HB):
            m = m_ref[e]
            m_new = jnp.maximum(m, t_ref[e])
            m_ref[e] = m_new
            p = jnp.exp2(s_ref[e] - m_new).astype(BF16)
            acc_ref[e] = jnp.exp2(m - m_new) * acc_ref[e] + _dot(vt_ref[e, g], p)

    @pl.when(n == 1)
    def _():
        scores(0, sa_ref, ta_ref, True)
        accumulate(0, sa_ref, ta_ref)

    @pl.when(n > 1)
    def _():
        scores(0, sa_ref, ta_ref, False)

    def pair(i, _):
        g = 2 * i
        scores(g + 1, sb_ref, tb_ref, False)
        accumulate(g, sa_ref, ta_ref)
        scores(g + 2, sa_ref, ta_ref, False)
        accumulate(g + 1, sb_ref, tb_ref)
        return 0

    lax.fori_loop(0, jnp.maximum(n - 2, 0) // 2, pair, 0)

    @pl.when((n > 1) & (n % 2 == 0))
    def _():
        scores(n - 1, sb_ref, tb_ref, True)
        accumulate(n - 2, sa_ref, ta_ref)
        accumulate(n - 1, sb_ref, tb_ref)

    @pl.when((n > 1) & (n % 2 == 1))
    def _():
        scores(n - 2, sb_ref, tb_ref, False)
        accumulate(n - 3, sa_ref, ta_ref)
        scores(n - 1, sa_ref, ta_ref, True)
        accumulate(n - 2, sb_ref, tb_ref)
        accumulate(n - 1, sa_ref, ta_ref)

    lv = lamv_ref[...]
    lam = (jnp.exp(jnp.sum(lv[0:1] * lv[1:2], axis=1, keepdims=True))
           - jnp.exp(jnp.sum(lv[2:3] * lv[3:4], axis=1, keepdims=True)) + lambda_init)
    for e in range(ATT_HB):
        o12 = acc_ref[e, :HEAD_W] * (1.0 / acc_ref[e, HEAD_W:HEAD_W + 1])
        o = o12[:, :TQ] - lam * o12[:, TQ:]
        o = o * lax.rsqrt(jnp.mean(o * o, axis=0, keepdims=True) + RMS_EPS)
        o = o * g_ref[...] * (1.0 - lambda_init)
        o_ref[:, e * HEAD_W:(e + 1) * HEAD_W] = o.T.astype(BF16)


def _diff_attention(qt, k, pos, vt, qfeat, corr, lamv, gcol, lambda_init):
    B, S, _ = k.shape
    nq = S // ATT_TQ
    gw = ATT_HB * HEAD_W
    lanes = 2 * ATT_TQ
    return pl.pallas_call(
        functools.partial(_attn_kernel, lambda_init=lambda_init),
        grid=(B, DIFF_HEADS // ATT_HB, nq),
        in_specs=[
            pl.BlockSpec((None, gw, ATT_TQ), lambda b, h, i: (b, h, i)),
            pl.BlockSpec((None, S, gw), lambda b, h, i: (b, 0, h)),
            pl.BlockSpec((S, LANES), lambda b, h, i: (0, 0)),
            pl.BlockSpec((None, ATT_HB, S // ATT_TK, VT_ROWS, ATT_TK), lambda b, h, i: (b, h, 0, 0, 0)),
            pl.BlockSpec((ATT_HB, LANES, 1), lambda b, h, i: (h, 0, 0)),
            pl.BlockSpec((ATT_HB, ATT_TK, lanes), lambda b, h, i: (h, 0, 0)),
            pl.BlockSpec((4, DIFF_QK_DIM), lambda b, h, i: (0, 0)),
            pl.BlockSpec((HEAD_W, 1), lambda b, h, i: (0, 0)),
        ],
        out_specs=pl.BlockSpec((None, ATT_TQ, gw), lambda b, h, i: (b, i, h)),
        scratch_shapes=[
            pltpu.VMEM((ATT_HB, HEAD_W + LANES, lanes), BF16),
            pltpu.VMEM((ATT_HB, ATT_TK, lanes), F32),
            pltpu.VMEM((ATT_HB, ATT_TK, lanes), F32),
            pltpu.VMEM((ATT_HB, 1, lanes), F32),
            pltpu.VMEM((ATT_HB, 1, lanes), F32),
            pltpu.VMEM((ATT_HB, 1, lanes), F32),
            pltpu.VMEM((ATT_HB, VT_ROWS, lanes), F32),
        ],
        out_shape=jax.ShapeDtypeStruct((B, S, DIFF_WIDTH), BF16),
        compiler_params=_cparams(("arbitrary", "arbitrary", "arbitrary")),
        name="diff_attn",
    )(qt, k, pos, vt, qfeat, corr, lamv, gcol)


def _mixer_kernel(x_ref, attn_ref, wuv_ref, buv_ref, wg_ref, bg_ref, lng_ref, lnb_ref, sgw_ref, sgb_ref,
                  wa_ref, wb_ref, wo_ref, g1_ref, b1_ref, h_ref):
    x = x_ref[...]
    xb = x.astype(BF16)
    uv = _gelu(_dot(xb, wuv_ref[...]) + buv_ref[...])
    u = uv[:, :GMLP_WIDTH]
    v = _layer_norm(uv[:, GMLP_WIDTH:], lng_ref[...], lnb_ref[...]).astype(BF16)

    ti = lax.broadcasted_iota(jnp.int32, (GMLP_BLOCK, GMLP_BLOCK), 0) // CHUNK
    si = lax.broadcasted_iota(jnp.int32, (GMLP_BLOCK, GMLP_BLOCK), 1) // CHUNK
    causal = ti >= si
    w = [jnp.where(causal, sgw_ref[g], 0.0).astype(BF16) for g in range(GMLP_GROUPS)]
    lane = lax.broadcasted_iota(jnp.int32, (GMLP_BLOCK, LANES), 1)
    lo = lane < GMLP_GROUP_DIM
    vzero = jnp.zeros((GMLP_BLOCK, LANES), BF16)

    blocks = []
    for n in range(MIX_TM // GMLP_BLOCK):
        slabs = []
        for p in range(GMLP_WIDTH // LANES):
            vs = v[n * GMLP_BLOCK:(n + 1) * GMLP_BLOCK, p * LANES:(p + 1) * LANES]
            mixed = (_dot(w[2 * p], jnp.where(lo, vs, vzero))
                     + _dot(w[2 * p + 1], jnp.where(lo, vzero, vs)) + sgb_ref[p])
            slabs.append(mixed)
        blocks.append(jnp.concatenate(slabs, axis=1))
    mixed = jnp.concatenate(blocks, axis=0)

    y_a = _dot((u * mixed).astype(BF16), wa_ref[...])
    y_b = _dot(attn_ref[...], wb_ref[...])
    gates = _sigmoid(_dot(xb, wg_ref[...]) + bg_ref[...])
    z = gates[:, :D_MODEL] * y_a + gates[:, D_MODEL:] * y_b
    mix = _dot(z.astype(BF16), wo_ref[...])
    h_ref[...] = _layer_norm(ALPHA * x + mix, g1_ref[...], b1_ref[...])


def _mixer(x2, attn2, wuv, buv, wg, bg, lng, lnb, sgw, sgb, wa, wb, wo, g1, b1):
    T, D = x2.shape
    c2 = lambda i: (0, 0)
    c3 = lambda i: (0, 0, 0)
    full = lambda a: pl.BlockSpec(a.shape, c2 if a.ndim == 2 else c3)
    return pl.pallas_call(
        _mixer_kernel,
        grid=(T // MIX_TM,),
        in_specs=[pl.BlockSpec((MIX_TM, D), lambda i: (i, 0)),
                  pl.BlockSpec((MIX_TM, DIFF_WIDTH), lambda i: (i, 0))]
                 + [full(a) for a in (wuv, buv, wg, bg, lng, lnb, sgw, sgb, wa, wb, wo, g1, b1)],
        out_specs=pl.BlockSpec((MIX_TM, D), lambda i: (i, 0)),
        out_shape=jax.ShapeDtypeStruct((T, D), F32),
        compiler_params=_cparams(("arbitrary",)),
        name="mixer",
    )(x2, attn2, wuv, buv, wg, bg, lng, lnb, sgw, sgb, wa, wb, wo, g1, b1)


def _router_kernel(h_ref, wr_hi_ref, wr_lo_ref, br_ref, eid_ref, wts_ref):
    hf = h_ref[...]
    hi = hf.astype(BF16)
    lo = (hf - hi.astype(F32)).astype(BF16)
    logits = (_dot_nt(wr_hi_ref[...], hi) + _dot_nt(wr_lo_ref[...], hi)
              + _dot_nt(wr_hi_ref[...], lo)) + br_ref[...]
    gidx = lax.broadcasted_iota(jnp.int32, (8, ROUTE_TM), 0)
    gl = jnp.where(gidx < N_GROUPS, logits[0:8], -jnp.inf)
    gmax = jnp.max(gl, axis=0, keepdims=True)
    g_sel = jnp.min(jnp.where(gl == gmax, gidx, N_GROUPS), axis=0, keepdims=True)
    g_w = 1.0 / jnp.sum(jnp.exp(gl - gmax), axis=0, keepdims=True)

    el = logits[8:8 + EXPERTS_PER_GROUP]
    for g in range(1, N_GROUPS):
        el = jnp.where(g_sel == g, logits[8 + g * EXPERTS_PER_GROUP:8 + (g + 1) * EXPERTS_PER_GROUP], el)
    eidx = lax.broadcasted_iota(jnp.int32, el.shape, 0)
    v1 = jnp.max(el, axis=0, keepdims=True)
    i1 = jnp.min(jnp.where(el == v1, eidx, EXPERTS_PER_GROUP), axis=0, keepdims=True)
    rest = jnp.where(eidx == i1, -jnp.inf, el)
    v2 = jnp.max(rest, axis=0, keepdims=True)
    i2 = jnp.min(jnp.where(rest == v2, eidx, EXPERTS_PER_GROUP), axis=0, keepdims=True)
    e21 = jnp.exp(v2 - v1)
    w1 = g_w / (1.0 + e21)
    w2 = g_w * e21 / (1.0 + e21)
    base = g_sel * EXPERTS_PER_GROUP
    eid_ref[0:1, :] = base + i1
    eid_ref[1:2, :] = base + i2
    wts_ref[0:1, :] = w1
    wts_ref[1:2, :] = w2


def _router(h1, wr_hi, wr_lo, br):
    T, D = h1.shape
    c2 = lambda i: (0, 0)
    return pl.pallas_call(
        _router_kernel,
        grid=(T // ROUTE_TM,),
        in_specs=[pl.BlockSpec((ROUTE_TM, D), lambda i: (i, 0)),
                  pl.BlockSpec(wr_hi.shape, c2), pl.BlockSpec(wr_lo.shape, c2), pl.BlockSpec(br.shape, c2)],
        out_specs=[pl.BlockSpec((EXPERT_TOPK, ROUTE_TM), lambda i: (0, i)),
                   pl.BlockSpec((EXPERT_TOPK, ROUTE_TM), lambda i: (0, i))],
        out_shape=[jax.ShapeDtypeStruct((EXPERT_TOPK, T), jnp.int32),
                   jax.ShapeDtypeStruct((EXPERT_TOPK, T), F32)],
        compiler_params=_cparams(("arbitrary",)),
        name="router",
    )(h1, wr_hi, wr_lo, br)


def _rank_kernel(eid_ref, tri_ref, rank_ref, cnt_ref, run_ref):
    @pl.when(pl.program_id(0) == 0)
    def _():
        run_ref[...] = jnp.zeros_like(run_ref)

    eid = eid_ref[...]
    eiota = lax.broadcasted_iota(jnp.int32, (N_EXPERTS, ROUTE_TM), 0)
    run = run_ref[...]
    ranks = []
    for c in range(EXPERT_TOPK):
        hot = eiota == eid[c:c + 1]
        hotf = hot.astype(F32)
        before = _dot(hotf.astype(BF16), tri_ref[...])
        ranks.append(jnp.sum(jnp.where(hot, run + before, 0.0), axis=0, keepdims=True))
        run = run + jnp.sum(hotf, axis=1, keepdims=True)
    run_ref[...] = run
    for c in range(EXPERT_TOPK):
        rank_ref[c:c + 1, :] = ranks[c].astype(jnp.int32)
    cnt_ref[...] = run.astype(jnp.int32)


def _ranks(eid, tri):
    _, T = eid.shape
    return pl.pallas_call(
        _rank_kernel,
        grid=(T // ROUTE_TM,),
        in_specs=[pl.BlockSpec((EXPERT_TOPK, ROUTE_TM), lambda i: (0, i)),
                  pl.BlockSpec(tri.shape, lambda i: (0, 0))],
        out_specs=[pl.BlockSpec((EXPERT_TOPK, ROUTE_TM), lambda i: (0, i)),
                   pl.BlockSpec((N_EXPERTS, 1), lambda i: (0, 0))],
        out_shape=[jax.ShapeDtypeStruct((EXPERT_TOPK, T), jnp.int32),
                   jax.ShapeDtypeStruct((N_EXPERTS, 1), jnp.int32)],
        scratch_shapes=[pltpu.VMEM((N_EXPERTS, 1), F32)],
        compiler_params=_cparams(("arbitrary",)),
        name="rank",
    )(eid, tri)


ROW_SUB = D_MODEL // LANES


def _row_copies(dest_refs, make_copy):
    def issue(t, _):
        for c in range(EXPERT_TOPK):
            make_copy(c, t, dest_refs[c][t]).start(priority=c)
        return 0

    lax.fori_loop(0, ROW_TM, issue, 0, unroll=ROW_UNROLL)

    def drain(t, _):
        for c in range(EXPERT_TOPK):
            make_copy(c, t, 0).wait()
        return 0

    lax.fori_loop(0, ROW_TM, drain, 0, unroll=ROW_UNROLL)


def _dispatch_kernel(cnt_ref, pad_ref, d0_ref, d1_ref, h_ref, xs_ref, rows_ref, zero_ref, sem, zsem):
    @pl.when(pl.program_id(0) == 0)
    def _():
        zero_ref[...] = jnp.zeros_like(zero_ref)

        def zero_copy(r):
            return pltpu.make_async_copy(zero_ref, xs_ref.at[r], zsem)

        def start(r, _):
            zero_copy(r).start()
            return 0

        def per_expert(e, carry):
            n, _ = carry
            lo = pad_ref[e] + cnt_ref[e]
            hi = pad_ref[e] + (cnt_ref[e] + DISPATCH_BLOCK - 1) // DISPATCH_BLOCK * DISPATCH_BLOCK
            lax.fori_loop(lo, hi, start, 0)
            return n + (hi - lo), hi

        n_pad, used = lax.fori_loop(0, N_EXPERTS, per_expert, (0, 0))
        lax.fori_loop(used, xs_ref.shape[0], start, 0)
        n_pad = n_pad + (xs_ref.shape[0] - used)

        def wait(_, c):
            zero_copy(0).wait()
            return c

        lax.fori_loop(0, n_pad, wait, 0)

    rows_ref[...] = h_ref[...].reshape(ROW_TM, ROW_SUB, LANES)

    def make_copy(c, t, dest):
        return pltpu.make_async_copy(rows_ref.at[t], xs_ref.at[dest], sem)

    _row_copies((d0_ref, d1_ref), make_copy)


def _dispatch(counts, pad_start, dest, h1, n_rows):
    T, D = h1.shape
    grid_spec = pltpu.PrefetchScalarGridSpec(
        num_scalar_prefetch=2,
        grid=(T // ROW_TM,),
        in_specs=[
            pl.BlockSpec((ROW_TM,), lambda i, c, p: (i,), memory_space=pltpu.SMEM),
            pl.BlockSpec((ROW_TM,), lambda i, c, p: (i,), memory_space=pltpu.SMEM),
            pl.BlockSpec((ROW_TM, D), lambda i, c, p: (i, 0)),
        ],
        out_specs=pl.BlockSpec(memory_space=pl.ANY),
        scratch_shapes=[pltpu.VMEM((ROW_TM, ROW_SUB, LANES), F32), pltpu.VMEM((ROW_SUB, LANES), F32),
                        pltpu.SemaphoreType.DMA(()), pltpu.SemaphoreType.DMA(())],
    )
    return pl.pallas_call(
        _dispatch_kernel,
        grid_spec=grid_spec,
        out_shape=jax.ShapeDtypeStruct((n_rows, ROW_SUB, LANES), F32),
        compiler_params=_cparams(("arbitrary",)),
        name="dispatch",
    )(counts, pad_start, dest[0], dest[1], h1)


def _expert_kernel(be_ref, nb_ref, xs_ref, wg_ref, wu_ref, wd_ref, o_ref):
    i = pl.program_id(0)

    @pl.when(i < nb_ref[0])
    def _():
        xb = xs_ref[...].reshape(DISPATCH_BLOCK, D_MODEL).astype(BF16)
        gate = _dot(xb, wg_ref[...])
        up = _dot(xb, wu_ref[...])
        hb = (gate * _sigmoid(gate) * up).astype(BF16)
        o_ref[...] = _dot(hb, wd_ref[...]).reshape(DISPATCH_BLOCK, ROW_SUB, LANES)

    @pl.when(i >= nb_ref[0])
    def _():
        o_ref[...] = jnp.zeros_like(o_ref)


def _experts(block_e, n_used, xs, wg, wu, wd):
    P = xs.shape[0]
    D = D_MODEL
    rows = pl.BlockSpec((DISPATCH_BLOCK, ROW_SUB, LANES), lambda i, be, nb: (i, 0, 0))
    grid_spec = pltpu.PrefetchScalarGridSpec(
        num_scalar_prefetch=2,
        grid=(P // DISPATCH_BLOCK,),
        in_specs=[
            rows,
            pl.BlockSpec((None, D, EXPERT_HIDDEN), lambda i, be, nb: (be[i], 0, 0)),
            pl.BlockSpec((None, D, EXPERT_HIDDEN), lambda i, be, nb: (be[i], 0, 0)),
            pl.BlockSpec((None, EXPERT_HIDDEN, D), lambda i, be, nb: (be[i], 0, 0)),
        ],
        out_specs=rows,
    )
    return pl.pallas_call(
        _expert_kernel,
        grid_spec=grid_spec,
        out_shape=jax.ShapeDtypeStruct(xs.shape, F32),
        compiler_params=_cparams(("arbitrary",)),
        name="experts",
    )(block_e, n_used, xs, wg, wu, wd)


def _combine_kernel(d0_ref, d1_ref, n0_ref, n1_ref, h_ref, wts_ref, g2_ref, b2_ref, ys_ref, o_ref,
                    rows_ref, sems):
    i = pl.program_id(0)
    n = pl.num_programs(0)
    slot = i % 2

    def gather(dest_refs, buf):
        def issue(t, _):
            for c in range(EXPERT_TOPK):
                pltpu.make_async_copy(ys_ref.at[dest_refs[c][t]], rows_ref.at[buf, c, t],
                                      sems.at[buf]).start(priority=c)
            return 0

        lax.fori_loop(0, ROW_TM, issue, 0, unroll=ROW_UNROLL)

    @pl.when(i == 0)
    def _():
        gather((d0_ref, d1_ref), 0)

    @pl.when(i + 1 < n)
    def _():
        gather((n0_ref, n1_ref), 1 - slot)

    def drain(t, _):
        for c in range(EXPERT_TOPK):
            pltpu.make_async_copy(ys_ref.at[0], rows_ref.at[slot, c, t], sems.at[slot]).wait()
        return 0

    lax.fori_loop(0, ROW_TM, drain, 0, unroll=ROW_UNROLL)
    w = wts_ref[...]
    ffn = (w[:, 0:1] * rows_ref[slot, 0].reshape(ROW_TM, D_MODEL)
           + w[:, 1:2] * rows_ref[slot, 1].reshape(ROW_TM, D_MODEL))
    o_ref[...] = _layer_norm(ALPHA * h_ref[...] + ffn, g2_ref[...], b2_ref[...])


def _combine(dest, h1, wts_t, g2, b2, ys):
    T, D = h1.shape
    n = T // ROW_TM
    cur = pl.BlockSpec((ROW_TM,), lambda i: (i,), memory_space=pltpu.SMEM)
    nxt = pl.BlockSpec((ROW_TM,), lambda i: (jnp.minimum(i + 1, n - 1),), memory_space=pltpu.SMEM)
    return pl.pallas_call(
        _combine_kernel,
        grid=(n,),
        in_specs=[
            cur, cur, nxt, nxt,
            pl.BlockSpec((ROW_TM, D), lambda i: (i, 0)),
            pl.BlockSpec((ROW_TM, EXPERT_TOPK), lambda i: (i, 0)),
            pl.BlockSpec((1, D), lambda i: (0, 0)),
            pl.BlockSpec((1, D), lambda i: (0, 0)),
            pl.BlockSpec(memory_space=pl.ANY),
        ],
        out_specs=pl.BlockSpec((ROW_TM, D), lambda i: (i, 0)),
        scratch_shapes=[pltpu.VMEM((2, EXPERT_TOPK, ROW_TM, ROW_SUB, LANES), F32),
                        pltpu.SemaphoreType.DMA((2,))],
        out_shape=jax.ShapeDtypeStruct((T, D), F32),
        compiler_params=_cparams(("arbitrary",)),
        name="combine",
    )(dest[0], dest[1], dest[0], dest[1], h1, wts_t, g2, b2, ys)


def _attn_bias_tables(S):
    c = jnp.asarray((2.0 ** (-8.0 * np.arange(1, DIFF_HEADS + 1, dtype=np.float64) / DIFF_HEADS)) * LOG2E, F32)
    c_hi = c.astype(BF16)
    c_mid = (c - c_hi.astype(F32)).astype(BF16)
    c_lo = (c - c_hi.astype(F32) - c_mid.astype(F32)).astype(BF16)
    pieces = jnp.stack([c_hi, c_mid, c_lo], axis=1)
    qfeat = jnp.concatenate([pieces * LANES, pieces, jnp.zeros((DIFF_HEADS, LANES - 6), BF16)], axis=1)
    j = np.arange(S)
    pos = np.zeros((S, LANES), np.float32)
    pos[:, 0:3] = (j // LANES)[:, None]
    pos[:, 3:6] = (j % LANES)[:, None]
    kj = np.arange(ATT_TK, dtype=np.float64)[:, None]
    qi = np.tile(np.arange(ATT_TQ, dtype=np.float64), 2)[None, :]
    visible = (kj // CHUNK) <= (qi // CHUNK)
    after = 2.0 * np.minimum(qi - kj, 0.0)
    corr = jnp.where(visible[None], c[:, None, None] * jnp.asarray(after, F32)[None], MASKED)
    return jnp.asarray(pos, BF16), qfeat[:, :, None], corr.astype(F32)


def _layer(h, layer, w_in, b_in, sg_ln_g, sg_ln_b, sg_w, sg_b, w_branch_a, lam_q1, lam_k1, lam_q2, lam_k2,
           subln_g, w_branch_b, w_out, ln1_g, ln1_b, w_group, b_group, w_expert, b_expert,
           w_gate, w_up, w_down, ln2_g, ln2_b):
    B, S, D = h.shape
    T = B * S
    row = lambda a: a.reshape(1, -1)
    col = lambda a: a.reshape(-1, 1)

    wk = w_in[:, OFF_K:OFF_VAL].astype(BF16)
    wqt = w_in[:, OFF_Q:OFF_K].T.astype(BF16)
    wvt = w_in[:, OFF_VAL:OFF_GA].T.astype(BF16)
    k, qt, vt = _qkv_proj(h, wk, row(b_in[OFF_K:OFF_VAL]), wqt, col(b_in[OFF_Q:OFF_K]),
                          wvt, col(b_in[OFF_VAL:OFF_GA]))
    pos, qfeat, corr = _attn_bias_tables(S)
    lambda_init = 0.8 - 0.6 * math.exp(-0.3 * layer)
    lamv = jnp.stack([lam_q1, lam_k1, lam_q2, lam_k2]).astype(F32)
    attn = _diff_attention(qt, k, pos, vt, qfeat, corr, lamv, col(subln_g), lambda_init)

    sgb = jnp.repeat(sg_b.reshape(GMLP_GROUPS // 2, 2, GMLP_BLOCK), GMLP_GROUP_DIM, axis=1)
    sgb = jnp.transpose(sgb, (0, 2, 1))
    h1 = _mixer(h.reshape(T, D), attn.reshape(T, DIFF_WIDTH),
                w_in[:, OFF_U:OFF_Q].astype(BF16), row(b_in[OFF_U:OFF_Q]),
                w_in[:, OFF_GA:IN_WIDTH].astype(BF16), row(b_in[OFF_GA:IN_WIDTH]),
                row(sg_ln_g), row(sg_ln_b), sg_w, sgb,
                w_branch_a.astype(BF16), w_branch_b.astype(BF16), w_out.astype(BF16),
                row(ln1_g), row(ln1_b))

    wr = jnp.zeros((LANES, D), F32)
    wr = wr.at[0:N_GROUPS].set(w_group.T).at[8:8 + N_EXPERTS].set(w_expert.T)
    wr_hi = wr.astype(BF16)
    wr_lo = (wr - wr_hi.astype(F32)).astype(BF16)
    br = jnp.zeros((LANES, 1), F32)
    br = br.at[0:N_GROUPS, 0].set(b_group.astype(F32)).at[8:8 + N_EXPERTS, 0].set(b_expert.astype(F32))
    eid, wts = _router(h1, wr_hi, wr_lo, br)

    tri = jnp.asarray(np.triu(np.ones((ROUTE_TM, ROUTE_TM), np.float32), k=1), BF16)
    rank, counts = _ranks(eid, tri)

    counts = counts[:, 0]
    padded = (counts + DISPATCH_BLOCK - 1) // DISPATCH_BLOCK * DISPATCH_BLOCK
    pad_end = jnp.cumsum(padded)
    pad_start = (pad_end - padded).astype(jnp.int32)
    P = T * EXPERT_TOPK + N_EXPERTS * DISPATCH_BLOCK
    n_blocks = P // DISPATCH_BLOCK
    block_start = jnp.arange(n_blocks, dtype=jnp.int32) * DISPATCH_BLOCK
    block_e = jnp.minimum(jnp.sum(pad_end[None, :] <= block_start[:, None], axis=1), N_EXPERTS - 1).astype(jnp.int32)
    n_used = (pad_end[-1:] // DISPATCH_BLOCK).astype(jnp.int32)

    hot = eid[:, :, None] == jnp.arange(N_EXPERTS, dtype=jnp.int32)
    dest = jnp.sum(jnp.where(hot, pad_start, 0), axis=-1) + rank
    xs = _dispatch(counts, pad_start, dest, h1, P)
    ys = _experts(block_e, n_used, xs, w_gate.astype(BF16), w_up.astype(BF16), w_down.astype(BF16))
    out = _combine(dest, h1, wts.T, row(ln2_g), row(ln2_b), ys)
    return out.reshape(B, S, D)


def kernel(x, w_in, b_in, sg_ln_g, sg_ln_b, sg_w, sg_b, w_branch_a, lam_q1, lam_k1, lam_q2, lam_k2, subln_g,
           w_branch_b, w_out, ln1_g, ln1_b, w_group, b_group, w_expert, b_expert, w_gate, w_up, w_down,
           ln2_g, ln2_b):
    h = x
    for layer in range(DEPTH):
        h = _layer(h, layer, w_in[layer], b_in[layer], sg_ln_g[layer], sg_ln_b[layer], sg_w[layer],
                   sg_b[layer], w_branch_a[layer], lam_q1[layer], lam_k1[layer], lam_q2[layer],
                   lam_k2[layer], subln_g[layer], w_branch_b[layer], w_out[layer], ln1_g[layer],
                   ln1_b[layer], w_group[layer], b_group[layer], w_expert[layer], b_expert[layer],
                   w_gate[layer], w_up[layer], w_down[layer], ln2_g[layer], ln2_b[layer])
    return h
```

```python
import functools
import math

import jax
import jax.numpy as jnp
import numpy as np
from jax import lax
from jax.experimental import pallas as pl
from jax.experimental.pallas import tpu as pltpu

D_MODEL = 1024
DEPTH = 1
CHUNK = 64
GMLP_GROUPS = 8
GMLP_GROUP_DIM = 64
GMLP_WIDTH = GMLP_GROUPS * GMLP_GROUP_DIM
GMLP_BLOCK = 128
DIFF_HEADS = 8
DIFF_QK_DIM = 64
DIFF_V_DIM = 2 * DIFF_QK_DIM
DIFF_QK_WIDTH = DIFF_HEADS * 2 * DIFF_QK_DIM
DIFF_WIDTH = DIFF_HEADS * DIFF_V_DIM
N_GROUPS = 4
EXPERTS_PER_GROUP = 8
N_EXPERTS = N_GROUPS * EXPERTS_PER_GROUP
EXPERT_TOPK = 2
EXPERT_HIDDEN = 512
DISPATCH_BLOCK = 256
ALPHA = (2.0 * DEPTH) ** 0.25
LN_EPS = 1e-5
RMS_EPS = 1e-5
OFF_U = 0
OFF_V = OFF_U + GMLP_WIDTH
OFF_Q = OFF_V + GMLP_WIDTH
OFF_K = OFF_Q + DIFF_QK_WIDTH
OFF_VAL = OFF_K + DIFF_QK_WIDTH
OFF_GA = OFF_VAL + DIFF_WIDTH
OFF_GB = OFF_GA + D_MODEL
IN_WIDTH = OFF_GB + D_MODEL

LANES = 128
HEAD_W = 2 * DIFF_QK_DIM
MASKED = -1e30
VMEM_LIMIT = 56 * 1024 * 1024

PROJ_TM = 512
ATT_T = 256
ATT_TK = 2 * ATT_T
ATT_TQ = ATT_TK
ATT_HB = 2
LOG2E = math.log2(math.e)
VT_ROWS = HEAD_W + 16
MIX_TM = 512
ROUTE_TM = 512
ROW_TM = 512
EXPERT_BLOCK = 2 * DISPATCH_BLOCK
ROW_UNROLL = 8

BF16 = jnp.bfloat16
F32 = jnp.float32


def _cparams(sem):
    return pltpu.CompilerParams(dimension_semantics=sem, vmem_limit_bytes=VMEM_LIMIT)


def _dot(a, b):
    return jnp.dot(a, b, preferred_element_type=F32)


def _dot_nt(a, b):
    return lax.dot_general(a, b, (((1,), (1,)), ((), ())), preferred_element_type=F32)


def _gelu(x):
    return 0.5 * x * (1.0 + lax.erf(x * (2.0 ** -0.5)))


def _sigmoid(x):
    return 1.0 / (1.0 + jnp.exp(-x))


def _layer_norm(x, g, b):
    mu = jnp.mean(x, axis=-1, keepdims=True)
    xc = x - mu
    var = jnp.mean(xc * xc, axis=-1, keepdims=True)
    return xc * lax.rsqrt(var + LN_EPS) * g + b


def _qkv_kernel(x_ref, wk_ref, bk_ref, wqt_ref, bq_ref, wvt_ref, bv_ref, k_ref, qt_ref, vt_ref):
    xb = x_ref[...].astype(BF16)
    k_ref[...] = (_dot(xb, wk_ref[...]) + bk_ref[...]).astype(BF16)
    qt = (_dot_nt(wqt_ref[...], xb) + bq_ref[...]) * (DIFF_QK_DIM ** -0.5 * LOG2E)
    qt_ref[...] = qt.astype(BF16)
    vt = (_dot_nt(wvt_ref[...], xb) + bv_ref[...]).astype(BF16)
    ones_row = (lax.broadcasted_iota(jnp.int32, (VT_ROWS - HEAD_W, ATT_TK), 0) == 0).astype(BF16)
    for hh in range(DIFF_HEADS):
        for j in range(PROJ_TM // ATT_TK):
            vt_ref[hh, j, :HEAD_W] = vt[hh * HEAD_W:(hh + 1) * HEAD_W, j * ATT_TK:(j + 1) * ATT_TK]
            vt_ref[hh, j, HEAD_W:] = ones_row


def _qkv_proj(x, wk, bk, wqt, bq, wvt, bv):
    B, S, D = x.shape
    nk = S // ATT_TK
    jt = PROJ_TM // ATT_TK
    const2 = lambda b, i: (0, 0)
    return pl.pallas_call(
        _qkv_kernel,
        grid=(B, S // PROJ_TM),
        in_specs=[
            pl.BlockSpec((None, PROJ_TM, D), lambda b, i: (b, i, 0)),
            pl.BlockSpec((D, DIFF_QK_WIDTH), const2),
            pl.BlockSpec((1, DIFF_QK_WIDTH), const2),
            pl.BlockSpec((DIFF_QK_WIDTH, D), const2),
            pl.BlockSpec((DIFF_QK_WIDTH, 1), const2),
            pl.BlockSpec((DIFF_WIDTH, D), const2),
            pl.BlockSpec((DIFF_WIDTH, 1), const2),
        ],
        out_specs=[
            pl.BlockSpec((None, PROJ_TM, DIFF_QK_WIDTH), lambda b, i: (b, i, 0)),
            pl.BlockSpec((None, DIFF_QK_WIDTH, PROJ_TM), lambda b, i: (b, 0, i)),
            pl.BlockSpec((None, DIFF_HEADS, jt, VT_ROWS, ATT_TK), lambda b, i: (b, 0, i, 0, 0)),
        ],
        out_shape=[
            jax.ShapeDtypeStruct((B, S, DIFF_QK_WIDTH), BF16),
            jax.ShapeDtypeStruct((B, DIFF_QK_WIDTH, S), BF16),
            jax.ShapeDtypeStruct((B, DIFF_HEADS, nk, VT_ROWS, ATT_TK), BF16),
        ],
        compiler_params=_cparams(("arbitrary", "arbitrary")),
        name="qkv_proj",
    )(x, wk, bk, wqt, bq, wvt, bv)


def _attn_kernel(qt_ref, k_ref, pos_ref, vt_ref, qf_ref, corr_ref, lamv_ref, g_ref, o_ref,
                 q12_ref, sa_ref, sb_ref, ta_ref, tb_ref, m_ref, acc_ref, *, lambda_init):
    qi = pl.program_id(2)
    T = ATT_T
    TQ = ATT_TQ
    n = qi + 1

    row = lax.broadcasted_iota(jnp.int32, (HEAD_W, TQ), 0)
    zero = jnp.zeros((HEAD_W, TQ), BF16)
    for e in range(ATT_HB):
        q = qt_ref[e * HEAD_W:(e + 1) * HEAD_W, :]
        q12_ref[e, :HEAD_W, :TQ] = jnp.where(row < DIFF_QK_DIM, q, zero)
        q12_ref[e, :HEAD_W, TQ:] = jnp.where(row >= DIFF_QK_DIM, q, zero)
        q12_ref[e, HEAD_W:, :] = jnp.broadcast_to(qf_ref[e], (LANES, 2 * TQ))
    acc_ref[...] = jnp.zeros_like(acc_ref)
    m_ref[...] = jnp.full(m_ref.shape, MASKED, F32)

    def scores(g, s_ref, t_ref, diagonal):
        rows = pl.ds(pl.multiple_of(g * 2 * T, 2 * T), 2 * T)
        for e in range(ATT_HB):
            kx = jnp.concatenate([k_ref[rows, e * HEAD_W:(e + 1) * HEAD_W], pos_ref[rows, :]], axis=1)
            t = _dot(kx, q12_ref[e])
            if diagonal:
                t = t + corr_ref[e]
            s_ref[e] = t
            t_ref[e] = jnp.max(t, axis=0, keepdims=True)

    def accumulate(g, s_ref, t_ref):
        for e in range(ATT_HB):
            m = m_ref[e]
            m_new = jnp.maximum(m, t_ref[e])
            m_ref[e] = m_new
            p = jnp.exp2(s_ref[e] - m_new).astype(BF16)
            acc_ref[e] = jnp.exp2(m - m_new) * acc_ref[e] + _dot(vt_ref[e, g], p)

    @pl.when(n == 1)
    def _():
        scores(0, sa_ref, ta_ref, True)
        accumulate(0, sa_ref, ta_ref)

    @pl.when(n > 1)
    def _():
        scores(0, sa_ref, ta_ref, False)

    def pair(i, _):
        g = 2 * i
        scores(g + 1, sb_ref, tb_ref, False)
        accumulate(g, sa_ref, ta_ref)
        scores(g + 2, sa_ref, ta_ref, False)
        accumulate(g + 1, sb_ref, tb_ref)
        return 0

    lax.fori_loop(0, jnp.maximum(n - 2, 0) // 2, pair, 0)

    @pl.when((n > 1) & (n % 2 == 0))
    def _():
        scores(n - 1, sb_ref, tb_ref, True)
        accumulate(n - 2, sa_ref, ta_ref)
        accumulate(n - 1, sb_ref, tb_ref)

    @pl.when((n > 1) & (n % 2 == 1))
    def _():
        scores(n - 2, sb_ref, tb_ref, False)
        accumulate(n - 3, sa_ref, ta_ref)
        scores(n - 1, sa_ref, ta_ref, True)
        accumulate(n - 2, sb_ref, tb_ref)
        accumulate(n - 1, sa_ref, ta_ref)

    lv = lamv_ref[...]
    lam = (jnp.exp(jnp.sum(lv[0:1] * lv[1:2], axis=1, keepdims=True))
           - jnp.exp(jnp.sum(lv[2:3] * lv[3:4], axis=1, keepdims=True)) + lambda_init)
    for e in range(ATT_HB):
        o12 = acc_ref[e, :HEAD_W] * (1.0 / acc_ref[e, HEAD_W:HEAD_W + 1])
        o = o12[:, :TQ] - lam * o12[:, TQ:]
        o = o * lax.rsqrt(jnp.mean(o * o, axis=0, keepdims=True) + RMS_EPS)
        o = o * g_ref[...] * (1.0 - lambda_init)
        o_ref[:, e * HEAD_W:(e + 1) * HEAD_W] = o.T.astype(BF16)


def _diff_attention(qt, k, pos, vt, qfeat, corr, lamv, gcol, lambda_init):
    B, S, _ = k.shape
    nq = S // ATT_TQ
    gw = ATT_HB * HEAD_W
    lanes = 2 * ATT_TQ
    return pl.pallas_call(
        functools.partial(_attn_kernel, lambda_init=lambda_init),
        grid=(B, DIFF_HEADS // ATT_HB, nq),
        in_specs=[
            pl.BlockSpec((None, gw, ATT_TQ), lambda b, h, i: (b, h, i)),
            pl.BlockSpec((None, S, gw), lambda b, h, i: (b, 0, h)),
            pl.BlockSpec((S, LANES), lambda b, h, i: (0, 0)),
            pl.BlockSpec((None, ATT_HB, S // ATT_TK, VT_ROWS, ATT_TK), lambda b, h, i: (b, h, 0, 0, 0)),
            pl.BlockSpec((ATT_HB, LANES, 1), lambda b, h, i: (h, 0, 0)),
            pl.BlockSpec((ATT_HB, ATT_TK, lanes), lambda b, h, i: (h, 0, 0)),
            pl.BlockSpec((4, DIFF_QK_DIM), lambda b, h, i: (0, 0)),
            pl.BlockSpec((HEAD_W, 1), lambda b, h, i: (0, 0)),
        ],
        out_specs=pl.BlockSpec((None, ATT_TQ, gw), lambda b, h, i: (b, i, h)),
        scratch_shapes=[
            pltpu.VMEM((ATT_HB, HEAD_W + LANES, lanes), BF16),
            pltpu.VMEM((ATT_HB, ATT_TK, lanes), F32),
            pltpu.VMEM((ATT_HB, ATT_TK, lanes), F32),
            pltpu.VMEM((ATT_HB, 1, lanes), F32),
            pltpu.VMEM((ATT_HB, 1, lanes), F32),
            pltpu.VMEM((ATT_HB, 1, lanes), F32),
            pltpu.VMEM((ATT_HB, VT_ROWS, lanes), F32),
        ],
        out_shape=jax.ShapeDtypeStruct((B, S, DIFF_WIDTH), BF16),
        compiler_params=_cparams(("arbitrary", "arbitrary", "arbitrary")),
        name="diff_attn",
    )(qt, k, pos, vt, qfeat, corr, lamv, gcol)


def _mixer_kernel(x_ref, attn_ref, wuv_ref, buv_ref, wg_ref, bg_ref, lng_ref, lnb_ref, sgw_ref, sgb_ref,
                  wa_ref, wb_ref, wo_ref, g1_ref, b1_ref, h_ref):
    x = x_ref[...]
    xb = x.astype(BF16)
    uv = _gelu(_dot(xb, wuv_ref[...]) + buv_ref[...])
    u = uv[:, :GMLP_WIDTH]
    v = _layer_norm(uv[:, GMLP_WIDTH:], lng_ref[...], lnb_ref[...]).astype(BF16)

    ti = lax.broadcasted_iota(jnp.int32, (GMLP_BLOCK, GMLP_BLOCK), 0) // CHUNK
    si = lax.broadcasted_iota(jnp.int32, (GMLP_BLOCK, GMLP_BLOCK), 1) // CHUNK
    causal = ti >= si
    w = [jnp.where(causal, sgw_ref[g], 0.0).astype(BF16) for g in range(GMLP_GROUPS)]
    lane = lax.broadcasted_iota(jnp.int32, (GMLP_BLOCK, LANES), 1)
    lo = lane < GMLP_GROUP_DIM
    vzero = jnp.zeros((GMLP_BLOCK, LANES), BF16)

    blocks = []
    for n in range(MIX_TM // GMLP_BLOCK):
        slabs = []
        for p in range(GMLP_WIDTH // LANES):
            vs = v[n * GMLP_BLOCK:(n + 1) * GMLP_BLOCK, p * LANES:(p + 1) * LANES]
            mixed = (_dot(w[2 * p], jnp.where(lo, vs, vzero))
                     + _dot(w[2 * p + 1], jnp.where(lo, vzero, vs)) + sgb_ref[p])
            slabs.append(mixed)
        blocks.append(jnp.concatenate(slabs, axis=1))
    mixed = jnp.concatenate(blocks, axis=0)

    y_a = _dot((u * mixed).astype(BF16), wa_ref[...])
    y_b = _dot(attn_ref[...], wb_ref[...])
    gates = _sigmoid(_dot(xb, wg_ref[...]) + bg_ref[...])
    z = gates[:, :D_MODEL] * y_a + gates[:, D_MODEL:] * y_b
    mix = _dot(z.astype(BF16), wo_ref[...])
    h_ref[...] = _layer_norm(ALPHA * x + mix, g1_ref[...], b1_ref[...])


def _mixer(x2, attn2, wuv, buv, wg, bg, lng, lnb, sgw, sgb, wa, wb, wo, g1, b1):
    T, D = x2.shape
    c2 = lambda i: (0, 0)
    c3 = lambda i: (0, 0, 0)
    full = lambda a: pl.BlockSpec(a.shape, c2 if a.ndim == 2 else c3)
    return pl.pallas_call(
        _mixer_kernel,
        grid=(T // MIX_TM,),
        in_specs=[pl.BlockSpec((MIX_TM, D), lambda i: (i, 0)),
                  pl.BlockSpec((MIX_TM, DIFF_WIDTH), lambda i: (i, 0))]
                 + [full(a) for a in (wuv, buv, wg, bg, lng, lnb, sgw, sgb, wa, wb, wo, g1, b1)],
        out_specs=pl.BlockSpec((MIX_TM, D), lambda i: (i, 0)),
        out_shape=jax.ShapeDtypeStruct((T, D), F32),
        compiler_params=_cparams(("arbitrary",)),
        name="mixer",
    )(x2, attn2, wuv, buv, wg, bg, lng, lnb, sgw, sgb, wa, wb, wo, g1, b1)


def _router_kernel(h_ref, wr_hi_ref, wr_lo_ref, br_ref, eid_ref, wts_ref):
    hf = h_ref[...]
    hi = hf.astype(BF16)
    lo = (hf - hi.astype(F32)).astype(BF16)
    logits = (_dot_nt(wr_hi_ref[...], hi) + _dot_nt(wr_lo_ref[...], hi)
              + _dot_nt(wr_hi_ref[...], lo)) + br_ref[...]
    gidx = lax.broadcasted_iota(jnp.int32, (8, ROUTE_TM), 0)
    gl = jnp.where(gidx < N_GROUPS, logits[0:8], -jnp.inf)
    gmax = jnp.max(gl, axis=0, keepdims=True)
    g_sel = jnp.min(jnp.where(gl == gmax, gidx, N_GROUPS), axis=0, keepdims=True)
    g_w = 1.0 / jnp.sum(jnp.exp(gl - gmax), axis=0, keepdims=True)

    el = logits[8:8 + EXPERTS_PER_GROUP]
    for g in range(1, N_GROUPS):
        el = jnp.where(g_sel == g, logits[8 + g * EXPERTS_PER_GROUP:8 + (g + 1) * EXPERTS_PER_GROUP], el)
    eidx = lax.broadcasted_iota(jnp.int32, el.shape, 0)
    v1 = jnp.max(el, axis=0, keepdims=True)
    i1 = jnp.min(jnp.where(el == v1, eidx, EXPERTS_PER_GROUP), axis=0, keepdims=True)
    rest = jnp.where(eidx == i1, -jnp.inf, el)
    v2 = jnp.max(rest, axis=0, keepdims=True)
    i2 = jnp.min(jnp.where(rest == v2, eidx, EXPERTS_PER_GROUP), axis=0, keepdims=True)
    e21 = jnp.exp(v2 - v1)
    w1 = g_w / (1.0 + e21)
    w2 = g_w * e21 / (1.0 + e21)
    base = g_sel * EXPERTS_PER_GROUP
    eid_ref[0:1, :] = base + i1
    eid_ref[1:2, :] = base + i2
    wts_ref[0:1, :] = w1
    wts_ref[1:2, :] = w2


def _router(h1, wr_hi, wr_lo, br):
    T, D = h1.shape
    c2 = lambda i: (0, 0)
    return pl.pallas_call(
        _router_kernel,
        grid=(T // ROUTE_TM,),
        in_specs=[pl.BlockSpec((ROUTE_TM, D), lambda i: (i, 0)),
                  pl.BlockSpec(wr_hi.shape, c2), pl.BlockSpec(wr_lo.shape, c2), pl.BlockSpec(br.shape, c2)],
        out_specs=[pl.BlockSpec((EXPERT_TOPK, ROUTE_TM), lambda i: (0, i)),
                   pl.BlockSpec((EXPERT_TOPK, ROUTE_TM), lambda i: (0, i))],
        out_shape=[jax.ShapeDtypeStruct((EXPERT_TOPK, T), jnp.int32),
                   jax.ShapeDtypeStruct((EXPERT_TOPK, T), F32)],
        compiler_params=_cparams(("arbitrary",)),
        name="router",
    )(h1, wr_hi, wr_lo, br)


def _rank_kernel(eid_ref, tri_ref, rank_ref, cnt_ref, run_ref):
    @pl.when(pl.program_id(0) == 0)
    def _():
        run_ref[...] = jnp.zeros_like(run_ref)

    eid = eid_ref[...]
    eiota = lax.broadcasted_iota(jnp.int32, (N_EXPERTS, ROUTE_TM), 0)
    run = run_ref[...]
    ranks = []
    for c in range(EXPERT_TOPK):
        hot = eiota == eid[c:c + 1]
        hotf = hot.astype(F32)
        before = _dot(hotf.astype(BF16), tri_ref[...])
        ranks.append(jnp.sum(jnp.where(hot, run + before, 0.0), axis=0, keepdims=True))
        run = run + jnp.sum(hotf, axis=1, keepdims=True)
    run_ref[...] = run
    for c in range(EXPERT_TOPK):
        rank_ref[c:c + 1, :] = ranks[c].astype(jnp.int32)
    cnt_ref[...] = run.astype(jnp.int32)


def _ranks(eid, tri):
    _, T = eid.shape
    return pl.pallas_call(
        _rank_kernel,
        grid=(T // ROUTE_TM,),
        in_specs=[pl.BlockSpec((EXPERT_TOPK, ROUTE_TM), lambda i: (0, i)),
                  pl.BlockSpec(tri.shape, lambda i: (0, 0))],
        out_specs=[pl.BlockSpec((EXPERT_TOPK, ROUTE_TM), lambda i: (0, i)),
                   pl.BlockSpec((N_EXPERTS, 1), lambda i: (0, 0))],
        out_shape=[jax.ShapeDtypeStruct((EXPERT_TOPK, T), jnp.int32),
                   jax.ShapeDtypeStruct((N_EXPERTS, 1), jnp.int32)],
        scratch_shapes=[pltpu.VMEM((N_EXPERTS, 1), F32)],
        compiler_params=_cparams(("arbitrary",)),
        name="rank",
    )(eid, tri)


ROW_SUB = D_MODEL // LANES


def _dispatch_kernel(cnt_ref, pad_ref, d0_ref, d1_ref, h_ref, xs_ref, rows_ref, zero_ref, sems, zsem):
    i = pl.program_id(0)
    slot = i % 2
    dest_refs = (d0_ref, d1_ref)

    @pl.when(i == 0)
    def _():
        zero_ref[...] = jnp.zeros_like(zero_ref)

        def zero_copy(r):
            return pltpu.make_async_copy(zero_ref, xs_ref.at[r], zsem)

        def start(r, _):
            zero_copy(r).start()
            return 0

        def per_expert(e, carry):
            n, _ = carry
            lo = pad_ref[e] + cnt_ref[e]
            hi = pad_ref[e] + (cnt_ref[e] + EXPERT_BLOCK - 1) // EXPERT_BLOCK * EXPERT_BLOCK
            lax.fori_loop(lo, hi, start, 0)
            return n + (hi - lo), hi

        n_pad, used = lax.fori_loop(0, N_EXPERTS, per_expert, (0, 0))
        lax.fori_loop(used, xs_ref.shape[0], start, 0)
        n_pad = n_pad + (xs_ref.shape[0] - used)

        def wait(_, c):
            zero_copy(0).wait()
            return c

        lax.fori_loop(0, n_pad, wait, 0)

    rows_ref[slot] = h_ref[...].reshape(ROW_TM, ROW_SUB, LANES)

    def issue(t, _):
        for c in range(EXPERT_TOPK):
            pltpu.make_async_copy(rows_ref.at[slot, t], xs_ref.at[dest_refs[c][t]],
                                  sems.at[slot]).start(priority=c)
        return 0

    lax.fori_loop(0, ROW_TM, issue, 0, unroll=ROW_UNROLL)

    def drain(buf):
        def wait(t, _):
            for c in range(EXPERT_TOPK):
                pltpu.make_async_copy(rows_ref.at[buf, t], xs_ref.at[0], sems.at[buf]).wait()
            return 0

        lax.fori_loop(0, ROW_TM, wait, 0, unroll=ROW_UNROLL)

    @pl.when(i > 0)
    def _():
        drain(1 - slot)

    @pl.when(i == pl.num_programs(0) - 1)
    def _():
        drain(slot)


def _dispatch(counts, pad_start, dest, h1, n_rows):
    T, D = h1.shape
    grid_spec = pltpu.PrefetchScalarGridSpec(
        num_scalar_prefetch=2,
        grid=(T // ROW_TM,),
        in_specs=[
            pl.BlockSpec((ROW_TM,), lambda i, c, p: (i,), memory_space=pltpu.SMEM),
            pl.BlockSpec((ROW_TM,), lambda i, c, p: (i,), memory_space=pltpu.SMEM),
            pl.BlockSpec((ROW_TM, D), lambda i, c, p: (i, 0)),
        ],
        out_specs=pl.BlockSpec(memory_space=pl.ANY),
        scratch_shapes=[pltpu.VMEM((2, ROW_TM, ROW_SUB, LANES), F32), pltpu.VMEM((ROW_SUB, LANES), F32),
                        pltpu.SemaphoreType.DMA((2,)), pltpu.SemaphoreType.DMA(())],
    )
    return pl.pallas_call(
        _dispatch_kernel,
        grid_spec=grid_spec,
        out_shape=jax.ShapeDtypeStruct((n_rows, ROW_SUB, LANES), F32),
        compiler_params=_cparams(("arbitrary",)),
        name="dispatch",
    )(counts, pad_start, dest[0], dest[1], h1)


def _expert_kernel(be_ref, nb_ref, xs_ref, wg_ref, wu_ref, wd_ref, o_ref):
    i = pl.program_id(0)

    @pl.when(i < nb_ref[0])
    def _():
        xb = xs_ref[...].reshape(EXPERT_BLOCK, D_MODEL).astype(BF16)
        gate = _dot(xb, wg_ref[...])
        up = _dot(xb, wu_ref[...])
        hb = (gate * _sigmoid(gate) * up).astype(BF16)
        o_ref[...] = _dot(hb, wd_ref[...]).reshape(EXPERT_BLOCK, ROW_SUB, LANES)

    @pl.when(i >= nb_ref[0])
    def _():
        o_ref[...] = jnp.zeros_like(o_ref)


def _experts(block_e, n_used, xs, wg, wu, wd):
    P = xs.shape[0]
    D = D_MODEL
    rows = pl.BlockSpec((EXPERT_BLOCK, ROW_SUB, LANES), lambda i, be, nb: (i, 0, 0))
    grid_spec = pltpu.PrefetchScalarGridSpec(
        num_scalar_prefetch=2,
        grid=(P // EXPERT_BLOCK,),
        in_specs=[
            rows,
            pl.BlockSpec((None, D, EXPERT_HIDDEN), lambda i, be, nb: (be[i], 0, 0)),
            pl.BlockSpec((None, D, EXPERT_HIDDEN), lambda i, be, nb: (be[i], 0, 0)),
            pl.BlockSpec((None, EXPERT_HIDDEN, D), lambda i, be, nb: (be[i], 0, 0)),
        ],
        out_specs=rows,
    )
    return pl.pallas_call(
        _expert_kernel,
        grid_spec=grid_spec,
        out_shape=jax.ShapeDtypeStruct(xs.shape, F32),
        compiler_params=_cparams(("arbitrary",)),
        name="experts",
    )(block_e, n_used, xs, wg, wu, wd)


def _combine_kernel(d0_ref, d1_ref, n0_ref, n1_ref, h_ref, wts_ref, g2_ref, b2_ref, ys_ref, o_ref,
                    rows_ref, sems):
    i = pl.program_id(0)
    n = pl.num_programs(0)
    slot = i % 2

    def gather(dest_refs, buf):
        def issue(t, _):
            for c in range(EXPERT_TOPK):
                pltpu.make_async_copy(ys_ref.at[dest_refs[c][t]], rows_ref.at[buf, c, t],
                                      sems.at[buf]).start(priority=c)
            return 0

        lax.fori_loop(0, ROW_TM, issue, 0, unroll=ROW_UNROLL)

    @pl.when(i == 0)
    def _():
        gather((d0_ref, d1_ref), 0)

    @pl.when(i + 1 < n)
    def _():
        gather((n0_ref, n1_ref), 1 - slot)

    def drain(t, _):
        for c in range(EXPERT_TOPK):
            pltpu.make_async_copy(ys_ref.at[0], rows_ref.at[slot, c, t], sems.at[slot]).wait()
        return 0

    lax.fori_loop(0, ROW_TM, drain, 0, unroll=ROW_UNROLL)
    w = wts_ref[...]
    ffn = (w[:, 0:1] * rows_ref[slot, 0].reshape(ROW_TM, D_MODEL)
           + w[:, 1:2] * rows_ref[slot, 1].reshape(ROW_TM, D_MODEL))
    o_ref[...] = _layer_norm(ALPHA * h_ref[...] + ffn, g2_ref[...], b2_ref[...])


def _combine(dest, h1, wts_t, g2, b2, ys):
    T, D = h1.shape
    n = T // ROW_TM
    cur = pl.BlockSpec((ROW_TM,), lambda i: (i,), memory_space=pltpu.SMEM)
    nxt = pl.BlockSpec((ROW_TM,), lambda i: (jnp.minimum(i + 1, n - 1),), memory_space=pltpu.SMEM)
    return pl.pallas_call(
        _combine_kernel,
        grid=(n,),
        in_specs=[
            cur, cur, nxt, nxt,
            pl.BlockSpec((ROW_TM, D), lambda i: (i, 0)),
            pl.BlockSpec((ROW_TM, EXPERT_TOPK), lambda i: (i, 0)),
            pl.BlockSpec((1, D), lambda i: (0, 0)),
            pl.BlockSpec((1, D), lambda i: (0, 0)),
            pl.BlockSpec(memory_space=pl.ANY),
        ],
        out_specs=pl.BlockSpec((ROW_TM, D), lambda i: (i, 0)),
        scratch_shapes=[pltpu.VMEM((2, EXPERT_TOPK, ROW_TM, ROW_SUB, LANES), F32),
                        pltpu.SemaphoreType.DMA((2,))],
        out_shape=jax.ShapeDtypeStruct((T, D), F32),
        compiler_params=_cparams(("arbitrary",)),
        name="combine",
    )(dest[0], dest[1], dest[0], dest[1], h1, wts_t, g2, b2, ys)


def _attn_bias_tables(S):
    c = jnp.asarray((2.0 ** (-8.0 * np.arange(1, DIFF_HEADS + 1, dtype=np.float64) / DIFF_HEADS)) * LOG2E, F32)
    c_hi = c.astype(BF16)
    c_mid = (c - c_hi.astype(F32)).astype(BF16)
    c_lo = (c - c_hi.astype(F32) - c_mid.astype(F32)).astype(BF16)
    pieces = jnp.stack([c_hi, c_mid, c_lo], axis=1)
    qfeat = jnp.concatenate([pieces * LANES, pieces, jnp.zeros((DIFF_HEADS, LANES - 6), BF16)], axis=1)
    j = np.arange(S)
    pos = np.zeros((S, LANES), np.float32)
    pos[:, 0:3] = (j // LANES)[:, None]
    pos[:, 3:6] = (j % LANES)[:, None]
    kj = np.arange(ATT_TK, dtype=np.float64)[:, None]
    qi = np.tile(np.arange(ATT_TQ, dtype=np.float64), 2)[None, :]
    visible = (kj // CHUNK) <= (qi // CHUNK)
    after = 2.0 * np.minimum(qi - kj, 0.0)
    corr = jnp.where(visible[None], c[:, None, None] * jnp.asarray(after, F32)[None], MASKED)
    return jnp.asarray(pos, BF16), qfeat[:, :, None], corr.astype(F32)


def _layer(h, layer, w_in, b_in, sg_ln_g, sg_ln_b, sg_w, sg_b, w_branch_a, lam_q1, lam_k1, lam_q2, lam_k2,
           subln_g, w_branch_b, w_out, ln1_g, ln1_b, w_group, b_group, w_expert, b_expert,
           w_gate, w_up, w_down, ln2_g, ln2_b):
    B, S, D = h.shape
    T = B * S
    row = lambda a: a.reshape(1, -1)
    col = lambda a: a.reshape(-1, 1)

    wk = w_in[:, OFF_K:OFF_VAL].astype(BF16)
    wqt = w_in[:, OFF_Q:OFF_K].T.astype(BF16)
    wvt = w_in[:, OFF_VAL:OFF_GA].T.astype(BF16)
    k, qt, vt = _qkv_proj(h, wk, row(b_in[OFF_K:OFF_VAL]), wqt, col(b_in[OFF_Q:OFF_K]),
                          wvt, col(b_in[OFF_VAL:OFF_GA]))
    pos, qfeat, corr = _attn_bias_tables(S)
    lambda_init = 0.8 - 0.6 * math.exp(-0.3 * layer)
    lamv = jnp.stack([lam_q1, lam_k1, lam_q2, lam_k2]).astype(F32)
    attn = _diff_attention(qt, k, pos, vt, qfeat, corr, lamv, col(subln_g), lambda_init)

    sgb = jnp.repeat(sg_b.reshape(GMLP_GROUPS // 2, 2, GMLP_BLOCK), GMLP_GROUP_DIM, axis=1)
    sgb = jnp.transpose(sgb, (0, 2, 1))
    h1 = _mixer(h.reshape(T, D), attn.reshape(T, DIFF_WIDTH),
                w_in[:, OFF_U:OFF_Q].astype(BF16), row(b_in[OFF_U:OFF_Q]),
                w_in[:, OFF_GA:IN_WIDTH].astype(BF16), row(b_in[OFF_GA:IN_WIDTH]),
                row(sg_ln_g), row(sg_ln_b), sg_w, sgb,
                w_branch_a.astype(BF16), w_branch_b.astype(BF16), w_out.astype(BF16),
                row(ln1_g), row(ln1_b))

    wr = jnp.zeros((LANES, D), F32)
    wr = wr.at[0:N_GROUPS].set(w_group.T).at[8:8 + N_EXPERTS].set(w_expert.T)
    wr_hi = wr.astype(BF16)
    wr_lo = (wr - wr_hi.astype(F32)).astype(BF16)
    br = jnp.zeros((LANES, 1), F32)
    br = br.at[0:N_GROUPS, 0].set(b_group.astype(F32)).at[8:8 + N_EXPERTS, 0].set(b_expert.astype(F32))
    eid, wts = _router(h1, wr_hi, wr_lo, br)

    tri = jnp.asarray(np.triu(np.ones((ROUTE_TM, ROUTE_TM), np.float32), k=1), BF16)
    rank, counts = _ranks(eid, tri)

    counts = counts[:, 0]
    padded = (counts + EXPERT_BLOCK - 1) // EXPERT_BLOCK * EXPERT_BLOCK
    pad_end = jnp.cumsum(padded)
    pad_start = (pad_end - padded).astype(jnp.int32)
    P = T * EXPERT_TOPK + N_EXPERTS * EXPERT_BLOCK
    n_blocks = P // EXPERT_BLOCK
    block_start = jnp.arange(n_blocks, dtype=jnp.int32) * EXPERT_BLOCK
    block_e = jnp.minimum(jnp.sum(pad_end[None, :] <= block_start[:, None], axis=1), N_EXPERTS - 1).astype(jnp.int32)
    n_used = (pad_end[-1:] // EXPERT_BLOCK).astype(jnp.int32)

    hot = eid[:, :, None] == jnp.arange(N_EXPERTS, dtype=jnp.int32)
    dest = jnp.sum(jnp.where(hot, pad_start, 0), axis=-1) + rank
    xs = _dispatch(counts, pad_start, dest, h1, P)
    ys = _experts(block_e, n_used, xs, w_gate.astype(BF16), w_up.astype(BF16), w_down.astype(BF16))
    out = _combine(dest, h1, wts.T, row(ln2_g), row(ln2_b), ys)
    return out.reshape(B, S, D)


def kernel(x, w_in, b_in, sg_ln_g, sg_ln_b, sg_w, sg_b, w_branch_a, lam_q1, lam_k1, lam_q2, lam_k2, subln_g,
           w_branch_b, w_out, ln1_g, ln1_b, w_group, b_group, w_expert, b_expert, w_gate, w_up, w_down,
           ln2_g, ln2_b):
    h = x
    for layer in range(DEPTH):
        h = _layer(h, layer, w_in[layer], b_in[layer], sg_ln_g[layer], sg_ln_b[layer], sg_w[layer],
                   sg_b[layer], w_branch_a[layer], lam_q1[layer], lam_k1[layer], lam_q2[layer],
                   lam_k2[layer], subln_g[layer], w_branch_b[layer], w_out[layer], ln1_g[layer],
                   ln1_b[layer], w_group[layer], b_group[layer], w_expert[layer], b_expert[layer],
                   w_gate[layer], w_up[layer], w_down[layer], ln2_g[layer], ln2_b[layer])
    return h
```

```python
import functools
import math

import jax
import jax.numpy as jnp
import numpy as np
from jax import lax
from jax.experimental import pallas as pl
from jax.experimental.pallas import tpu as pltpu

D_MODEL = 1024
DEPTH = 1
CHUNK = 64
GMLP_GROUPS = 8
GMLP_GROUP_DIM = 64
GMLP_WIDTH = GMLP_GROUPS * GMLP_GROUP_DIM
GMLP_BLOCK = 128
DIFF_HEADS = 8
DIFF_QK_DIM = 64
DIFF_V_DIM = 2 * DIFF_QK_DIM
DIFF_QK_WIDTH = DIFF_HEADS * 2 * DIFF_QK_DIM
DIFF_WIDTH = DIFF_HEADS * DIFF_V_DIM
N_GROUPS = 4
EXPERTS_PER_GROUP = 8
N_EXPERTS = N_GROUPS * EXPERTS_PER_GROUP
EXPERT_TOPK = 2
EXPERT_HIDDEN = 512
DISPATCH_BLOCK = 256
ALPHA = (2.0 * DEPTH) ** 0.25
LN_EPS = 1e-5
RMS_EPS = 1e-5
OFF_U = 0
OFF_V = OFF_U + GMLP_WIDTH
OFF_Q = OFF_V + GMLP_WIDTH
OFF_K = OFF_Q + DIFF_QK_WIDTH
OFF_VAL = OFF_K + DIFF_QK_WIDTH
OFF_GA = OFF_VAL + DIFF_WIDTH
OFF_GB = OFF_GA + D_MODEL
IN_WIDTH = OFF_GB + D_MODEL

LANES = 128
HEAD_W = 2 * DIFF_QK_DIM
MASKED = -1e30
VMEM_LIMIT = 56 * 1024 * 1024

PROJ_TM = 512
ATT_T = 256
ATT_TK = 2 * ATT_T
ATT_TQ = ATT_TK
ATT_HB = 2
LOG2E = math.log2(math.e)
VT_ROWS = HEAD_W + 16
MIX_TM = 512
ROUTE_TM = 512
ROW_TM = 256
EXPERT_BLOCK = 2 * DISPATCH_BLOCK
ROW_UNROLL = 8

BF16 = jnp.bfloat16
F32 = jnp.float32


def _cparams(sem):
    return pltpu.CompilerParams(dimension_semantics=sem, vmem_limit_bytes=VMEM_LIMIT)


def _dot(a, b):
    return jnp.dot(a, b, preferred_element_type=F32)


def _dot_nt(a, b):
    return lax.dot_general(a, b, (((1,), (1,)), ((), ())), preferred_element_type=F32)


def _gelu(x):
    return 0.5 * x * (1.0 + lax.erf(x * (2.0 ** -0.5)))


def _sigmoid(x):
    return 1.0 / (1.0 + jnp.exp(-x))


def _layer_norm(x, g, b):
    mu = jnp.mean(x, axis=-1, keepdims=True)
    xc = x - mu
    var = jnp.mean(xc * xc, axis=-1, keepdims=True)
    return xc * lax.rsqrt(var + LN_EPS) * g + b


def _qkv_kernel(x_ref, wk_ref, bk_ref, wqt_ref, bq_ref, wvt_ref, bv_ref, k_ref, qt_ref, vt_ref):
    xb = x_ref[...].astype(BF16)
    k_ref[...] = (_dot(xb, wk_ref[...]) + bk_ref[...]).astype(BF16)
    qt = (_dot_nt(wqt_ref[...], xb) + bq_ref[...]) * (DIFF_QK_DIM ** -0.5 * LOG2E)
    qt_ref[...] = qt.astype(BF16)
    vt = (_dot_nt(wvt_ref[...], xb) + bv_ref[...]).astype(BF16)
    ones_row = (lax.broadcasted_iota(jnp.int32, (VT_ROWS - HEAD_W, ATT_TK), 0) == 0).astype(BF16)
    for hh in range(DIFF_HEADS):
        for j in range(PROJ_TM // ATT_TK):
            vt_ref[hh, j, :HEAD_W] = vt[hh * HEAD_W:(hh + 1) * HEAD_W, j * ATT_TK:(j + 1) * ATT_TK]
            vt_ref[hh, j, HEAD_W:] = ones_row


def _qkv_proj(x, wk, bk, wqt, bq, wvt, bv):
    B, S, D = x.shape
    nk = S // ATT_TK
    jt = PROJ_TM // ATT_TK
    const2 = lambda b, i: (0, 0)
    return pl.pallas_call(
        _qkv_kernel,
        grid=(B, S // PROJ_TM),
        in_specs=[
            pl.BlockSpec((None, PROJ_TM, D), lambda b, i: (b, i, 0)),
            pl.BlockSpec((D, DIFF_QK_WIDTH), const2),
            pl.BlockSpec((1, DIFF_QK_WIDTH), const2),
            pl.BlockSpec((DIFF_QK_WIDTH, D), const2),
            pl.BlockSpec((DIFF_QK_WIDTH, 1), const2),
            pl.BlockSpec((DIFF_WIDTH, D), const2),
            pl.BlockSpec((DIFF_WIDTH, 1), const2),
        ],
        out_specs=[
            pl.BlockSpec((None, PROJ_TM, DIFF_QK_WIDTH), lambda b, i: (b, i, 0)),
            pl.BlockSpec((None, DIFF_QK_WIDTH, PROJ_TM), lambda b, i: (b, 0, i)),
            pl.BlockSpec((None, DIFF_HEADS, jt, VT_ROWS, ATT_TK), lambda b, i: (b, 0, i, 0, 0)),
        ],
        out_shape=[
            jax.ShapeDtypeStruct((B, S, DIFF_QK_WIDTH), BF16),
            jax.ShapeDtypeStruct((B, DIFF_QK_WIDTH, S), BF16),
            jax.ShapeDtypeStruct((B, DIFF_HEADS, nk, VT_ROWS, ATT_TK), BF16),
        ],
        compiler_params=_cparams(("arbitrary", "arbitrary")),
        name="qkv_proj",
    )(x, wk, bk, wqt, bq, wvt, bv)


def _attn_kernel(qt_ref, k_ref, pos_ref, vt_ref, qf_ref, corr_ref, lamv_ref, g_ref, o_ref,
                 q12_ref, sa_ref, sb_ref, ta_ref, tb_ref, m_ref, acc_ref, *, lambda_init):
    qi = pl.program_id(2)
    T = ATT_T
    TQ = ATT_TQ
    n = qi + 1

    row = lax.broadcasted_iota(jnp.int32, (HEAD_W, TQ), 0)
    zero = jnp.zeros((HEAD_W, TQ), BF16)
    for e in range(ATT_HB):
        q = qt_ref[e * HEAD_W:(e + 1) * HEAD_W, :]
        q12_ref[e, :HEAD_W, :TQ] = jnp.where(row < DIFF_QK_DIM, q, zero)
        q12_ref[e, :HEAD_W, TQ:] = jnp.where(row >= DIFF_QK_DIM, q, zero)
        q12_ref[e, HEAD_W:, :] = jnp.broadcast_to(qf_ref[e], (LANES, 2 * TQ))
    acc_ref[...] = jnp.zeros_like(acc_ref)
    m_ref[...] = jnp.full(m_ref.shape, MASKED, F32)

    def scores(g, s_ref, t_ref, diagonal):
        rows = pl.ds(pl.multiple_of(g * 2 * T, 2 * T), 2 * T)
        for e in range(ATT_HB):
            kx = jnp.concatenate([k_ref[rows, e * HEAD_W:(e + 1) * HEAD_W], pos_ref[rows, :]], axis=1)
            t = _dot(kx, q12_ref[e])
            if diagonal:
                t = t + corr_ref[e]
            s_ref[e] = t
            t_ref[e] = jnp.max(t, axis=0, keepdims=True)

    def accumulate(g, s_ref, t_ref):
        for e in range(ATT_HB):
            m = m_ref[e]
            m_new = jnp.maximum(m, t_ref[e])
            m_ref[e] = m_new
            p = jnp.exp2(s_ref[e] - m_new).astype(BF16)
            acc_ref[e] = jnp.exp2(m - m_new) * acc_ref[e] + _dot(vt_ref[e, g], p)

    def finish():
        lv = lamv_ref[...]
        lam = (jnp.exp(jnp.sum(lv[0:1] * lv[1:2], axis=1, keepdims=True))
               - jnp.exp(jnp.sum(lv[2:3] * lv[3:4], axis=1, keepdims=True)) + lambda_init)
        for e in range(ATT_HB):
            o12 = acc_ref[e, :HEAD_W] * (1.0 / acc_ref[e, HEAD_W:HEAD_W + 1])
            o = o12[:, :TQ] - lam * o12[:, TQ:]
            o = o * lax.rsqrt(jnp.mean(o * o, axis=0, keepdims=True) + RMS_EPS)
            o = o * g_ref[...] * (1.0 - lambda_init)
            o_ref[:, e * HEAD_W:(e + 1) * HEAD_W] = o.T.astype(BF16)

    @pl.when(n == 1)
    def _():
        scores(0, sa_ref, ta_ref, True)
        accumulate(0, sa_ref, ta_ref)
        finish()

    @pl.when(n > 1)
    def _():
        scores(0, sa_ref, ta_ref, False)

    def pair(i, _):
        g = 2 * i
        scores(g + 1, sb_ref, tb_ref, False)
        accumulate(g, sa_ref, ta_ref)
        scores(g + 2, sa_ref, ta_ref, False)
        accumulate(g + 1, sb_ref, tb_ref)
        return 0

    lax.fori_loop(0, jnp.maximum(n - 2, 0) // 2, pair, 0)

    @pl.when((n > 1) & (n % 2 == 0))
    def _():
        scores(n - 1, sb_ref, tb_ref, True)
        accumulate(n - 2, sa_ref, ta_ref)
        accumulate(n - 1, sb_ref, tb_ref)
        finish()

    @pl.when((n > 1) & (n % 2 == 1))
    def _():
        scores(n - 2, sb_ref, tb_ref, False)
        accumulate(n - 3, sa_ref, ta_ref)
        scores(n - 1, sa_ref, ta_ref, True)
        accumulate(n - 2, sb_ref, tb_ref)
        accumulate(n - 1, sa_ref, ta_ref)
        finish()


def _diff_attention(qt, k, pos, vt, qfeat, corr, lamv, gcol, lambda_init):
    B, S, _ = k.shape
    nq = S // ATT_TQ
    gw = ATT_HB * HEAD_W
    lanes = 2 * ATT_TQ
    return pl.pallas_call(
        functools.partial(_attn_kernel, lambda_init=lambda_init),
        grid=(B, DIFF_HEADS // ATT_HB, nq),
        in_specs=[
            pl.BlockSpec((None, gw, ATT_TQ), lambda b, h, i: (b, h, i)),
            pl.BlockSpec((None, S, gw), lambda b, h, i: (b, 0, h)),
            pl.BlockSpec((S, LANES), lambda b, h, i: (0, 0)),
            pl.BlockSpec((None, ATT_HB, S // ATT_TK, VT_ROWS, ATT_TK), lambda b, h, i: (b, h, 0, 0, 0)),
            pl.BlockSpec((ATT_HB, LANES, 1), lambda b, h, i: (h, 0, 0)),
            pl.BlockSpec((ATT_HB, ATT_TK, lanes), lambda b, h, i: (h, 0, 0)),
            pl.BlockSpec((4, DIFF_QK_DIM), lambda b, h, i: (0, 0)),
            pl.BlockSpec((HEAD_W, 1), lambda b, h, i: (0, 0)),
        ],
        out_specs=pl.BlockSpec((None, ATT_TQ, gw), lambda b, h, i: (b, i, h)),
        scratch_shapes=[
            pltpu.VMEM((ATT_HB, HEAD_W + LANES, lanes), BF16),
            pltpu.VMEM((ATT_HB, ATT_TK, lanes), F32),
            pltpu.VMEM((ATT_HB, ATT_TK, lanes), F32),
            pltpu.VMEM((ATT_HB, 1, lanes), F32),
            pltpu.VMEM((ATT_HB, 1, lanes), F32),
            pltpu.VMEM((ATT_HB, 1, lanes), F32),
            pltpu.VMEM((ATT_HB, VT_ROWS, lanes), F32),
        ],
        out_shape=jax.ShapeDtypeStruct((B, S, DIFF_WIDTH), BF16),
        compiler_params=_cparams(("arbitrary", "arbitrary", "arbitrary")),
        name="diff_attn",
    )(qt, k, pos, vt, qfeat, corr, lamv, gcol)


def _mixer_kernel(x_ref, attn_ref, wuv_ref, buv_ref, wg_ref, bg_ref, lng_ref, lnb_ref, sgw_ref, sgb_ref,
                  wa_ref, wb_ref, wo_ref, g1_ref, b1_ref, h_ref):
    x = x_ref[...]
    xb = x.astype(BF16)
    uv = _gelu(_dot(xb, wuv_ref[...]) + buv_ref[...])
    u = uv[:, :GMLP_WIDTH]
    v = _layer_norm(uv[:, GMLP_WIDTH:], lng_ref[...], lnb_ref[...]).astype(BF16)

    ti = lax.broadcasted_iota(jnp.int32, (GMLP_BLOCK, GMLP_BLOCK), 0) // CHUNK
    si = lax.broadcasted_iota(jnp.int32, (GMLP_BLOCK, GMLP_BLOCK), 1) // CHUNK
    causal = ti >= si
    w = [jnp.where(causal, sgw_ref[g], 0.0).astype(BF16) for g in range(GMLP_GROUPS)]
    lane = lax.broadcasted_iota(jnp.int32, (GMLP_BLOCK, LANES), 1)
    lo = lane < GMLP_GROUP_DIM
    vzero = jnp.zeros((GMLP_BLOCK, LANES), BF16)

    blocks = []
    for n in range(MIX_TM // GMLP_BLOCK):
        slabs = []
        for p in range(GMLP_WIDTH // LANES):
            vs = v[n * GMLP_BLOCK:(n + 1) * GMLP_BLOCK, p * LANES:(p + 1) * LANES]
            mixed = (_dot(w[2 * p], jnp.where(lo, vs, vzero))
                     + _dot(w[2 * p + 1], jnp.where(lo, vzero, vs)) + sgb_ref[p])
            slabs.append(mixed)
        blocks.append(jnp.concatenate(slabs, axis=1))
    mixed = jnp.concatenate(blocks, axis=0)

    y_a = _dot((u * mixed).astype(BF16), wa_ref[...])
    y_b = _dot(attn_ref[...], wb_ref[...])
    gates = _sigmoid(_dot(xb, wg_ref[...]) + bg_ref[...])
    z = gates[:, :D_MODEL] * y_a + gates[:, D_MODEL:] * y_b
    mix = _dot(z.astype(BF16), wo_ref[...])
    h_ref[...] = _layer_norm(ALPHA * x + mix, g1_ref[...], b1_ref[...])


def _mixer(x2, attn2, wuv, buv, wg, bg, lng, lnb, sgw, sgb, wa, wb, wo, g1, b1):
    T, D = x2.shape
    c2 = lambda i: (0, 0)
    c3 = lambda i: (0, 0, 0)
    full = lambda a: pl.BlockSpec(a.shape, c2 if a.ndim == 2 else c3)
    return pl.pallas_call(
        _mixer_kernel,
        grid=(T // MIX_TM,),
        in_specs=[pl.BlockSpec((MIX_TM, D), lambda i: (i, 0)),
                  pl.BlockSpec((MIX_TM, DIFF_WIDTH), lambda i: (i, 0))]
                 + [full(a) for a in (wuv, buv, wg, bg, lng, lnb, sgw, sgb, wa, wb, wo, g1, b1)],
        out_specs=pl.BlockSpec((MIX_TM, D), lambda i: (i, 0)),
        out_shape=jax.ShapeDtypeStruct((T, D), F32),
        compiler_params=_cparams(("arbitrary",)),
        name="mixer",
    )(x2, attn2, wuv, buv, wg, bg, lng, lnb, sgw, sgb, wa, wb, wo, g1, b1)


def _router_kernel(h_ref, wr_hi_ref, wr_lo_ref, br_ref, eid_ref, wts_ref):
    hf = h_ref[...]
    hi = hf.astype(BF16)
    lo = (hf - hi.astype(F32)).astype(BF16)
    logits = (_dot_nt(wr_hi_ref[...], hi) + _dot_nt(wr_lo_ref[...], hi)
              + _dot_nt(wr_hi_ref[...], lo)) + br_ref[...]
    gidx = lax.broadcasted_iota(jnp.int32, (8, ROUTE_TM), 0)
    gl = jnp.where(gidx < N_GROUPS, logits[0:8], -jnp.inf)
    gmax = jnp.max(gl, axis=0, keepdims=True)
    g_sel = jnp.min(jnp.where(gl == gmax, gidx, N_GROUPS), axis=0, keepdims=True)
    g_w = 1.0 / jnp.sum(jnp.exp(gl - gmax), axis=0, keepdims=True)

    el = logits[8:8 + EXPERTS_PER_GROUP]
    for g in range(1, N_GROUPS):
        el = jnp.where(g_sel == g, logits[8 + g * EXPERTS_PER_GROUP:8 + (g + 1) * EXPERTS_PER_GROUP], el)
    eidx = lax.broadcasted_iota(jnp.int32, el.shape, 0)
    v1 = jnp.max(el, axis=0, keepdims=True)
    i1 = jnp.min(jnp.where(el == v1, eidx, EXPERTS_PER_GROUP), axis=0, keepdims=True)
    rest = jnp.where(eidx == i1, -jnp.inf, el)
    v2 = jnp.max(rest, axis=0, keepdims=True)
    i2 = jnp.min(jnp.where(rest == v2, eidx, EXPERTS_PER_GROUP), axis=0, keepdims=True)
    e21 = jnp.exp(v2 - v1)
    w1 = g_w / (1.0 + e21)
    w2 = g_w * e21 / (1.0 + e21)
    base = g_sel * EXPERTS_PER_GROUP
    eid_ref[0:1, :] = base + i1
    eid_ref[1:2, :] = base + i2
    wts_ref[0:1, :] = w1
    wts_ref[1:2, :] = w2


def _router(h1, wr_hi, wr_lo, br):
    T, D = h1.shape
    c2 = lambda i: (0, 0)
    return pl.pallas_call(
        _router_kernel,
        grid=(T // ROUTE_TM,),
        in_specs=[pl.BlockSpec((ROUTE_TM, D), lambda i: (i, 0)),
                  pl.BlockSpec(wr_hi.shape, c2), pl.BlockSpec(wr_lo.shape, c2), pl.BlockSpec(br.shape, c2)],
        out_specs=[pl.BlockSpec((EXPERT_TOPK, ROUTE_TM), lambda i: (0, i)),
                   pl.BlockSpec((EXPERT_TOPK, ROUTE_TM), lambda i: (0, i))],
        out_shape=[jax.ShapeDtypeStruct((EXPERT_TOPK, T), jnp.int32),
                   jax.ShapeDtypeStruct((EXPERT_TOPK, T), F32)],
        compiler_params=_cparams(("arbitrary",)),
        name="router",
    )(h1, wr_hi, wr_lo, br)


def _rank_kernel(eid_ref, tri_ref, rank_ref, cnt_ref, run_ref):
    @pl.when(pl.program_id(0) == 0)
    def _():
        run_ref[...] = jnp.zeros_like(run_ref)

    eid = eid_ref[...]
    eiota = lax.broadcasted_iota(jnp.int32, (N_EXPERTS, ROUTE_TM), 0)
    run = run_ref[...]
    ranks = []
    for c in range(EXPERT_TOPK):
        hot = eiota == eid[c:c + 1]
        hotf = hot.astype(F32)
        before = _dot(hotf.astype(BF16), tri_ref[...])
        ranks.append(jnp.sum(jnp.where(hot, run + before, 0.0), axis=0, keepdims=True))
        run = run + jnp.sum(hotf, axis=1, keepdims=True)
    run_ref[...] = run
    for c in range(EXPERT_TOPK):
        rank_ref[c:c + 1, :] = ranks[c].astype(jnp.int32)
    cnt_ref[...] = run.astype(jnp.int32)


def _ranks(eid, tri):
    _, T = eid.shape
    return pl.pallas_call(
        _rank_kernel,
        grid=(T // ROUTE_TM,),
        in_specs=[pl.BlockSpec((EXPERT_TOPK, ROUTE_TM), lambda i: (0, i)),
                  pl.BlockSpec(tri.shape, lambda i: (0, 0))],
        out_specs=[pl.BlockSpec((EXPERT_TOPK, ROUTE_TM), lambda i: (0, i)),
                   pl.BlockSpec((N_EXPERTS, 1), lambda i: (0, 0))],
        out_shape=[jax.ShapeDtypeStruct((EXPERT_TOPK, T), jnp.int32),
                   jax.ShapeDtypeStruct((N_EXPERTS, 1), jnp.int32)],
        scratch_shapes=[pltpu.VMEM((N_EXPERTS, 1), F32)],
        compiler_params=_cparams(("arbitrary",)),
        name="rank",
    )(eid, tri)


ROW_SUB = D_MODEL // LANES


def _dispatch_kernel(cnt_ref, pad_ref, d0_ref, d1_ref, h_ref, xs_ref, rows_ref, zero_ref, sems, zsem):
    i = pl.program_id(0)
    slot = i % 2
    dest_refs = (d0_ref, d1_ref)

    @pl.when(i == 0)
    def _():
        zero_ref[...] = jnp.zeros_like(zero_ref)

        def zero_copy(r):
            return pltpu.make_async_copy(zero_ref, xs_ref.at[r], zsem)

        def start(r, _):
            zero_copy(r).start()
            return 0

        def per_expert(e, carry):
            n, _ = carry
            lo = pad_ref[e] + cnt_ref[e]
            hi = pad_ref[e] + (cnt_ref[e] + EXPERT_BLOCK - 1) // EXPERT_BLOCK * EXPERT_BLOCK
            lax.fori_loop(lo, hi, start, 0)
            return n + (hi - lo), hi

        n_pad, used = lax.fori_loop(0, N_EXPERTS, per_expert, (0, 0))
        lax.fori_loop(used, xs_ref.shape[0], start, 0)
        n_pad = n_pad + (xs_ref.shape[0] - used)

        def wait(_, c):
            zero_copy(0).wait()
            return c

        lax.fori_loop(0, n_pad, wait, 0)

    rows_ref[slot] = h_ref[...].reshape(ROW_TM, ROW_SUB, LANES)

    def issue(t, _):
        for c in range(EXPERT_TOPK):
            pltpu.make_async_copy(rows_ref.at[slot, t], xs_ref.at[dest_refs[c][t]],
                                  sems.at[slot]).start(priority=c)
        return 0

    lax.fori_loop(0, ROW_TM, issue, 0, unroll=ROW_UNROLL)

    def drain(buf):
        def wait(t, _):
            for c in range(EXPERT_TOPK):
                pltpu.make_async_copy(rows_ref.at[buf, t], xs_ref.at[0], sems.at[buf]).wait()
            return 0

        lax.fori_loop(0, ROW_TM, wait, 0, unroll=ROW_UNROLL)

    @pl.when(i > 0)
    def _():
        drain(1 - slot)

    @pl.when(i == pl.num_programs(0) - 1)
    def _():
        drain(slot)


def _dispatch(counts, pad_start, dest, h1, n_rows):
    T, D = h1.shape
    grid_spec = pltpu.PrefetchScalarGridSpec(
        num_scalar_prefetch=2,
        grid=(T // ROW_TM,),
        in_specs=[
            pl.BlockSpec((ROW_TM,), lambda i, c, p: (i,), memory_space=pltpu.SMEM),
            pl.BlockSpec((ROW_TM,), lambda i, c, p: (i,), memory_space=pltpu.SMEM),
            pl.BlockSpec((ROW_TM, D), lambda i, c, p: (i, 0)),
        ],
        out_specs=pl.BlockSpec(memory_space=pl.ANY),
        scratch_shapes=[pltpu.VMEM((2, ROW_TM, ROW_SUB, LANES), F32), pltpu.VMEM((ROW_SUB, LANES), F32),
                        pltpu.SemaphoreType.DMA((2,)), pltpu.SemaphoreType.DMA(())],
    )
    return pl.pallas_call(
        _dispatch_kernel,
        grid_spec=grid_spec,
        out_shape=jax.ShapeDtypeStruct((n_rows, ROW_SUB, LANES), F32),
        compiler_params=_cparams(("arbitrary",)),
        name="dispatch",
    )(counts, pad_start, dest[0], dest[1], h1)


def _expert_kernel(be_ref, nb_ref, xs_ref, wg_ref, wu_ref, wd_ref, o_ref):
    i = pl.program_id(0)

    @pl.when(i < nb_ref[0])
    def _():
        xb = xs_ref[...].reshape(EXPERT_BLOCK, D_MODEL).astype(BF16)
        gate = _dot(xb, wg_ref[...])
        up = _dot(xb, wu_ref[...])
        hb = (gate * _sigmoid(gate) * up).astype(BF16)
        o_ref[...] = _dot(hb, wd_ref[...]).reshape(EXPERT_BLOCK, ROW_SUB, LANES)

    @pl.when(i >= nb_ref[0])
    def _():
        o_ref[...] = jnp.zeros_like(o_ref)


def _experts(block_e, n_used, xs, wg, wu, wd):
    P = xs.shape[0]
    D = D_MODEL
    rows = pl.BlockSpec((EXPERT_BLOCK, ROW_SUB, LANES), lambda i, be, nb: (i, 0, 0))
    grid_spec = pltpu.PrefetchScalarGridSpec(
        num_scalar_prefetch=2,
        grid=(P // EXPERT_BLOCK,),
        in_specs=[
            rows,
            pl.BlockSpec((None, D, EXPERT_HIDDEN), lambda i, be, nb: (be[i], 0, 0)),
            pl.BlockSpec((None, D, EXPERT_HIDDEN), lambda i, be, nb: (be[i], 0, 0)),
            pl.BlockSpec((None, EXPERT_HIDDEN, D), lambda i, be, nb: (be[i], 0, 0)),
        ],
        out_specs=rows,
    )
    return pl.pallas_call(
        _expert_kernel,
        grid_spec=grid_spec,
        out_shape=jax.ShapeDtypeStruct(xs.shape, F32),
        compiler_params=_cparams(("arbitrary",)),
        name="experts",
    )(block_e, n_used, xs, wg, wu, wd)


def _combine_kernel(d0_ref, d1_ref, n0_ref, n1_ref, h_ref, wts_ref, g2_ref, b2_ref, ys_ref, o_ref,
                    rows_ref, sems):
    i = pl.program_id(0)
    n = pl.num_programs(0)
    slot = i % 2

    def gather(dest_refs, buf):
        def issue(t, _):
            for c in range(EXPERT_TOPK):
                pltpu.make_async_copy(ys_ref.at[dest_refs[c][t]], rows_ref.at[buf, c, t],
                                      sems.at[buf]).start(priority=c)
            return 0

        lax.fori_loop(0, ROW_TM, issue, 0, unroll=ROW_UNROLL)

    @pl.when(i == 0)
    def _():
        gather((d0_ref, d1_ref), 0)

    @pl.when(i + 1 < n)
    def _():
        gather((n0_ref, n1_ref), 1 - slot)

    def drain(t, _):
        for c in range(EXPERT_TOPK):
            pltpu.make_async_copy(ys_ref.at[0], rows_ref.at[slot, c, t], sems.at[slot]).wait()
        return 0

    lax.fori_loop(0, ROW_TM, drain, 0, unroll=ROW_UNROLL)
    w = wts_ref[...]
    ffn = (w[:, 0:1] * rows_ref[slot, 0].reshape(ROW_TM, D_MODEL)
           + w[:, 1:2] * rows_ref[slot, 1].reshape(ROW_TM, D_MODEL))
    o_ref[...] = _layer_norm(ALPHA * h_ref[...] + ffn, g2_ref[...], b2_ref[...])


def _combine(dest, h1, wts_t, g2, b2, ys):
    T, D = h1.shape
    n = T // ROW_TM
    cur = pl.BlockSpec((ROW_TM,), lambda i: (i,), memory_space=pltpu.SMEM)
    nxt = pl.BlockSpec((ROW_TM,), lambda i: (jnp.minimum(i + 1, n - 1),), memory_space=pltpu.SMEM)
    return pl.pallas_call(
        _combine_kernel,
        grid=(n,),
        in_specs=[
            cur, cur, nxt, nxt,
            pl.BlockSpec((ROW_TM, D), lambda i: (i, 0)),
            pl.BlockSpec((ROW_TM, EXPERT_TOPK), lambda i: (i, 0)),
            pl.BlockSpec((1, D), lambda i: (0, 0)),
            pl.BlockSpec((1, D), lambda i: (0, 0)),
            pl.BlockSpec(memory_space=pl.ANY),
        ],
        out_specs=pl.BlockSpec((ROW_TM, D), lambda i: (i, 0)),
        scratch_shapes=[pltpu.VMEM((2, EXPERT_TOPK, ROW_TM, ROW_SUB, LANES), F32),
                        pltpu.SemaphoreType.DMA((2,))],
        out_shape=jax.ShapeDtypeStruct((T, D), F32),
        compiler_params=_cparams(("arbitrary",)),
        name="combine",
    )(dest[0], dest[1], dest[0], dest[1], h1, wts_t, g2, b2, ys)


def _attn_bias_tables(S):
    c = jnp.asarray((2.0 ** (-8.0 * np.arange(1, DIFF_HEADS + 1, dtype=np.float64) / DIFF_HEADS)) * LOG2E, F32)
    c_hi = c.astype(BF16)
    c_mid = (c - c_hi.astype(F32)).astype(BF16)
    c_lo = (c - c_hi.astype(F32) - c_mid.astype(F32)).astype(BF16)
    pieces = jnp.stack([c_hi, c_mid, c_lo], axis=1)
    qfeat = jnp.concatenate([pieces * LANES, pieces, jnp.zeros((DIFF_HEADS, LANES - 6), BF16)], axis=1)
    j = np.arange(S)
    pos = np.zeros((S, LANES), np.float32)
    pos[:, 0:3] = (j // LANES)[:, None]
    pos[:, 3:6] = (j % LANES)[:, None]
    kj = np.arange(ATT_TK, dtype=np.float64)[:, None]
    qi = np.tile(np.arange(ATT_TQ, dtype=np.float64), 2)[None, :]
    visible = (kj // CHUNK) <= (qi // CHUNK)
    after = 2.0 * np.minimum(qi - kj, 0.0)
    corr = jnp.where(visible[None], c[:, None, None] * jnp.asarray(after, F32)[None], MASKED)
    return jnp.asarray(pos, BF16), qfeat[:, :, None], corr.astype(F32)


def _layer(h, layer, w_in, b_in, sg_ln_g, sg_ln_b, sg_w, sg_b, w_branch_a, lam_q1, lam_k1, lam_q2, lam_k2,
           subln_g, w_branch_b, w_out, ln1_g, ln1_b, w_group, b_group, w_expert, b_expert,
           w_gate, w_up, w_down, ln2_g, ln2_b):
    B, S, D = h.shape
    T = B * S
    row = lambda a: a.reshape(1, -1)
    col = lambda a: a.reshape(-1, 1)

    wk = w_in[:, OFF_K:OFF_VAL].astype(BF16)
    wqt = w_in[:, OFF_Q:OFF_K].T.astype(BF16)
    wvt = w_in[:, OFF_VAL:OFF_GA].T.astype(BF16)
    k, qt, vt = _qkv_proj(h, wk, row(b_in[OFF_K:OFF_VAL]), wqt, col(b_in[OFF_Q:OFF_K]),
                          wvt, col(b_in[OFF_VAL:OFF_GA]))
    pos, qfeat, corr = _attn_bias_tables(S)
    lambda_init = 0.8 - 0.6 * math.exp(-0.3 * layer)
    lamv = jnp.stack([lam_q1, lam_k1, lam_q2, lam_k2]).astype(F32)
    attn = _diff_attention(qt, k, pos, vt, qfeat, corr, lamv, col(subln_g), lambda_init)

    sgb = jnp.repeat(sg_b.reshape(GMLP_GROUPS // 2, 2, GMLP_BLOCK), GMLP_GROUP_DIM, axis=1)
    sgb = jnp.transpose(sgb, (0, 2, 1))
    h1 = _mixer(h.reshape(T, D), attn.reshape(T, DIFF_WIDTH),
                w_in[:, OFF_U:OFF_Q].astype(BF16), row(b_in[OFF_U:OFF_Q]),
                w_in[:, OFF_GA:IN_WIDTH].astype(BF16), row(b_in[OFF_GA:IN_WIDTH]),
                row(sg_ln_g), row(sg_ln_b), sg_w, sgb,
                w_branch_a.astype(BF16), w_branch_b.astype(BF16), w_out.astype(BF16),
                row(ln1_g), row(ln1_b))

    wr = jnp.zeros((LANES, D), F32)
    wr = wr.at[0:N_GROUPS].set(w_group.T).at[8:8 + N_EXPERTS].set(w_expert.T)
    wr_hi = wr.astype(BF16)
    wr_lo = (wr - wr_hi.astype(F32)).astype(BF16)
    br = jnp.zeros((LANES, 1), F32)
    br = br.at[0:N_GROUPS, 0].set(b_group.astype(F32)).at[8:8 + N_EXPERTS, 0].set(b_expert.astype(F32))
    eid, wts = _router(h1, wr_hi, wr_lo, br)

    tri = jnp.asarray(np.triu(np.ones((ROUTE_TM, ROUTE_TM), np.float32), k=1), BF16)
    rank, counts = _ranks(eid, tri)

    counts = counts[:, 0]
    padded = (counts + EXPERT_BLOCK - 1) // EXPERT_BLOCK * EXPERT_BLOCK
    pad_end = jnp.cumsum(padded)
    pad_start = (pad_end - padded).astype(jnp.int32)
    P = T * EXPERT_TOPK + N_EXPERTS * EXPERT_BLOCK
    n_blocks = P // EXPERT_BLOCK
    block_start = jnp.arange(n_blocks, dtype=jnp.int32) * EXPERT_BLOCK
    block_e = jnp.minimum(jnp.sum(pad_end[None, :] <= block_start[:, None], axis=1), N_EXPERTS - 1).astype(jnp.int32)
    n_used = (pad_end[-1:] // EXPERT_BLOCK).astype(jnp.int32)

    hot = eid[:, :, None] == jnp.arange(N_EXPERTS, dtype=jnp.int32)
    dest = jnp.sum(jnp.where(hot, pad_start, 0), axis=-1) + rank
    xs = _dispatch(counts, pad_start, dest, h1, P)
    ys = _experts(block_e, n_used, xs, w_gate.astype(BF16), w_up.astype(BF16), w_down.astype(BF16))
    out = _combine(dest, h1, wts.T, row(ln2_g), row(ln2_b), ys)
    return out.reshape(B, S, D)


def kernel(x, w_in, b_in, sg_ln_g, sg_ln_b, sg_w, sg_b, w_branch_a, lam_q1, lam_k1, lam_q2, lam_k2, subln_g,
           w_branch_b, w_out, ln1_g, ln1_b, w_group, b_group, w_expert, b_expert, w_gate, w_up, w_down,
           ln2_g, ln2_b):
    h = x
    for layer in range(DEPTH):
        h = _layer(h, layer, w_in[layer], b_in[layer], sg_ln_g[layer], sg_ln_b[layer], sg_w[layer],
                   sg_b[layer], w_branch_a[layer], lam_q1[layer], lam_k1[layer], lam_q2[layer],
                   lam_k2[layer], subln_g[layer], w_branch_b[layer], w_out[layer], ln1_g[layer],
                   ln1_b[layer], w_group[layer], b_group[layer], w_expert[layer], b_expert[layer],
                   w_gate[layer], w_up[layer], w_down[layer], ln2_g[layer], ln2_b[layer])
    return h
```

```python
import functools
import math

import jax
import jax.numpy as jnp
import numpy as np
from jax import lax
from jax.experimental import pallas as pl
from jax.experimental.pallas import tpu as pltpu

D_MODEL = 1024
DEPTH = 1
CHUNK = 64
GMLP_GROUPS = 8
GMLP_GROUP_DIM = 64
GMLP_WIDTH = GMLP_GROUPS * GMLP_GROUP_DIM
GMLP_BLOCK = 128
DIFF_HEADS = 8
DIFF_QK_DIM = 64
DIFF_V_DIM = 2 * DIFF_QK_DIM
DIFF_QK_WIDTH = DIFF_HEADS * 2 * DIFF_QK_DIM
DIFF_WIDTH = DIFF_HEADS * DIFF_V_DIM
N_GROUPS = 4
EXPERTS_PER_GROUP = 8
N_EXPERTS = N_GROUPS * EXPERTS_PER_GROUP
EXPERT_TOPK = 2
EXPERT_HIDDEN = 512
DISPATCH_BLOCK = 256
ALPHA = (2.0 * DEPTH) ** 0.25
LN_EPS = 1e-5
RMS_EPS = 1e-5
OFF_U = 0
OFF_V = OFF_U + GMLP_WIDTH
OFF_Q = OFF_V + GMLP_WIDTH
OFF_K = OFF_Q + DIFF_QK_WIDTH
OFF_VAL = OFF_K + DIFF_QK_WIDTH
OFF_GA = OFF_VAL + DIFF_WIDTH
OFF_GB = OFF_GA + D_MODEL
IN_WIDTH = OFF_GB + D_MODEL

LANES = 128
HEAD_W = 2 * DIFF_QK_DIM
MASKED = -1e30
VMEM_LIMIT = 56 * 1024 * 1024

PROJ_TM = 512
ATT_T = 256
ATT_TK = 2 * ATT_T
ATT_TQ = ATT_TK
ATT_HB = 2
LOG2E = math.log2(math.e)
VT_ROWS = HEAD_W + 16
MIX_TM = 512
ROUTE_TM = 512
ROW_TM = 256
EXPERT_BLOCK = 2 * DISPATCH_BLOCK
ROW_UNROLL = 8

BF16 = jnp.bfloat16
F32 = jnp.float32


def _cparams(sem):
    return pltpu.CompilerParams(dimension_semantics=sem, vmem_limit_bytes=VMEM_LIMIT)


def _dot(a, b):
    return jnp.dot(a, b, preferred_element_type=F32)


def _dot_nt(a, b):
    return lax.dot_general(a, b, (((1,), (1,)), ((), ())), preferred_element_type=F32)


def _gelu(x):
    return 0.5 * x * (1.0 + lax.erf(x * (2.0 ** -0.5)))


def _sigmoid(x):
    return 1.0 / (1.0 + jnp.exp(-x))


def _layer_norm(x, g, b):
    mu = jnp.mean(x, axis=-1, keepdims=True)
    xc = x - mu
    var = jnp.mean(xc * xc, axis=-1, keepdims=True)
    return xc * lax.rsqrt(var + LN_EPS) * g + b


def _qkv_kernel(x_ref, wk_ref, bk_ref, wqt_ref, bq_ref, wvt_ref, bv_ref, k_ref, qt_ref, vt_ref):
    xb = x_ref[...].astype(BF16)
    k_ref[...] = (_dot(xb, wk_ref[...]) + bk_ref[...]).astype(BF16)
    qt = (_dot_nt(wqt_ref[...], xb) + bq_ref[...]) * (DIFF_QK_DIM ** -0.5 * LOG2E)
    qt_ref[...] = qt.astype(BF16)
    vt = (_dot_nt(wvt_ref[...], xb) + bv_ref[...]).astype(BF16)
    ones_row = (lax.broadcasted_iota(jnp.int32, (VT_ROWS - HEAD_W, ATT_TK), 0) == 0).astype(BF16)
    for hh in range(DIFF_HEADS):
        for j in range(PROJ_TM // ATT_TK):
            vt_ref[hh, j, :HEAD_W] = vt[hh * HEAD_W:(hh + 1) * HEAD_W, j * ATT_TK:(j + 1) * ATT_TK]
            vt_ref[hh, j, HEAD_W:] = ones_row


def _qkv_proj(x, wk, bk, wqt, bq, wvt, bv):
    B, S, D = x.shape
    nk = S // ATT_TK
    jt = PROJ_TM // ATT_TK
    const2 = lambda b, i: (0, 0)
    return pl.pallas_call(
        _qkv_kernel,
        grid=(B, S // PROJ_TM),
        in_specs=[
            pl.BlockSpec((None, PROJ_TM, D), lambda b, i: (b, i, 0)),
            pl.BlockSpec((D, DIFF_QK_WIDTH), const2),
            pl.BlockSpec((1, DIFF_QK_WIDTH), const2),
            pl.BlockSpec((DIFF_QK_WIDTH, D), const2),
            pl.BlockSpec((DIFF_QK_WIDTH, 1), const2),
            pl.BlockSpec((DIFF_WIDTH, D), const2),
            pl.BlockSpec((DIFF_WIDTH, 1), const2),
        ],
        out_specs=[
            pl.BlockSpec((None, PROJ_TM, DIFF_QK_WIDTH), lambda b, i: (b, i, 0)),
            pl.BlockSpec((None, DIFF_QK_WIDTH, PROJ_TM), lambda b, i: (b, 0, i)),
            pl.BlockSpec((None, DIFF_HEADS, jt, VT_ROWS, ATT_TK), lambda b, i: (b, 0, i, 0, 0)),
        ],
        out_shape=[
            jax.ShapeDtypeStruct((B, S, DIFF_QK_WIDTH), BF16),
            jax.ShapeDtypeStruct((B, DIFF_QK_WIDTH, S), BF16),
            jax.ShapeDtypeStruct((B, DIFF_HEADS, nk, VT_ROWS, ATT_TK), BF16),
        ],
        compiler_params=_cparams(("arbitrary", "arbitrary")),
        name="qkv_proj",
    )(x, wk, bk, wqt, bq, wvt, bv)


def _attn_kernel(qt_ref, k_ref, pos_ref, vt_ref, qf_ref, corr_ref, lamv_ref, g_ref, o_ref,
                 q12_ref, sa_ref, sb_ref, ta_ref, tb_ref, m_ref, acc_ref, *, lambda_init):
    qi = pl.program_id(2)
    T = ATT_T
    TQ = ATT_TQ
    n = qi + 1

    row = lax.broadcasted_iota(jnp.int32, (HEAD_W, TQ), 0)
    zero = jnp.zeros((HEAD_W, TQ), BF16)
    for e in range(ATT_HB):
        q = qt_ref[e * HEAD_W:(e + 1) * HEAD_W, :]
        q12_ref[e, :HEAD_W, :TQ] = jnp.where(row < DIFF_QK_DIM, q, zero)
        q12_ref[e, :HEAD_W, TQ:] = jnp.where(row >= DIFF_QK_DIM, q, zero)
        q12_ref[e, HEAD_W:, :] = jnp.broadcast_to(qf_ref[e], (LANES, 2 * TQ))
    acc_ref[...] = jnp.zeros_like(acc_ref)
    m_ref[...] = jnp.full(m_ref.shape, MASKED, F32)

    def scores(g, s_ref, t_ref, diagonal):
        rows = pl.ds(pl.multiple_of(g * 2 * T, 2 * T), 2 * T)
        for e in range(ATT_HB):
            kx = jnp.concatenate([k_ref[rows, e * HEAD_W:(e + 1) * HEAD_W], pos_ref[rows, :]], axis=1)
            t = _dot(kx, q12_ref[e])
            if diagonal:
                t = t + corr_ref[e]
            s_ref[e] = t
            t_ref[e] = jnp.max(t, axis=0, keepdims=True)

    def accumulate(g, s_ref, t_ref):
        for e in range(ATT_HB):
            m = m_ref[e]
            m_new = jnp.maximum(m, t_ref[e])
            m_ref[e] = m_new
            p = jnp.exp2(s_ref[e] - m_new).astype(BF16)
            acc_ref[e] = jnp.exp2(m - m_new) * acc_ref[e] + _dot(vt_ref[e, g], p)

    def finish():
        lv = lamv_ref[...]
        lam = (jnp.exp(jnp.sum(lv[0:1] * lv[1:2], axis=1, keepdims=True))
               - jnp.exp(jnp.sum(lv[2:3] * lv[3:4], axis=1, keepdims=True)) + lambda_init)
        for e in range(ATT_HB):
            o12 = acc_ref[e, :HEAD_W] * (1.0 / acc_ref[e, HEAD_W:HEAD_W + 1])
            o = o12[:, :TQ] - lam * o12[:, TQ:]
            o = o * lax.rsqrt(jnp.mean(o * o, axis=0, keepdims=True) + RMS_EPS)
            o = o * g_ref[...] * (1.0 - lambda_init)
            o_ref[:, e * HEAD_W:(e + 1) * HEAD_W] = o.T.astype(BF16)

    @pl.when(n == 1)
    def _():
        scores(0, sa_ref, ta_ref, True)
        accumulate(0, sa_ref, ta_ref)
        finish()

    @pl.when(n > 1)
    def _():
        scores(0, sa_ref, ta_ref, False)

    def pair(i, _):
        g = 2 * i
        scores(g + 1, sb_ref, tb_ref, False)
        accumulate(g, sa_ref, ta_ref)
        scores(g + 2, sa_ref, ta_ref, False)
        accumulate(g + 1, sb_ref, tb_ref)
        return 0

    lax.fori_loop(0, jnp.maximum(n - 2, 0) // 2, pair, 0)

    @pl.when((n > 1) & (n % 2 == 0))
    def _():
        scores(n - 1, sb_ref, tb_ref, True)
        accumulate(n - 2, sa_ref, ta_ref)
        accumulate(n - 1, sb_ref, tb_ref)
        finish()

    @pl.when((n > 1) & (n % 2 == 1))
    def _():
        scores(n - 2, sb_ref, tb_ref, False)
        accumulate(n - 3, sa_ref, ta_ref)
        scores(n - 1, sa_ref, ta_ref, True)
        accumulate(n - 2, sb_ref, tb_ref)
        accumulate(n - 1, sa_ref, ta_ref)
        finish()


def _diff_attention(qt, k, pos, vt, qfeat, corr, lamv, gcol, lambda_init):
    B, S, _ = k.shape
    nq = S // ATT_TQ
    gw = ATT_HB * HEAD_W
    lanes = 2 * ATT_TQ
    return pl.pallas_call(
        functools.partial(_attn_kernel, lambda_init=lambda_init),
        grid=(B, DIFF_HEADS // ATT_HB, nq),
        in_specs=[
            pl.BlockSpec((None, gw, ATT_TQ), lambda b, h, i: (b, h, i)),
            pl.BlockSpec((None, S, gw), lambda b, h, i: (b, 0, h)),
            pl.BlockSpec((S, LANES), lambda b, h, i: (0, 0)),
            pl.BlockSpec((None, ATT_HB, S // ATT_TK, VT_ROWS, ATT_TK), lambda b, h, i: (b, h, 0, 0, 0)),
            pl.BlockSpec((ATT_HB, LANES, 1), lambda b, h, i: (h, 0, 0)),
            pl.BlockSpec((ATT_HB, ATT_TK, lanes), lambda b, h, i: (h, 0, 0)),
            pl.BlockSpec((4, DIFF_QK_DIM), lambda b, h, i: (0, 0)),
            pl.BlockSpec((HEAD_W, 1), lambda b, h, i: (0, 0)),
        ],
        out_specs=pl.BlockSpec((None, ATT_TQ, gw), lambda b, h, i: (b, i, h)),
        scratch_shapes=[
            pltpu.VMEM((ATT_HB, HEAD_W + LANES, lanes), BF16),
            pltpu.VMEM((ATT_HB, ATT_TK, lanes), F32),
            pltpu.VMEM((ATT_HB, ATT_TK, lanes), F32),
            pltpu.VMEM((ATT_HB, 1, lanes), F32),
            pltpu.VMEM((ATT_HB, 1, lanes), F32),
            pltpu.VMEM((ATT_HB, 1, lanes), F32),
            pltpu.VMEM((ATT_HB, VT_ROWS, lanes), F32),
        ],
        out_shape=jax.ShapeDtypeStruct((B, S, DIFF_WIDTH), BF16),
        compiler_params=_cparams(("arbitrary", "arbitrary", "arbitrary")),
        name="diff_attn",
    )(qt, k, pos, vt, qfeat, corr, lamv, gcol)


def _mixer_kernel(x_ref, attn_ref, wuv_ref, buv_ref, wg_ref, bg_ref, lng_ref, lnb_ref, sgw_ref, sgb_ref,
                  wa_ref, wb_ref, wo_ref, g1_ref, b1_ref, h_ref):
    x = x_ref[...]
    xb = x.astype(BF16)
    uv = _gelu(_dot(xb, wuv_ref[...]) + buv_ref[...])
    u = uv[:, :GMLP_WIDTH]
    v = _layer_norm(uv[:, GMLP_WIDTH:], lng_ref[...], lnb_ref[...]).astype(BF16)

    ti = lax.broadcasted_iota(jnp.int32, (GMLP_BLOCK, GMLP_BLOCK), 0) // CHUNK
    si = lax.broadcasted_iota(jnp.int32, (GMLP_BLOCK, GMLP_BLOCK), 1) // CHUNK
    causal = ti >= si
    w = [jnp.where(causal, sgw_ref[g], 0.0).astype(BF16) for g in range(GMLP_GROUPS)]
    lane = lax.broadcasted_iota(jnp.int32, (GMLP_BLOCK, LANES), 1)
    lo = lane < GMLP_GROUP_DIM
    vzero = jnp.zeros((GMLP_BLOCK, LANES), BF16)

    blocks = []
    for n in range(MIX_TM // GMLP_BLOCK):
        slabs = []
        for p in range(GMLP_WIDTH // LANES):
            vs = v[n * GMLP_BLOCK:(n + 1) * GMLP_BLOCK, p * LANES:(p + 1) * LANES]
            mixed = (_dot(w[2 * p], jnp.where(lo, vs, vzero))
                     + _dot(w[2 * p + 1], jnp.where(lo, vzero, vs)) + sgb_ref[p])
            slabs.append(mixed)
        blocks.append(jnp.concatenate(slabs, axis=1))
    mixed = jnp.concatenate(blocks, axis=0)

    y_a = _dot((u * mixed).astype(BF16), wa_ref[...])
    y_b = _dot(attn_ref[...], wb_ref[...])
    gates = _sigmoid(_dot(xb, wg_ref[...]) + bg_ref[...])
    z = gates[:, :D_MODEL] * y_a + gates[:, D_MODEL:] * y_b
    mix = _dot(z.astype(BF16), wo_ref[...])
    h_ref[...] = _layer_norm(ALPHA * x + mix, g1_ref[...], b1_ref[...])


def _mixer(x2, attn2, wuv, buv, wg, bg, lng, lnb, sgw, sgb, wa, wb, wo, g1, b1):
    T, D = x2.shape
    c2 = lambda i: (0, 0)
    c3 = lambda i: (0, 0, 0)
    full = lambda a: pl.BlockSpec(a.shape, c2 if a.ndim == 2 else c3)
    return pl.pallas_call(
        _mixer_kernel,
        grid=(T // MIX_TM,),
        in_specs=[pl.BlockSpec((MIX_TM, D), lambda i: (i, 0)),
                  pl.BlockSpec((MIX_TM, DIFF_WIDTH), lambda i: (i, 0))]
                 + [full(a) for a in (wuv, buv, wg, bg, lng, lnb, sgw, sgb, wa, wb, wo, g1, b1)],
        out_specs=pl.BlockSpec((MIX_TM, D), lambda i: (i, 0)),
        out_shape=jax.ShapeDtypeStruct((T, D), F32),
        compiler_params=_cparams(("arbitrary",)),
        name="mixer",
    )(x2, attn2, wuv, buv, wg, bg, lng, lnb, sgw, sgb, wa, wb, wo, g1, b1)


def _router_kernel(h_ref, wr_hi_ref, wr_lo_ref, br_ref, eid_ref, wts_ref):
    hf = h_ref[...]
    hi = hf.astype(BF16)
    lo = (hf - hi.astype(F32)).astype(BF16)
    logits = (_dot_nt(wr_hi_ref[...], hi) + _dot_nt(wr_lo_ref[...], hi)
              + _dot_nt(wr_hi_ref[...], lo)) + br_ref[...]
    gidx = lax.broadcasted_iota(jnp.int32, (8, ROUTE_TM), 0)
    gl = jnp.where(gidx < N_GROUPS, logits[0:8], -jnp.inf)
    gmax = jnp.max(gl, axis=0, keepdims=True)
    g_sel = jnp.min(jnp.where(gl == gmax, gidx, N_GROUPS), axis=0, keepdims=True)
    g_w = 1.0 / jnp.sum(jnp.exp(gl - gmax), axis=0, keepdims=True)

    el = logits[8:8 + EXPERTS_PER_GROUP]
    for g in range(1, N_GROUPS):
        el = jnp.where(g_sel == g, logits[8 + g * EXPERTS_PER_GROUP:8 + (g + 1) * EXPERTS_PER_GROUP], el)
    eidx = lax.broadcasted_iota(jnp.int32, el.shape, 0)
    v1 = jnp.max(el, axis=0, keepdims=True)
    i1 = jnp.min(jnp.where(el == v1, eidx, EXPERTS_PER_GROUP), axis=0, keepdims=True)
    rest = jnp.where(eidx == i1, -jnp.inf, el)
    v2 = jnp.max(rest, axis=0, keepdims=True)
    i2 = jnp.min(jnp.where(rest == v2, eidx, EXPERTS_PER_GROUP), axis=0, keepdims=True)
    e21 = jnp.exp(v2 - v1)
    w1 = g_w / (1.0 + e21)
    w2 = g_w * e21 / (1.0 + e21)
    base = g_sel * EXPERTS_PER_GROUP
    eid_ref[0:1, :] = base + i1
    eid_ref[1:2, :] = base + i2
    wts_ref[0:1, :] = w1
    wts_ref[1:2, :] = w2


def _router(h1, wr_hi, wr_lo, br):
    T, D = h1.shape
    c2 = lambda i: (0, 0)
    return pl.pallas_call(
        _router_kernel,
        grid=(T // ROUTE_TM,),
        in_specs=[pl.BlockSpec((ROUTE_TM, D), lambda i: (i, 0)),
                  pl.BlockSpec(wr_hi.shape, c2), pl.BlockSpec(wr_lo.shape, c2), pl.BlockSpec(br.shape, c2)],
        out_specs=[pl.BlockSpec((EXPERT_TOPK, ROUTE_TM), lambda i: (0, i)),
                   pl.BlockSpec((EXPERT_TOPK, ROUTE_TM), lambda i: (0, i))],
        out_shape=[jax.ShapeDtypeStruct((EXPERT_TOPK, T), jnp.int32),
                   jax.ShapeDtypeStruct((EXPERT_TOPK, T), F32)],
        compiler_params=_cparams(("arbitrary",)),
        name="router",
    )(h1, wr_hi, wr_lo, br)


def _rank_kernel(eid_ref, tri_ref, rank_ref, cnt_ref, run_ref):
    @pl.when(pl.program_id(0) == 0)
    def _():
        run_ref[...] = jnp.zeros_like(run_ref)

    eid = eid_ref[...]
    eiota = lax.broadcasted_iota(jnp.int32, (N_EXPERTS, ROUTE_TM), 0)
    run = run_ref[...]
    ranks = []
    for c in range(EXPERT_TOPK):
        hot = eiota == eid[c:c + 1]
        hotf = hot.astype(F32)
        before = _dot(hotf.astype(BF16), tri_ref[...])
        ranks.append(jnp.sum(jnp.where(hot, run + before, 0.0), axis=0, keepdims=True))
        run = run + jnp.sum(hotf, axis=1, keepdims=True)
    run_ref[...] = run
    for c in range(EXPERT_TOPK):
        rank_ref[c:c + 1, :] = ranks[c].astype(jnp.int32)
    cnt_ref[...] = run.astype(jnp.int32)


def _ranks(eid, tri):
    _, T = eid.shape
    return pl.pallas_call(
        _rank_kernel,
        grid=(T // ROUTE_TM,),
        in_specs=[pl.BlockSpec((EXPERT_TOPK, ROUTE_TM), lambda i: (0, i)),
                  pl.BlockSpec(tri.shape, lambda i: (0, 0))],
        out_specs=[pl.BlockSpec((EXPERT_TOPK, ROUTE_TM), lambda i: (0, i)),
                   pl.BlockSpec((N_EXPERTS, 1), lambda i: (0, 0))],
        out_shape=[jax.ShapeDtypeStruct((EXPERT_TOPK, T), jnp.int32),
                   jax.ShapeDtypeStruct((N_EXPERTS, 1), jnp.int32)],
        scratch_shapes=[pltpu.VMEM((N_EXPERTS, 1), F32)],
        compiler_params=_cparams(("arbitrary",)),
        name="rank",
    )(eid, tri)


ROW_SUB = D_MODEL // LANES


def _dispatch_kernel(cnt_ref, pad_ref, d0_ref, d1_ref, h_ref, xs_ref, rows_ref, zero_ref, sem, zsem):
    dest_refs = (d0_ref, d1_ref)

    @pl.when(pl.program_id(0) == 0)
    def _():
        zero_ref[...] = jnp.zeros_like(zero_ref)

        def zero_row(r):
            return pltpu.make_async_copy(zero_ref.at[0], xs_ref.at[r], zsem)

        def zero_block(b):
            return pltpu.make_async_copy(zero_ref, xs_ref.at[pl.ds(b * EXPERT_BLOCK, EXPERT_BLOCK)], zsem)

        def per_expert(e, _):
            lo = pad_ref[e] + cnt_ref[e]
            hi = pad_ref[e] + (cnt_ref[e] + EXPERT_BLOCK - 1) // EXPERT_BLOCK * EXPERT_BLOCK

            def start(r, c):
                zero_row(r).start()
                return c

            def wait(r, c):
                zero_row(r).wait()
                return c

            lax.fori_loop(lo, hi, start, 0)
            lax.fori_loop(lo, hi, wait, 0)
            return hi // EXPERT_BLOCK

        used = lax.fori_loop(0, N_EXPERTS, per_expert, 0)

        def start_block(b, c):
            zero_block(b).start()
            return c

        def wait_block(b, c):
            zero_block(b).wait()
            return c

        lax.fori_loop(used, xs_ref.shape[0] // EXPERT_BLOCK, start_block, 0)
        lax.fori_loop(used, xs_ref.shape[0] // EXPERT_BLOCK, wait_block, 0)

    rows_ref[...] = h_ref[...].reshape(ROW_TM, ROW_SUB, LANES)

    def row_copy(c, t, dest):
        return pltpu.make_async_copy(rows_ref.at[t], xs_ref.at[dest], sem)

    def issue(t, _):
        for c in range(EXPERT_TOPK):
            row_copy(c, t, dest_refs[c][t]).start(priority=c)
        return 0

    lax.fori_loop(0, ROW_TM, issue, 0, unroll=ROW_UNROLL)

    def wait(t, _):
        for c in range(EXPERT_TOPK):
            row_copy(c, t, 0).wait()
        return 0

    lax.fori_loop(0, ROW_TM, wait, 0, unroll=ROW_UNROLL)


def _dispatch(counts, pad_start, dest, h1, n_rows):
    T, D = h1.shape
    grid_spec = pltpu.PrefetchScalarGridSpec(
        num_scalar_prefetch=2,
        grid=(T // ROW_TM,),
        in_specs=[
            pl.BlockSpec((ROW_TM,), lambda i, c, p: (i,), memory_space=pltpu.SMEM),
            pl.BlockSpec((ROW_TM,), lambda i, c, p: (i,), memory_space=pltpu.SMEM),
            pl.BlockSpec((ROW_TM, D), lambda i, c, p: (i, 0)),
        ],
        out_specs=pl.BlockSpec(memory_space=pl.ANY),
        scratch_shapes=[pltpu.VMEM((ROW_TM, ROW_SUB, LANES), F32),
                        pltpu.VMEM((EXPERT_BLOCK, ROW_SUB, LANES), F32),
                        pltpu.SemaphoreType.DMA(()), pltpu.SemaphoreType.DMA(())],
    )
    return pl.pallas_call(
        _dispatch_kernel,
        grid_spec=grid_spec,
        out_shape=jax.ShapeDtypeStruct((n_rows, ROW_SUB, LANES), F32),
        compiler_params=_cparams(("arbitrary",)),
        name="dispatch",
    )(counts, pad_start, dest[0], dest[1], h1)


def _expert_kernel(be_ref, nb_ref, xs_ref, wg_ref, wu_ref, wd_ref, o_ref):
    i = pl.program_id(0)

    @pl.when(i < nb_ref[0])
    def _():
        xb = xs_ref[...].reshape(EXPERT_BLOCK, D_MODEL).astype(BF16)
        gate = _dot(xb, wg_ref[...])
        up = _dot(xb, wu_ref[...])
        hb = (gate * _sigmoid(gate) * up).astype(BF16)
        o_ref[...] = _dot(hb, wd_ref[...]).reshape(EXPERT_BLOCK, ROW_SUB, LANES)

    @pl.when(i >= nb_ref[0])
    def _():
        o_ref[...] = jnp.zeros_like(o_ref)


def _experts(block_e, n_used, xs, wg, wu, wd):
    P = xs.shape[0]
    D = D_MODEL
    rows = pl.BlockSpec((EXPERT_BLOCK, ROW_SUB, LANES), lambda i, be, nb: (i, 0, 0))
    grid_spec = pltpu.PrefetchScalarGridSpec(
        num_scalar_prefetch=2,
        grid=(P // EXPERT_BLOCK,),
        in_specs=[
            rows,
            pl.BlockSpec((None, D, EXPERT_HIDDEN), lambda i, be, nb: (be[i], 0, 0)),
            pl.BlockSpec((None, D, EXPERT_HIDDEN), lambda i, be, nb: (be[i], 0, 0)),
            pl.BlockSpec((None, EXPERT_HIDDEN, D), lambda i, be, nb: (be[i], 0, 0)),
        ],
        out_specs=rows,
    )
    return pl.pallas_call(
        _expert_kernel,
        grid_spec=grid_spec,
        out_shape=jax.ShapeDtypeStruct(xs.shape, F32),
        compiler_params=_cparams(("arbitrary",)),
        name="experts",
    )(block_e, n_used, xs, wg, wu, wd)


def _combine_kernel(d0_ref, d1_ref, n0_ref, n1_ref, h_ref, wts_ref, g2_ref, b2_ref, ys_ref, o_ref,
                    rows_ref, sems):
    i = pl.program_id(0)
    n = pl.num_programs(0)
    slot = i % 2

    def gather(dest_refs, buf):
        def issue(t, _):
            for c in range(EXPERT_TOPK):
                pltpu.make_async_copy(ys_ref.at[dest_refs[c][t]], rows_ref.at[buf, c, t],
                                      sems.at[buf]).start(priority=c)
            return 0

        lax.fori_loop(0, ROW_TM, issue, 0, unroll=ROW_UNROLL)

    @pl.when(i == 0)
    def _():
        gather((d0_ref, d1_ref), 0)

    @pl.when(i + 1 < n)
    def _():
        gather((n0_ref, n1_ref), 1 - slot)

    def drain(t, _):
        for c in range(EXPERT_TOPK):
            pltpu.make_async_copy(ys_ref.at[0], rows_ref.at[slot, c, t], sems.at[slot]).wait()
        return 0

    lax.fori_loop(0, ROW_TM, drain, 0, unroll=ROW_UNROLL)
    w = wts_ref[...]
    ffn = (w[:, 0:1] * rows_ref[slot, 0].reshape(ROW_TM, D_MODEL)
           + w[:, 1:2] * rows_ref[slot, 1].reshape(ROW_TM, D_MODEL))
    o_ref[...] = _layer_norm(ALPHA * h_ref[...] + ffn, g2_ref[...], b2_ref[...])


def _combine(dest, h1, wts_t, g2, b2, ys):
    T, D = h1.shape
    n = T // ROW_TM
    cur = pl.BlockSpec((ROW_TM,), lambda i: (i,), memory_space=pltpu.SMEM)
    nxt = pl.BlockSpec((ROW_TM,), lambda i: (jnp.minimum(i + 1, n - 1),), memory_space=pltpu.SMEM)
    return pl.pallas_call(
        _combine_kernel,
        grid=(n,),
        in_specs=[
            cur, cur, nxt, nxt,
            pl.BlockSpec((ROW_TM, D), lambda i: (i, 0)),
            pl.BlockSpec((ROW_TM, EXPERT_TOPK), lambda i: (i, 0)),
            pl.BlockSpec((1, D), lambda i: (0, 0)),
            pl.BlockSpec((1, D), lambda i: (0, 0)),
            pl.BlockSpec(memory_space=pl.ANY),
        ],
        out_specs=pl.BlockSpec((ROW_TM, D), lambda i: (i, 0)),
        scratch_shapes=[pltpu.VMEM((2, EXPERT_TOPK, ROW_TM, ROW_SUB, LANES), F32),
                        pltpu.SemaphoreType.DMA((2,))],
        out_shape=jax.ShapeDtypeStruct((T, D), F32),
        compiler_params=_cparams(("arbitrary",)),
        name="combine",
    )(dest[0], dest[1], dest[0], dest[1], h1, wts_t, g2, b2, ys)


def _attn_bias_tables(S):
    c = jnp.asarray((2.0 ** (-8.0 * np.arange(1, DIFF_HEADS + 1, dtype=np.float64) / DIFF_HEADS)) * LOG2E, F32)
    c_hi = c.astype(BF16)
    c_mid = (c - c_hi.astype(F32)).astype(BF16)
    c_lo = (c - c_hi.astype(F32) - c_mid.astype(F32)).astype(BF16)
    pieces = jnp.stack([c_hi, c_mid, c_lo], axis=1)
    qfeat = jnp.concatenate([pieces * LANES, pieces, jnp.zeros((DIFF_HEADS, LANES - 6), BF16)], axis=1)
    j = np.arange(S)
    pos = np.zeros((S, LANES), np.float32)
    pos[:, 0:3] = (j // LANES)[:, None]
    pos[:, 3:6] = (j % LANES)[:, None]
    kj = np.arange(ATT_TK, dtype=np.float64)[:, None]
    qi = np.tile(np.arange(ATT_TQ, dtype=np.float64), 2)[None, :]
    visible = (kj // CHUNK) <= (qi // CHUNK)
    after = 2.0 * np.minimum(qi - kj, 0.0)
    corr = jnp.where(visible[None], c[:, None, None] * jnp.asarray(after, F32)[None], MASKED)
    return jnp.asarray(pos, BF16), qfeat[:, :, None], corr.astype(F32)


def _layer(h, layer, w_in, b_in, sg_ln_g, sg_ln_b, sg_w, sg_b, w_branch_a, lam_q1, lam_k1, lam_q2, lam_k2,
           subln_g, w_branch_b, w_out, ln1_g, ln1_b, w_group, b_group, w_expert, b_expert,
           w_gate, w_up, w_down, ln2_g, ln2_b):
    B, S, D = h.shape
    T = B * S
    row = lambda a: a.reshape(1, -1)
    col = lambda a: a.reshape(-1, 1)

    wk = w_in[:, OFF_K:OFF_VAL].astype(BF16)
    wqt = w_in[:, OFF_Q:OFF_K].T.astype(BF16)
    wvt = w_in[:, OFF_VAL:OFF_GA].T.astype(BF16)
    k, qt, vt = _qkv_proj(h, wk, row(b_in[OFF_K:OFF_VAL]), wqt, col(b_in[OFF_Q:OFF_K]),
                          wvt, col(b_in[OFF_VAL:OFF_GA]))
    pos, qfeat, corr = _attn_bias_tables(S)
    lambda_init = 0.8 - 0.6 * math.exp(-0.3 * layer)
    lamv = jnp.stack([lam_q1, lam_k1, lam_q2, lam_k2]).astype(F32)
    attn = _diff_attention(qt, k, pos, vt, qfeat, corr, lamv, col(subln_g), lambda_init)

    sgb = jnp.repeat(sg_b.reshape(GMLP_GROUPS // 2, 2, GMLP_BLOCK), GMLP_GROUP_DIM, axis=1)
    sgb = jnp.transpose(sgb, (0, 2, 1))
    h1 = _mixer(h.reshape(T, D), attn.reshape(T, DIFF_WIDTH),
                w_in[:, OFF_U:OFF_Q].astype(BF16), row(b_in[OFF_U:OFF_Q]),
                w_in[:, OFF_GA:IN_WIDTH].astype(BF16), row(b_in[OFF_GA:IN_WIDTH]),
                row(sg_ln_g), row(sg_ln_b), sg_w, sgb,
                w_branch_a.astype(BF16), w_branch_b.astype(BF16), w_out.astype(BF16),
                row(ln1_g), row(ln1_b))

    wr = jnp.zeros((LANES, D), F32)
    wr = wr.at[0:N_GROUPS].set(w_group.T).at[8:8 + N_EXPERTS].set(w_expert.T)
    wr_hi = wr.astype(BF16)
    wr_lo = (wr - wr_hi.astype(F32)).astype(BF16)
    br = jnp.zeros((LANES, 1), F32)
    br = br.at[0:N_GROUPS, 0].set(b_group.astype(F32)).at[8:8 + N_EXPERTS, 0].set(b_expert.astype(F32))
    eid, wts = _router(h1, wr_hi, wr_lo, br)

    tri = jnp.asarray(np.triu(np.ones((ROUTE_TM, ROUTE_TM), np.float32), k=1), BF16)
    rank, counts = _ranks(eid, tri)

    counts = counts[:, 0]
    padded = (counts + EXPERT_BLOCK - 1) // EXPERT_BLOCK * EXPERT_BLOCK
    pad_end = jnp.cumsum(padded)
    pad_start = (pad_end - padded).astype(jnp.int32)
    P = T * EXPERT_TOPK + N_EXPERTS * EXPERT_BLOCK
    n_blocks = P // EXPERT_BLOCK
    block_start = jnp.arange(n_blocks, dtype=jnp.int32) * EXPERT_BLOCK
    block_e = jnp.minimum(jnp.sum(pad_end[None, :] <= block_start[:, None], axis=1), N_EXPERTS - 1).astype(jnp.int32)
    n_used = (pad_end[-1:] // EXPERT_BLOCK).astype(jnp.int32)

    hot = eid[:, :, None] == jnp.arange(N_EXPERTS, dtype=jnp.int32)
    dest = jnp.sum(jnp.where(hot, pad_start, 0), axis=-1) + rank
    xs = _dispatch(counts, pad_start, dest, h1, P)
    ys = _experts(block_e, n_used, xs, w_gate.astype(BF16), w_up.astype(BF16), w_down.astype(BF16))
    out = _combine(dest, h1, wts.T, row(ln2_g), row(ln2_b), ys)
    return out.reshape(B, S, D)


def kernel(x, w_in, b_in, sg_ln_g, sg_ln_b, sg_w, sg_b, w_branch_a, lam_q1, lam_k1, lam_q2, lam_k2, subln_g,
           w_branch_b, w_out, ln1_g, ln1_b, w_group, b_group, w_expert, b_expert, w_gate, w_up, w_down,
           ln2_g, ln2_b):
    h = x
    for layer in range(DEPTH):
        h = _layer(h, layer, w_in[layer], b_in[layer], sg_ln_g[layer], sg_ln_b[layer], sg_w[layer],
                   sg_b[layer], w_branch_a[layer], lam_q1[layer], lam_k1[layer], lam_q2[layer],
                   lam_k2[layer], subln_g[layer], w_branch_b[layer], w_out[layer], ln1_g[layer],
                   ln1_b[layer], w_group[layer], b_group[layer], w_expert[layer], b_expert[layer],
                   w_gate[layer], w_up[layer], w_down[layer], ln2_g[layer], ln2_b[layer])
    return h
```

```python
import functools
import math

import jax
import jax.numpy as jnp
import numpy as np
from jax import lax
from jax.experimental import pallas as pl
from jax.experimental.pallas import tpu as pltpu

D_MODEL = 1024
DEPTH = 1
CHUNK = 64
GMLP_GROUPS = 8
GMLP_GROUP_DIM = 64
GMLP_WIDTH = GMLP_GROUPS * GMLP_GROUP_DIM
GMLP_BLOCK = 128
DIFF_HEADS = 8
DIFF_QK_DIM = 64
DIFF_V_DIM = 2 * DIFF_QK_DIM
DIFF_QK_WIDTH = DIFF_HEADS * 2 * DIFF_QK_DIM
DIFF_WIDTH = DIFF_HEADS * DIFF_V_DIM
N_GROUPS = 4
EXPERTS_PER_GROUP = 8
N_EXPERTS = N_GROUPS * EXPERTS_PER_GROUP
EXPERT_TOPK = 2
EXPERT_HIDDEN = 512
DISPATCH_BLOCK = 256
ALPHA = (2.0 * DEPTH) ** 0.25
LN_EPS = 1e-5
RMS_EPS = 1e-5
OFF_U = 0
OFF_V = OFF_U + GMLP_WIDTH
OFF_Q = OFF_V + GMLP_WIDTH
OFF_K = OFF_Q + DIFF_QK_WIDTH
OFF_VAL = OFF_K + DIFF_QK_WIDTH
OFF_GA = OFF_VAL + DIFF_WIDTH
OFF_GB = OFF_GA + D_MODEL
IN_WIDTH = OFF_GB + D_MODEL

LANES = 128
HEAD_W = 2 * DIFF_QK_DIM
MASKED = -1e30
VMEM_LIMIT = 56 * 1024 * 1024

PROJ_TM = 512
ATT_T = 256
ATT_TK = 2 * ATT_T
ATT_TQ = ATT_TK
ATT_HB = 2
LOG2E = math.log2(math.e)
VT_ROWS = HEAD_W + 16
MIX_TM = 512
ROUTE_TM = 512
ROW_TM = 256
EXPERT_BLOCK = 2 * DISPATCH_BLOCK
ROW_UNROLL = 8

BF16 = jnp.bfloat16
F32 = jnp.float32


def _cparams(sem):
    return pltpu.CompilerParams(dimension_semantics=sem, vmem_limit_bytes=VMEM_LIMIT)


def _dot(a, b):
    return jnp.dot(a, b, preferred_element_type=F32)


def _dot_nt(a, b):
    return lax.dot_general(a, b, (((1,), (1,)), ((), ())), preferred_element_type=F32)


def _gelu(x):
    return 0.5 * x * (1.0 + lax.erf(x * (2.0 ** -0.5)))


def _sigmoid(x):
    return 1.0 / (1.0 + jnp.exp(-x))


def _layer_norm(x, g, b):
    mu = jnp.mean(x, axis=-1, keepdims=True)
    xc = x - mu
    var = jnp.mean(xc * xc, axis=-1, keepdims=True)
    return xc * lax.rsqrt(var + LN_EPS) * g + b


def _qkv_kernel(x_ref, wk_ref, bk_ref, wqt_ref, bq_ref, wvt_ref, bv_ref, k_ref, qt_ref, vt_ref):
    xb = x_ref[...].astype(BF16)
    k_ref[...] = (_dot(xb, wk_ref[...]) + bk_ref[...]).astype(BF16)
    qt = (_dot_nt(wqt_ref[...], xb) + bq_ref[...]) * (DIFF_QK_DIM ** -0.5 * LOG2E)
    qt_ref[...] = qt.astype(BF16)
    vt = (_dot_nt(wvt_ref[...], xb) + bv_ref[...]).astype(BF16)
    ones_row = (lax.broadcasted_iota(jnp.int32, (VT_ROWS - HEAD_W, ATT_TK), 0) == 0).astype(BF16)
    for hh in range(DIFF_HEADS):
        for j in range(PROJ_TM // ATT_TK):
            vt_ref[hh, j, :HEAD_W] = vt[hh * HEAD_W:(hh + 1) * HEAD_W, j * ATT_TK:(j + 1) * ATT_TK]
            vt_ref[hh, j, HEAD_W:] = ones_row


def _qkv_proj(x, wk, bk, wqt, bq, wvt, bv):
    B, S, D = x.shape
    nk = S // ATT_TK
    jt = PROJ_TM // ATT_TK
    const2 = lambda b, i: (0, 0)
    return pl.pallas_call(
        _qkv_kernel,
        grid=(B, S // PROJ_TM),
        in_specs=[
            pl.BlockSpec((None, PROJ_TM, D), lambda b, i: (b, i, 0)),
            pl.BlockSpec((D, DIFF_QK_WIDTH), const2),
            pl.BlockSpec((1, DIFF_QK_WIDTH), const2),
            pl.BlockSpec((DIFF_QK_WIDTH, D), const2),
            pl.BlockSpec((DIFF_QK_WIDTH, 1), const2),
            pl.BlockSpec((DIFF_WIDTH, D), const2),
            pl.BlockSpec((DIFF_WIDTH, 1), const2),
        ],
        out_specs=[
            pl.BlockSpec((None, PROJ_TM, DIFF_QK_WIDTH), lambda b, i: (b, i, 0)),
            pl.BlockSpec((None, DIFF_QK_WIDTH, PROJ_TM), lambda b, i: (b, 0, i)),
            pl.BlockSpec((None, DIFF_HEADS, jt, VT_ROWS, ATT_TK), lambda b, i: (b, 0, i, 0, 0)),
        ],
        out_shape=[
            jax.ShapeDtypeStruct((B, S, DIFF_QK_WIDTH), BF16),
            jax.ShapeDtypeStruct((B, DIFF_QK_WIDTH, S), BF16),
            jax.ShapeDtypeStruct((B, DIFF_HEADS, nk, VT_ROWS, ATT_TK), BF16),
        ],
        compiler_params=_cparams(("arbitrary", "arbitrary")),
        name="qkv_proj",
    )(x, wk, bk, wqt, bq, wvt, bv)


def _attn_kernel(qt_ref, k_ref, pos_ref, vt_ref, qf_ref, corr_ref, lamv_ref, g_ref, o_ref,
                 q12_ref, sa_ref, sb_ref, ta_ref, tb_ref, m_ref, acc_ref, *, lambda_init):
    qi = pl.program_id(2)
    T = ATT_T
    TQ = ATT_TQ
    n = qi + 1

    row = lax.broadcasted_iota(jnp.int32, (HEAD_W, TQ), 0)
    zero = jnp.zeros((HEAD_W, TQ), BF16)
    for e in range(ATT_HB):
        q = qt_ref[e * HEAD_W:(e + 1) * HEAD_W, :]
        q12_ref[e, :HEAD_W, :TQ] = jnp.where(row < DIFF_QK_DIM, q, zero)
        q12_ref[e, :HEAD_W, TQ:] = jnp.where(row >= DIFF_QK_DIM, q, zero)
        q12_ref[e, HEAD_W:, :] = jnp.broadcast_to(qf_ref[e], (LANES, 2 * TQ))
    acc_ref[...] = jnp.zeros_like(acc_ref)
    m_ref[...] = jnp.full(m_ref.shape, MASKED, F32)

    def scores(g, s_ref, t_ref, diagonal):
        rows = pl.ds(pl.multiple_of(g * 2 * T, 2 * T), 2 * T)
        for e in range(ATT_HB):
            kx = jnp.concatenate([k_ref[rows, e * HEAD_W:(e + 1) * HEAD_W], pos_ref[rows, :]], axis=1)
            t = _dot(kx, q12_ref[e])
            if diagonal:
                t = t + corr_ref[e]
            s_ref[e] = t
            t_ref[e] = jnp.max(t, axis=0, keepdims=True)

    def accumulate(g, s_ref, t_ref):
        for e in range(ATT_HB):
            m = m_ref[e]
            m_new = jnp.maximum(m, t_ref[e])
            m_ref[e] = m_new
            p = jnp.exp2(s_ref[e] - m_new).astype(BF16)
            acc_ref[e] = jnp.exp2(m - m_new) * acc_ref[e] + _dot(vt_ref[e, g], p)

    def finish():
        lv = lamv_ref[...]
        lam = (jnp.exp(jnp.sum(lv[0:1] * lv[1:2], axis=1, keepdims=True))
               - jnp.exp(jnp.sum(lv[2:3] * lv[3:4], axis=1, keepdims=True)) + lambda_init)
        for e in range(ATT_HB):
            o12 = acc_ref[e, :HEAD_W] * (1.0 / acc_ref[e, HEAD_W:HEAD_W + 1])
            o = o12[:, :TQ] - lam * o12[:, TQ:]
            o = o * lax.rsqrt(jnp.mean(o * o, axis=0, keepdims=True) + RMS_EPS)
            o = o * g_ref[...] * (1.0 - lambda_init)
            o_ref[:, e * HEAD_W:(e + 1) * HEAD_W] = o.T.astype(BF16)

    @pl.when(n == 1)
    def _():
        scores(0, sa_ref, ta_ref, True)
        accumulate(0, sa_ref, ta_ref)
        finish()

    @pl.when(n > 1)
    def _():
        scores(0, sa_ref, ta_ref, False)

    def pair(i, _):
        g = 2 * i
        scores(g + 1, sb_ref, tb_ref, False)
        accumulate(g, sa_ref, ta_ref)
        scores(g + 2, sa_ref, ta_ref, False)
        accumulate(g + 1, sb_ref, tb_ref)
        return 0

    lax.fori_loop(0, jnp.maximum(n - 2, 0) // 2, pair, 0)

    @pl.when((n > 1) & (n % 2 == 0))
    def _():
        scores(n - 1, sb_ref, tb_ref, True)
        accumulate(n - 2, sa_ref, ta_ref)
        accumulate(n - 1, sb_ref, tb_ref)
        finish()

    @pl.when((n > 1) & (n % 2 == 1))
    def _():
        scores(n - 2, sb_ref, tb_ref, False)
        accumulate(n - 3, sa_ref, ta_ref)
        scores(n - 1, sa_ref, ta_ref, True)
        accumulate(n - 2, sb_ref, tb_ref)
        accumulate(n - 1, sa_ref, ta_ref)
        finish()


def _diff_attention(qt, k, pos, vt, qfeat, corr, lamv, gcol, lambda_init):
    B, S, _ = k.shape
    nq = S // ATT_TQ
    gw = ATT_HB * HEAD_W
    lanes = 2 * ATT_TQ
    return pl.pallas_call(
        functools.partial(_attn_kernel, lambda_init=lambda_init),
        grid=(B, DIFF_HEADS // ATT_HB, nq),
        in_specs=[
            pl.BlockSpec((None, gw, ATT_TQ), lambda b, h, i: (b, h, i)),
            pl.BlockSpec((None, S, gw), lambda b, h, i: (b, 0, h)),
            pl.BlockSpec((S, LANES), lambda b, h, i: (0, 0)),
            pl.BlockSpec((None, ATT_HB, S // ATT_TK, VT_ROWS, ATT_TK), lambda b, h, i: (b, h, 0, 0, 0)),
            pl.BlockSpec((ATT_HB, LANES, 1), lambda b, h, i: (h, 0, 0)),
            pl.BlockSpec((ATT_HB, ATT_TK, lanes), lambda b, h, i: (h, 0, 0)),
            pl.BlockSpec((4, DIFF_QK_DIM), lambda b, h, i: (0, 0)),
            pl.BlockSpec((HEAD_W, 1), lambda b, h, i: (0, 0)),
        ],
        out_specs=pl.BlockSpec((None, ATT_TQ, gw), lambda b, h, i: (b, i, h)),
        scratch_shapes=[
            pltpu.VMEM((ATT_HB, HEAD_W + LANES, lanes), BF16),
            pltpu.VMEM((ATT_HB, ATT_TK, lanes), F32),
            pltpu.VMEM((ATT_HB, ATT_TK, lanes), F32),
            pltpu.VMEM((ATT_HB, 1, lanes), F32),
            pltpu.VMEM((ATT_HB, 1, lanes), F32),
            pltpu.VMEM((ATT_HB, 1, lanes), F32),
            pltpu.VMEM((ATT_HB, VT_ROWS, lanes), F32),
        ],
        out_shape=jax.ShapeDtypeStruct((B, S, DIFF_WIDTH), BF16),
        compiler_params=_cparams(("arbitrary", "arbitrary", "arbitrary")),
        name="diff_attn",
    )(qt, k, pos, vt, qfeat, corr, lamv, gcol)


def _mixer_kernel(x_ref, attn_ref, wuv_ref, buv_ref, wg_ref, bg_ref, lng_ref, lnb_ref, sgw_ref, sgb_ref,
                  wa_ref, wb_ref, wo_ref, g1_ref, b1_ref, h_ref):
    x = x_ref[...]
    xb = x.astype(BF16)
    uv = _gelu(_dot(xb, wuv_ref[...]) + buv_ref[...])
    u = uv[:, :GMLP_WIDTH]
    v = _layer_norm(uv[:, GMLP_WIDTH:], lng_ref[...], lnb_ref[...]).astype(BF16)

    ti = lax.broadcasted_iota(jnp.int32, (GMLP_BLOCK, GMLP_BLOCK), 0) // CHUNK
    si = lax.broadcasted_iota(jnp.int32, (GMLP_BLOCK, GMLP_BLOCK), 1) // CHUNK
    causal = ti >= si
    w = [jnp.where(causal, sgw_ref[g], 0.0).astype(BF16) for g in range(GMLP_GROUPS)]
    lane = lax.broadcasted_iota(jnp.int32, (GMLP_BLOCK, LANES), 1)
    lo = lane < GMLP_GROUP_DIM
    vzero = jnp.zeros((GMLP_BLOCK, LANES), BF16)

    blocks = []
    for n in range(MIX_TM // GMLP_BLOCK):
        slabs = []
        for p in range(GMLP_WIDTH // LANES):
            vs = v[n * GMLP_BLOCK:(n + 1) * GMLP_BLOCK, p * LANES:(p + 1) * LANES]
            mixed = (_dot(w[2 * p], jnp.where(lo, vs, vzero))
                     + _dot(w[2 * p + 1], jnp.where(lo, vzero, vs)) + sgb_ref[p])
            slabs.append(mixed)
        blocks.append(jnp.concatenate(slabs, axis=1))
    mixed = jnp.concatenate(blocks, axis=0)

    y_a = _dot((u * mixed).astype(BF16), wa_ref[...])
    y_b = _dot(attn_ref[...], wb_ref[...])
    gates = _sigmoid(_dot(xb, wg_ref[...]) + bg_ref[...])
    z = gates[:, :D_MODEL] * y_a + gates[:, D_MODEL:] * y_b
    mix = _dot(z.astype(BF16), wo_ref[...])
    h_ref[...] = _layer_norm(ALPHA * x + mix, g1_ref[...], b1_ref[...])


def _mixer(x2, attn2, wuv, buv, wg, bg, lng, lnb, sgw, sgb, wa, wb, wo, g1, b1):
    T, D = x2.shape
    c2 = lambda i: (0, 0)
    c3 = lambda i: (0, 0, 0)
    full = lambda a: pl.BlockSpec(a.shape, c2 if a.ndim == 2 else c3)
    return pl.pallas_call(
        _mixer_kernel,
        grid=(T // MIX_TM,),
        in_specs=[pl.BlockSpec((MIX_TM, D), lambda i: (i, 0)),
                  pl.BlockSpec((MIX_TM, DIFF_WIDTH), lambda i: (i, 0))]
                 + [full(a) for a in (wuv, buv, wg, bg, lng, lnb, sgw, sgb, wa, wb, wo, g1, b1)],
        out_specs=pl.BlockSpec((MIX_TM, D), lambda i: (i, 0)),
        out_shape=jax.ShapeDtypeStruct((T, D), F32),
        compiler_params=_cparams(("arbitrary",)),
        name="mixer",
    )(x2, attn2, wuv, buv, wg, bg, lng, lnb, sgw, sgb, wa, wb, wo, g1, b1)


def _router_kernel(h_ref, wr_hi_ref, wr_lo_ref, br_ref, eid_ref, wts_ref):
    hf = h_ref[...]
    hi = hf.astype(BF16)
    lo = (hf - hi.astype(F32)).astype(BF16)
    logits = (_dot_nt(wr_hi_ref[...], hi) + _dot_nt(wr_lo_ref[...], hi)
              + _dot_nt(wr_hi_ref[...], lo)) + br_ref[...]
    gidx = lax.broadcasted_iota(jnp.int32, (8, ROUTE_TM), 0)
    gl = jnp.where(gidx < N_GROUPS, logits[0:8], -jnp.inf)
    gmax = jnp.max(gl, axis=0, keepdims=True)
    g_sel = jnp.min(jnp.where(gl == gmax, gidx, N_GROUPS), axis=0, keepdims=True)
    g_w = 1.0 / jnp.sum(jnp.exp(gl - gmax), axis=0, keepdims=True)

    el = logits[8:8 + EXPERTS_PER_GROUP]
    for g in range(1, N_GROUPS):
        el = jnp.where(g_sel == g, logits[8 + g * EXPERTS_PER_GROUP:8 + (g + 1) * EXPERTS_PER_GROUP], el)
    eidx = lax.broadcasted_iota(jnp.int32, el.shape, 0)
    v1 = jnp.max(el, axis=0, keepdims=True)
    i1 = jnp.min(jnp.where(el == v1, eidx, EXPERTS_PER_GROUP), axis=0, keepdims=True)
    rest = jnp.where(eidx == i1, -jnp.inf, el)
    v2 = jnp.max(rest, axis=0, keepdims=True)
    i2 = jnp.min(jnp.where(rest == v2, eidx, EXPERTS_PER_GROUP), axis=0, keepdims=True)
    e21 = jnp.exp(v2 - v1)
    w1 = g_w / (1.0 + e21)
    w2 = g_w * e21 / (1.0 + e21)
    base = g_sel * EXPERTS_PER_GROUP
    eid_ref[0:1, :] = base + i1
    eid_ref[1:2, :] = base + i2
    wts_ref[0:1, :] = w1
    wts_ref[1:2, :] = w2


def _router(h1, wr_hi, wr_lo, br):
    T, D = h1.shape
    c2 = lambda i: (0, 0)
    return pl.pallas_call(
        _router_kernel,
        grid=(T // ROUTE_TM,),
        in_specs=[pl.BlockSpec((ROUTE_TM, D), lambda i: (i, 0)),
                  pl.BlockSpec(wr_hi.shape, c2), pl.BlockSpec(wr_lo.shape, c2), pl.BlockSpec(br.shape, c2)],
        out_specs=[pl.BlockSpec((EXPERT_TOPK, ROUTE_TM), lambda i: (0, i)),
                   pl.BlockSpec((EXPERT_TOPK, ROUTE_TM), lambda i: (0, i))],
        out_shape=[jax.ShapeDtypeStruct((EXPERT_TOPK, T), jnp.int32),
                   jax.ShapeDtypeStruct((EXPERT_TOPK, T), F32)],
        compiler_params=_cparams(("arbitrary",)),
        name="router",
    )(h1, wr_hi, wr_lo, br)


def _rank_kernel(eid_ref, tri_ref, rank_ref, cnt_ref, run_ref):
    @pl.when(pl.program_id(0) == 0)
    def _():
        run_ref[...] = jnp.zeros_like(run_ref)

    eid = eid_ref[...]
    eiota = lax.broadcasted_iota(jnp.int32, (N_EXPERTS, ROUTE_TM), 0)
    run = run_ref[...]
    ranks = []
    for c in range(EXPERT_TOPK):
        hot = eiota == eid[c:c + 1]
        hotf = hot.astype(F32)
        before = _dot(hotf.astype(BF16), tri_ref[...])
        ranks.append(jnp.sum(jnp.where(hot, run + before, 0.0), axis=0, keepdims=True))
        run = run + jnp.sum(hotf, axis=1, keepdims=True)
    run_ref[...] = run
    for c in range(EXPERT_TOPK):
        rank_ref[c:c + 1, :] = ranks[c].astype(jnp.int32)
    cnt_ref[...] = run.astype(jnp.int32)


def _ranks(eid, tri):
    _, T = eid.shape
    return pl.pallas_call(
        _rank_kernel,
        grid=(T // ROUTE_TM,),
        in_specs=[pl.BlockSpec((EXPERT_TOPK, ROUTE_TM), lambda i: (0, i)),
                  pl.BlockSpec(tri.shape, lambda i: (0, 0))],
        out_specs=[pl.BlockSpec((EXPERT_TOPK, ROUTE_TM), lambda i: (0, i)),
                   pl.BlockSpec((N_EXPERTS, 1), lambda i: (0, 0))],
        out_shape=[jax.ShapeDtypeStruct((EXPERT_TOPK, T), jnp.int32),
                   jax.ShapeDtypeStruct((N_EXPERTS, 1), jnp.int32)],
        scratch_shapes=[pltpu.VMEM((N_EXPERTS, 1), F32)],
        compiler_params=_cparams(("arbitrary",)),
        name="rank",
    )(eid, tri)


ROW_SUB = D_MODEL // LANES


def _dispatch_kernel(cnt_ref, pad_ref, d0_ref, d1_ref, h_ref, xs_ref, rows_ref, zero_ref, sem, zsem):
    dest_refs = (d0_ref, d1_ref)

    @pl.when(pl.program_id(0) == 0)
    def _():
        zero_ref[...] = jnp.zeros_like(zero_ref)

        def zero_row(r):
            return pltpu.make_async_copy(zero_ref.at[0], xs_ref.at[r], zsem)

        def zero_block(b):
            return pltpu.make_async_copy(zero_ref, xs_ref.at[pl.ds(b * EXPERT_BLOCK, EXPERT_BLOCK)], zsem)

        def per_expert(e, _):
            lo = pad_ref[e] + cnt_ref[e]
            hi = pad_ref[e] + (cnt_ref[e] + EXPERT_BLOCK - 1) // EXPERT_BLOCK * EXPERT_BLOCK

            def start(r, c):
                zero_row(r).start()
                return c

            def wait(r, c):
                zero_row(r).wait()
                return c

            lax.fori_loop(lo, hi, start, 0)
            lax.fori_loop(lo, hi, wait, 0)
            return hi // EXPERT_BLOCK

        used = lax.fori_loop(0, N_EXPERTS, per_expert, 0)

        def start_block(b, c):
            zero_block(b).start()
            return c

        def wait_block(b, c):
            zero_block(b).wait()
            return c

        lax.fori_loop(used, xs_ref.shape[0] // EXPERT_BLOCK, start_block, 0)
        lax.fori_loop(used, xs_ref.shape[0] // EXPERT_BLOCK, wait_block, 0)

    rows_ref[...] = h_ref[...].reshape(ROW_TM, ROW_SUB, LANES)

    def row_copy(c, t, dest):
        return pltpu.make_async_copy(rows_ref.at[t], xs_ref.at[dest], sem)

    def issue(t, _):
        for c in range(EXPERT_TOPK):
            row_copy(c, t, dest_refs[c][t]).start(priority=c)
        return 0

    lax.fori_loop(0, ROW_TM, issue, 0, unroll=ROW_UNROLL)

    def wait(t, _):
        for c in range(EXPERT_TOPK):
            row_copy(c, t, 0).wait()
        return 0

    lax.fori_loop(0, ROW_TM, wait, 0, unroll=ROW_UNROLL)


def _dispatch(counts, pad_start, dest, h1, n_rows):
    T, D = h1.shape
    grid_spec = pltpu.PrefetchScalarGridSpec(
        num_scalar_prefetch=2,
        grid=(T // ROW_TM,),
        in_specs=[
            pl.BlockSpec((ROW_TM,), lambda i, c, p: (i,), memory_space=pltpu.SMEM),
            pl.BlockSpec((ROW_TM,), lambda i, c, p: (i,), memory_space=pltpu.SMEM),
            pl.BlockSpec((ROW_TM, D), lambda i, c, p: (i, 0)),
        ],
        out_specs=pl.BlockSpec(memory_space=pl.ANY),
        scratch_shapes=[pltpu.VMEM((ROW_TM, ROW_SUB, LANES), F32),
                        pltpu.VMEM((EXPERT_BLOCK, ROW_SUB, LANES), F32),
                        pltpu.SemaphoreType.DMA(()), pltpu.SemaphoreType.DMA(())],
    )
    return pl.pallas_call(
        _dispatch_kernel,
        grid_spec=grid_spec,
        out_shape=jax.ShapeDtypeStruct((n_rows, ROW_SUB, LANES), F32),
        compiler_params=_cparams(("arbitrary",)),
        name="dispatch",
    )(counts, pad_start, dest[0], dest[1], h1)


def _expert_kernel(be_ref, nb_ref, xs_ref, wg_ref, wu_ref, wd_ref, o_ref):
    i = pl.program_id(0)

    @pl.when(i < nb_ref[0])
    def _():
        xb = xs_ref[...].reshape(EXPERT_BLOCK, D_MODEL).astype(BF16)
        gate = _dot(xb, wg_ref[...].astype(BF16))
        up = _dot(xb, wu_ref[...].astype(BF16))
        hb = (gate * _sigmoid(gate) * up).astype(BF16)
        o_ref[...] = _dot(hb, wd_ref[...].astype(BF16)).reshape(EXPERT_BLOCK, ROW_SUB, LANES)

    @pl.when(i >= nb_ref[0])
    def _():
        o_ref[...] = jnp.zeros_like(o_ref)


def _experts(block_e, n_used, xs, wg, wu, wd):
    P = xs.shape[0]
    D = D_MODEL
    rows = pl.BlockSpec((EXPERT_BLOCK, ROW_SUB, LANES), lambda i, be, nb: (i, 0, 0))
    grid_spec = pltpu.PrefetchScalarGridSpec(
        num_scalar_prefetch=2,
        grid=(P // EXPERT_BLOCK,),
        in_specs=[
            rows,
            pl.BlockSpec((None, D, EXPERT_HIDDEN), lambda i, be, nb: (be[i], 0, 0)),
            pl.BlockSpec((None, D, EXPERT_HIDDEN), lambda i, be, nb: (be[i], 0, 0)),
            pl.BlockSpec((None, EXPERT_HIDDEN, D), lambda i, be, nb: (be[i], 0, 0)),
        ],
        out_specs=rows,
    )
    return pl.pallas_call(
        _expert_kernel,
        grid_spec=grid_spec,
        out_shape=jax.ShapeDtypeStruct(xs.shape, F32),
        compiler_params=_cparams(("arbitrary",)),
        name="experts",
    )(block_e, n_used, xs, wg, wu, wd)


def _combine_kernel(d0_ref, d1_ref, n0_ref, n1_ref, h_ref, wts_ref, g2_ref, b2_ref, ys_ref, o_ref,
                    rows_ref, sems):
    i = pl.program_id(0)
    n = pl.num_programs(0)
    slot = i % 2

    def gather(dest_refs, buf):
        def issue(t, _):
            for c in range(EXPERT_TOPK):
                pltpu.make_async_copy(ys_ref.at[dest_refs[c][t]], rows_ref.at[buf, c, t],
                                      sems.at[buf]).start(priority=c)
            return 0

        lax.fori_loop(0, ROW_TM, issue, 0, unroll=ROW_UNROLL)

    @pl.when(i == 0)
    def _():
        gather((d0_ref, d1_ref), 0)

    @pl.when(i + 1 < n)
    def _():
        gather((n0_ref, n1_ref), 1 - slot)

    def drain(t, _):
        for c in range(EXPERT_TOPK):
            pltpu.make_async_copy(ys_ref.at[0], rows_ref.at[slot, c, t], sems.at[slot]).wait()
        return 0

    lax.fori_loop(0, ROW_TM, drain, 0, unroll=ROW_UNROLL)
    w = wts_ref[...]
    ffn = (w[:, 0:1] * rows_ref[slot, 0].reshape(ROW_TM, D_MODEL)
           + w[:, 1:2] * rows_ref[slot, 1].reshape(ROW_TM, D_MODEL))
    o_ref[...] = _layer_norm(ALPHA * h_ref[...] + ffn, g2_ref[...], b2_ref[...])


def _combine(dest, h1, wts_t, g2, b2, ys):
    T, D = h1.shape
    n = T // ROW_TM
    cur = pl.BlockSpec((ROW_TM,), lambda i: (i,), memory_space=pltpu.SMEM)
    nxt = pl.BlockSpec((ROW_TM,), lambda i: (jnp.minimum(i + 1, n - 1),), memory_space=pltpu.SMEM)
    return pl.pallas_call(
        _combine_kernel,
        grid=(n,),
        in_specs=[
            cur, cur, nxt, nxt,
            pl.BlockSpec((ROW_TM, D), lambda i: (i, 0)),
            pl.BlockSpec((ROW_TM, EXPERT_TOPK), lambda i: (i, 0)),
            pl.BlockSpec((1, D), lambda i: (0, 0)),
            pl.BlockSpec((1, D), lambda i: (0, 0)),
            pl.BlockSpec(memory_space=pl.ANY),
        ],
        out_specs=pl.BlockSpec((ROW_TM, D), lambda i: (i, 0)),
        scratch_shapes=[pltpu.VMEM((2, EXPERT_TOPK, ROW_TM, ROW_SUB, LANES), F32),
                        pltpu.SemaphoreType.DMA((2,))],
        out_shape=jax.ShapeDtypeStruct((T, D), F32),
        compiler_params=_cparams(("arbitrary",)),
        name="combine",
    )(dest[0], dest[1], dest[0], dest[1], h1, wts_t, g2, b2, ys)


def _attn_bias_tables(S):
    c = jnp.asarray((2.0 ** (-8.0 * np.arange(1, DIFF_HEADS + 1, dtype=np.float64) / DIFF_HEADS)) * LOG2E, F32)
    c_hi = c.astype(BF16)
    c_mid = (c - c_hi.astype(F32)).astype(BF16)
    c_lo = (c - c_hi.astype(F32) - c_mid.astype(F32)).astype(BF16)
    pieces = jnp.stack([c_hi, c_mid, c_lo], axis=1)
    qfeat = jnp.concatenate([pieces * LANES, pieces, jnp.zeros((DIFF_HEADS, LANES - 6), BF16)], axis=1)
    j = np.arange(S)
    pos = np.zeros((S, LANES), np.float32)
    pos[:, 0:3] = (j // LANES)[:, None]
    pos[:, 3:6] = (j % LANES)[:, None]
    kj = np.arange(ATT_TK, dtype=np.float64)[:, None]
    qi = np.tile(np.arange(ATT_TQ, dtype=np.float64), 2)[None, :]
    visible = (kj // CHUNK) <= (qi // CHUNK)
    after = 2.0 * np.minimum(qi - kj, 0.0)
    corr = jnp.where(visible[None], c[:, None, None] * jnp.asarray(after, F32)[None], MASKED)
    return jnp.asarray(pos, BF16), qfeat[:, :, None], corr.astype(F32)


def _layer(h, layer, w_in, b_in, sg_ln_g, sg_ln_b, sg_w, sg_b, w_branch_a, lam_q1, lam_k1, lam_q2, lam_k2,
           subln_g, w_branch_b, w_out, ln1_g, ln1_b, w_group, b_group, w_expert, b_expert,
           w_gate, w_up, w_down, ln2_g, ln2_b):
    B, S, D = h.shape
    T = B * S
    row = lambda a: a.reshape(1, -1)
    col = lambda a: a.reshape(-1, 1)

    wk = w_in[:, OFF_K:OFF_VAL].astype(BF16)
    wqt = w_in[:, OFF_Q:OFF_K].T.astype(BF16)
    wvt = w_in[:, OFF_VAL:OFF_GA].T.astype(BF16)
    k, qt, vt = _qkv_proj(h, wk, row(b_in[OFF_K:OFF_VAL]), wqt, col(b_in[OFF_Q:OFF_K]),
                          wvt, col(b_in[OFF_VAL:OFF_GA]))
    pos, qfeat, corr = _attn_bias_tables(S)
    lambda_init = 0.8 - 0.6 * math.exp(-0.3 * layer)
    lamv = jnp.stack([lam_q1, lam_k1, lam_q2, lam_k2]).astype(F32)
    attn = _diff_attention(qt, k, pos, vt, qfeat, corr, lamv, col(subln_g), lambda_init)

    sgb = jnp.repeat(sg_b.reshape(GMLP_GROUPS // 2, 2, GMLP_BLOCK), GMLP_GROUP_DIM, axis=1)
    sgb = jnp.transpose(sgb, (0, 2, 1))
    h1 = _mixer(h.reshape(T, D), attn.reshape(T, DIFF_WIDTH),
                w_in[:, OFF_U:OFF_Q].astype(BF16), row(b_in[OFF_U:OFF_Q]),
                w_in[:, OFF_GA:IN_WIDTH].astype(BF16), row(b_in[OFF_GA:IN_WIDTH]),
                row(sg_ln_g), row(sg_ln_b), sg_w, sgb,
                w_branch_a.astype(BF16), w_branch_b.astype(BF16), w_out.astype(BF16),
                row(ln1_g), row(ln1_b))

    wr = jnp.zeros((LANES, D), F32)
    wr = wr.at[0:N_GROUPS].set(w_group.T).at[8:8 + N_EXPERTS].set(w_expert.T)
    wr_hi = wr.astype(BF16)
    wr_lo = (wr - wr_hi.astype(F32)).astype(BF16)
    br = jnp.zeros((LANES, 1), F32)
    br = br.at[0:N_GROUPS, 0].set(b_group.astype(F32)).at[8:8 + N_EXPERTS, 0].set(b_expert.astype(F32))
    eid, wts = _router(h1, wr_hi, wr_lo, br)

    tri = jnp.asarray(np.triu(np.ones((ROUTE_TM, ROUTE_TM), np.float32), k=1), BF16)
    rank, counts = _ranks(eid, tri)

    counts = counts[:, 0]
    padded = (counts + EXPERT_BLOCK - 1) // EXPERT_BLOCK * EXPERT_BLOCK
    pad_end = jnp.cumsum(padded)
    pad_start = (pad_end - padded).astype(jnp.int32)
    P = T * EXPERT_TOPK + N_EXPERTS * EXPERT_BLOCK
    n_blocks = P // EXPERT_BLOCK
    block_start = jnp.arange(n_blocks, dtype=jnp.int32) * EXPERT_BLOCK
    block_e = jnp.minimum(jnp.sum(pad_end[None, :] <= block_start[:, None], axis=1), N_EXPERTS - 1).astype(jnp.int32)
    n_used = (pad_end[-1:] // EXPERT_BLOCK).astype(jnp.int32)

    hot = eid[:, :, None] == jnp.arange(N_EXPERTS, dtype=jnp.int32)
    dest = jnp.sum(jnp.where(hot, pad_start, 0), axis=-1) + rank
    xs = _dispatch(counts, pad_start, dest, h1, P)
    ys = _experts(block_e, n_used, xs, w_gate, w_up, w_down)
    out = _combine(dest, h1, wts.T, row(ln2_g), row(ln2_b), ys)
    return out.reshape(B, S, D)


def kernel(x, w_in, b_in, sg_ln_g, sg_ln_b, sg_w, sg_b, w_branch_a, lam_q1, lam_k1, lam_q2, lam_k2, subln_g,
           w_branch_b, w_out, ln1_g, ln1_b, w_group, b_group, w_expert, b_expert, w_gate, w_up, w_down,
           ln2_g, ln2_b):
    h = x
    for layer in range(DEPTH):
        h = _layer(h, layer, w_in[layer], b_in[layer], sg_ln_g[layer], sg_ln_b[layer], sg_w[layer],
                   sg_b[layer], w_branch_a[layer], lam_q1[layer], lam_k1[layer], lam_q2[layer],
                   lam_k2[layer], subln_g[layer], w_branch_b[layer], w_out[layer], ln1_g[layer],
                   ln1_b[layer], w_group[layer], b_group[layer], w_expert[layer], b_expert[layer],
                   w_gate[layer], w_up[layer], w_down[layer], ln2_g[layer], ln2_b[layer])
    return h
```

```python
import functools
import math

import jax
import jax.numpy as jnp
import numpy as np
from jax import lax
from jax.experimental import pallas as pl
from jax.experimental.pallas import tpu as pltpu

D_MODEL = 1024
DEPTH = 1
CHUNK = 64
GMLP_GROUPS = 8
GMLP_GROUP_DIM = 64
GMLP_WIDTH = GMLP_GROUPS * GMLP_GROUP_DIM
GMLP_BLOCK = 128
DIFF_HEADS = 8
DIFF_QK_DIM = 64
DIFF_V_DIM = 2 * DIFF_QK_DIM
DIFF_QK_WIDTH = DIFF_HEADS * 2 * DIFF_QK_DIM
DIFF_WIDTH = DIFF_HEADS * DIFF_V_DIM
N_GROUPS = 4
EXPERTS_PER_GROUP = 8
N_EXPERTS = N_GROUPS * EXPERTS_PER_GROUP
EXPERT_TOPK = 2
EXPERT_HIDDEN = 512
DISPATCH_BLOCK = 256
ALPHA = (2.0 * DEPTH) ** 0.25
LN_EPS = 1e-5
RMS_EPS = 1e-5
OFF_U = 0
OFF_V = OFF_U + GMLP_WIDTH
OFF_Q = OFF_V + GMLP_WIDTH
OFF_K = OFF_Q + DIFF_QK_WIDTH
OFF_VAL = OFF_K + DIFF_QK_WIDTH
OFF_GA = OFF_VAL + DIFF_WIDTH
OFF_GB = OFF_GA + D_MODEL
IN_WIDTH = OFF_GB + D_MODEL

LANES = 128
HEAD_W = 2 * DIFF_QK_DIM
MASKED = -1e30
VMEM_LIMIT = 56 * 1024 * 1024

PROJ_TM = 512
ATT_T = 256
ATT_TK = 2 * ATT_T
ATT_TQ = ATT_TK
ATT_HB = 2
LOG2E = math.log2(math.e)
VT_ROWS = HEAD_W + 16
MIX_TM = 512
ROUTE_TM = 512
ROW_TM = 256
EXPERT_BLOCK = 2 * DISPATCH_BLOCK
ROW_UNROLL = 8

BF16 = jnp.bfloat16
F32 = jnp.float32


def _cparams(sem):
    return pltpu.CompilerParams(dimension_semantics=sem, vmem_limit_bytes=VMEM_LIMIT)


def _dot(a, b):
    return jnp.dot(a, b, preferred_element_type=F32)


def _dot_nt(a, b):
    return lax.dot_general(a, b, (((1,), (1,)), ((), ())), preferred_element_type=F32)


def _gelu(x):
    return 0.5 * x * (1.0 + lax.erf(x * (2.0 ** -0.5)))


def _sigmoid(x):
    return 1.0 / (1.0 + jnp.exp(-x))


def _layer_norm(x, g, b):
    mu = jnp.mean(x, axis=-1, keepdims=True)
    xc = x - mu
    var = jnp.mean(xc * xc, axis=-1, keepdims=True)
    return xc * lax.rsqrt(var + LN_EPS) * g + b


def _qkv_kernel(x_ref, wk_ref, bk_ref, wqt_ref, bq_ref, wvt_ref, bv_ref, k_ref, qt_ref, vt_ref):
    xb = x_ref[...].astype(BF16)
    k_ref[...] = (_dot(xb, wk_ref[...]) + bk_ref[...]).astype(BF16)
    qt = (_dot_nt(wqt_ref[...], xb) + bq_ref[...]) * (DIFF_QK_DIM ** -0.5 * LOG2E)
    qt_ref[...] = qt.astype(BF16)
    vt = (_dot_nt(wvt_ref[...], xb) + bv_ref[...]).astype(BF16)
    ones_row = (lax.broadcasted_iota(jnp.int32, (VT_ROWS - HEAD_W, ATT_TK), 0) == 0).astype(BF16)
    for hh in range(DIFF_HEADS):
        for j in range(PROJ_TM // ATT_TK):
            vt_ref[hh, j, :HEAD_W] = vt[hh * HEAD_W:(hh + 1) * HEAD_W, j * ATT_TK:(j + 1) * ATT_TK]
            vt_ref[hh, j, HEAD_W:] = ones_row


def _qkv_proj(x, wk, bk, wqt, bq, wvt, bv):
    B, S, D = x.shape
    nk = S // ATT_TK
    jt = PROJ_TM // ATT_TK
    const2 = lambda b, i: (0, 0)
    return pl.pallas_call(
        _qkv_kernel,
        grid=(B, S // PROJ_TM),
        in_specs=[
            pl.BlockSpec((None, PROJ_TM, D), lambda b, i: (b, i, 0)),
            pl.BlockSpec((D, DIFF_QK_WIDTH), const2),
            pl.BlockSpec((1, DIFF_QK_WIDTH), const2),
            pl.BlockSpec((DIFF_QK_WIDTH, D), const2),
            pl.BlockSpec((DIFF_QK_WIDTH, 1), const2),
            pl.BlockSpec((DIFF_WIDTH, D), const2),
            pl.BlockSpec((DIFF_WIDTH, 1), const2),
        ],
        out_specs=[
            pl.BlockSpec((None, PROJ_TM, DIFF_QK_WIDTH), lambda b, i: (b, i, 0)),
            pl.BlockSpec((None, DIFF_QK_WIDTH, PROJ_TM), lambda b, i: (b, 0, i)),
            pl.BlockSpec((None, DIFF_HEADS, jt, VT_ROWS, ATT_TK), lambda b, i: (b, 0, i, 0, 0)),
        ],
        out_shape=[
            jax.ShapeDtypeStruct((B, S, DIFF_QK_WIDTH), BF16),
            jax.ShapeDtypeStruct((B, DIFF_QK_WIDTH, S), BF16),
            jax.ShapeDtypeStruct((B, DIFF_HEADS, nk, VT_ROWS, ATT_TK), BF16),
        ],
        compiler_params=_cparams(("arbitrary", "arbitrary")),
        name="qkv_proj",
    )(x, wk, bk, wqt, bq, wvt, bv)


def _attn_kernel(qt_ref, k_ref, pos_ref, vt_ref, qf_ref, corr_ref, lamv_ref, g_ref, o_ref,
                 q12_ref, sa_ref, sb_ref, ta_ref, tb_ref, m_ref, acc_ref, *, lambda_init):
    qi = pl.program_id(2)
    T = ATT_T
    TQ = ATT_TQ
    n = qi + 1

    row = lax.broadcasted_iota(jnp.int32, (HEAD_W, TQ), 0)
    zero = jnp.zeros((HEAD_W, TQ), BF16)
    for e in range(ATT_HB):
        q = qt_ref[e * HEAD_W:(e + 1) * HEAD_W, :]
        q12_ref[e, :HEAD_W, :TQ] = jnp.where(row < DIFF_QK_DIM, q, zero)
        q12_ref[e, :HEAD_W, TQ:] = jnp.where(row >= DIFF_QK_DIM, q, zero)
        q12_ref[e, HEAD_W:, :] = jnp.broadcast_to(qf_ref[e], (LANES, 2 * TQ))
    acc_ref[...] = jnp.zeros_like(acc_ref)
    m_ref[...] = jnp.full(m_ref.shape, MASKED, F32)

    def scores(g, s_ref, t_ref, diagonal):
        rows = pl.ds(pl.multiple_of(g * 2 * T, 2 * T), 2 * T)
        for e in range(ATT_HB):
            kx = jnp.concatenate([k_ref[rows, e * HEAD_W:(e + 1) * HEAD_W], pos_ref[rows, :]], axis=1)
            t = _dot(kx, q12_ref[e])
            if diagonal:
                t = t + corr_ref[e]
            s_ref[e] = t
            t_ref[e] = jnp.max(t, axis=0, keepdims=True)

    def accumulate(g, s_ref, t_ref):
        for e in range(ATT_HB):
            m = m_ref[e]
            m_new = jnp.maximum(m, t_ref[e])
            m_ref[e] = m_new
            p = jnp.exp2(s_ref[e] - m_new).astype(BF16)
            acc_ref[e] = jnp.exp2(m - m_new) * acc_ref[e] + _dot(vt_ref[e, g], p)

    def finish():
        lv = lamv_ref[...]
        lam = (jnp.exp(jnp.sum(lv[0:1] * lv[1:2], axis=1, keepdims=True))
               - jnp.exp(jnp.sum(lv[2:3] * lv[3:4], axis=1, keepdims=True)) + lambda_init)
        for e in range(ATT_HB):
            o12 = acc_ref[e, :HEAD_W] * (1.0 / acc_ref[e, HEAD_W:HEAD_W + 1])
            o = o12[:, :TQ] - lam * o12[:, TQ:]
            o = o * lax.rsqrt(jnp.mean(o * o, axis=0, keepdims=True) + RMS_EPS)
            o = o * g_ref[...] * (1.0 - lambda_init)
            o_ref[:, e * HEAD_W:(e + 1) * HEAD_W] = o.T.astype(BF16)

    @pl.when(n == 1)
    def _():
        scores(0, sa_ref, ta_ref, True)
        accumulate(0, sa_ref, ta_ref)
        finish()

    @pl.when(n > 1)
    def _():
        scores(0, sa_ref, ta_ref, False)

    def pair(i, _):
        g = 2 * i
        scores(g + 1, sb_ref, tb_ref, False)
        accumulate(g, sa_ref, ta_ref)
        scores(g + 2, sa_ref, ta_ref, False)
        accumulate(g + 1, sb_ref, tb_ref)
        return 0

    lax.fori_loop(0, jnp.maximum(n - 2, 0) // 2, pair, 0)

    @pl.when((n > 1) & (n % 2 == 0))
    def _():
        scores(n - 1, sb_ref, tb_ref, True)
        accumulate(n - 2, sa_ref, ta_ref)
        accumulate(n - 1, sb_ref, tb_ref)
        finish()

    @pl.when((n > 1) & (n % 2 == 1))
    def _():
        scores(n - 2, sb_ref, tb_ref, False)
        accumulate(n - 3, sa_ref, ta_ref)
        scores(n - 1, sa_ref, ta_ref, True)
        accumulate(n - 2, sb_ref, tb_ref)
        accumulate(n - 1, sa_ref, ta_ref)
        finish()


def _diff_attention(qt, k, pos, vt, qfeat, corr, lamv, gcol, lambda_init):
    B, S, _ = k.shape
    nq = S // ATT_TQ
    gw = ATT_HB * HEAD_W
    lanes = 2 * ATT_TQ
    return pl.pallas_call(
        functools.partial(_attn_kernel, lambda_init=lambda_init),
        grid=(B, DIFF_HEADS // ATT_HB, nq),
        in_specs=[
            pl.BlockSpec((None, gw, ATT_TQ), lambda b, h, i: (b, h, i)),
            pl.BlockSpec((None, S, gw), lambda b, h, i: (b, 0, h)),
            pl.BlockSpec((S, LANES), lambda b, h, i: (0, 0)),
            pl.BlockSpec((None, ATT_HB, S // ATT_TK, VT_ROWS, ATT_TK), lambda b, h, i: (b, h, 0, 0, 0)),
            pl.BlockSpec((ATT_HB, LANES, 1), lambda b, h, i: (h, 0, 0)),
            pl.BlockSpec((ATT_HB, ATT_TK, lanes), lambda b, h, i: (h, 0, 0)),
            pl.BlockSpec((4, DIFF_QK_DIM), lambda b, h, i: (0, 0)),
            pl.BlockSpec((HEAD_W, 1), lambda b, h, i: (0, 0)),
        ],
        out_specs=pl.BlockSpec((None, ATT_TQ, gw), lambda b, h, i: (b, i, h)),
        scratch_shapes=[
            pltpu.VMEM((ATT_HB, HEAD_W + LANES, lanes), BF16),
            pltpu.VMEM((ATT_HB, ATT_TK, lanes), F32),
            pltpu.VMEM((ATT_HB, ATT_TK, lanes), F32),
            pltpu.VMEM((ATT_HB, 1, lanes), F32),
            pltpu.VMEM((ATT_HB, 1, lanes), F32),
            pltpu.VMEM((ATT_HB, 1, lanes), F32),
            pltpu.VMEM((ATT_HB, VT_ROWS, lanes), F32),
        ],
        out_shape=jax.ShapeDtypeStruct((B, S, DIFF_WIDTH), BF16),
        compiler_params=_cparams(("arbitrary", "arbitrary", "arbitrary")),
        name="diff_attn",
    )(qt, k, pos, vt, qfeat, corr, lamv, gcol)


def _mixer_kernel(x_ref, attn_ref, wuv_ref, buv_ref, wg_ref, bg_ref, lng_ref, lnb_ref, sgw_ref, sgb_ref,
                  wa_ref, wb_ref, wo_ref, g1_ref, b1_ref, h_ref):
    x = x_ref[...]
    xb = x.astype(BF16)
    uv = _gelu(_dot(xb, wuv_ref[...]) + buv_ref[...])
    u = uv[:, :GMLP_WIDTH]
    v = _layer_norm(uv[:, GMLP_WIDTH:], lng_ref[...], lnb_ref[...]).astype(BF16)

    ti = lax.broadcasted_iota(jnp.int32, (GMLP_BLOCK, GMLP_BLOCK), 0) // CHUNK
    si = lax.broadcasted_iota(jnp.int32, (GMLP_BLOCK, GMLP_BLOCK), 1) // CHUNK
    causal = ti >= si
    w = [jnp.where(causal, sgw_ref[g], 0.0).astype(BF16) for g in range(GMLP_GROUPS)]
    lane = lax.broadcasted_iota(jnp.int32, (GMLP_BLOCK, LANES), 1)
    lo = lane < GMLP_GROUP_DIM
    vzero = jnp.zeros((GMLP_BLOCK, LANES), BF16)

    blocks = []
    for n in range(MIX_TM // GMLP_BLOCK):
        slabs = []
        for p in range(GMLP_WIDTH // LANES):
            vs = v[n * GMLP_BLOCK:(n + 1) * GMLP_BLOCK, p * LANES:(p + 1) * LANES]
            mixed = (_dot(w[2 * p], jnp.where(lo, vs, vzero))
                     + _dot(w[2 * p + 1], jnp.where(lo, vzero, vs)) + sgb_ref[p])
            slabs.append(mixed)
        blocks.append(jnp.concatenate(slabs, axis=1))
    mixed = jnp.concatenate(blocks, axis=0)

    y_a = _dot((u * mixed).astype(BF16), wa_ref[...])
    y_b = _dot(attn_ref[...], wb_ref[...])
    gates = _sigmoid(_dot(xb, wg_ref[...]) + bg_ref[...])
    z = gates[:, :D_MODEL] * y_a + gates[:, D_MODEL:] * y_b
    mix = _dot(z.astype(BF16), wo_ref[...])
    h_ref[...] = _layer_norm(ALPHA * x + mix, g1_ref[...], b1_ref[...])


def _mixer(x2, attn2, wuv, buv, wg, bg, lng, lnb, sgw, sgb, wa, wb, wo, g1, b1):
    T, D = x2.shape
    c2 = lambda i: (0, 0)
    c3 = lambda i: (0, 0, 0)
    full = lambda a: pl.BlockSpec(a.shape, c2 if a.ndim == 2 else c3)
    return pl.pallas_call(
        _mixer_kernel,
        grid=(T // MIX_TM,),
        in_specs=[pl.BlockSpec((MIX_TM, D), lambda i: (i, 0)),
                  pl.BlockSpec((MIX_TM, DIFF_WIDTH), lambda i: (i, 0))]
                 + [full(a) for a in (wuv, buv, wg, bg, lng, lnb, sgw, sgb, wa, wb, wo, g1, b1)],
        out_specs=pl.BlockSpec((MIX_TM, D), lambda i: (i, 0)),
        out_shape=jax.ShapeDtypeStruct((T, D), F32),
        compiler_params=_cparams(("arbitrary",)),
        name="mixer",
    )(x2, attn2, wuv, buv, wg, bg, lng, lnb, sgw, sgb, wa, wb, wo, g1, b1)


def _router_kernel(h_ref, wr_hi_ref, wr_lo_ref, br_ref, tri_ref, eid_ref, wts_ref, rank_ref, cnt_ref, run_ref):
    hf = h_ref[...]
    hi = hf.astype(BF16)
    lo = (hf - hi.astype(F32)).astype(BF16)
    logits = (_dot_nt(wr_hi_ref[...], hi) + _dot_nt(wr_lo_ref[...], hi)
              + _dot_nt(wr_hi_ref[...], lo)) + br_ref[...]
    gidx = lax.broadcasted_iota(jnp.int32, (8, ROUTE_TM), 0)
    gl = jnp.where(gidx < N_GROUPS, logits[0:8], -jnp.inf)
    gmax = jnp.max(gl, axis=0, keepdims=True)
    g_sel = jnp.min(jnp.where(gl == gmax, gidx, N_GROUPS), axis=0, keepdims=True)
    g_w = 1.0 / jnp.sum(jnp.exp(gl - gmax), axis=0, keepdims=True)

    el = logits[8:8 + EXPERTS_PER_GROUP]
    for g in range(1, N_GROUPS):
        el = jnp.where(g_sel == g, logits[8 + g * EXPERTS_PER_GROUP:8 + (g + 1) * EXPERTS_PER_GROUP], el)
    eidx = lax.broadcasted_iota(jnp.int32, el.shape, 0)
    v1 = jnp.max(el, axis=0, keepdims=True)
    i1 = jnp.min(jnp.where(el == v1, eidx, EXPERTS_PER_GROUP), axis=0, keepdims=True)
    rest = jnp.where(eidx == i1, -jnp.inf, el)
    v2 = jnp.max(rest, axis=0, keepdims=True)
    i2 = jnp.min(jnp.where(rest == v2, eidx, EXPERTS_PER_GROUP), axis=0, keepdims=True)
    e21 = jnp.exp(v2 - v1)
    w1 = g_w / (1.0 + e21)
    w2 = g_w * e21 / (1.0 + e21)
    base = g_sel * EXPERTS_PER_GROUP
    eids = (base + i1, base + i2)
    eid_ref[0:1, :] = eids[0]
    eid_ref[1:2, :] = eids[1]
    wts_ref[0:1, :] = w1
    wts_ref[1:2, :] = w2

    @pl.when(pl.program_id(0) == 0)
    def _():
        run_ref[...] = jnp.zeros_like(run_ref)

    eiota = lax.broadcasted_iota(jnp.int32, (N_EXPERTS, ROUTE_TM), 0)
    run = run_ref[...]
    for c in range(EXPERT_TOPK):
        hot = eiota == eids[c]
        hotf = hot.astype(F32)
        before = _dot(hotf.astype(BF16), tri_ref[...])
        rank = jnp.sum(jnp.where(hot, run + before, 0.0), axis=0, keepdims=True)
        rank_ref[c:c + 1, :] = rank.astype(jnp.int32)
        run = run + jnp.sum(hotf, axis=1, keepdims=True)
    run_ref[...] = run
    cnt_ref[...] = run.astype(jnp.int32)


def _router(h1, wr_hi, wr_lo, br, tri):
    T, D = h1.shape
    c2 = lambda i: (0, 0)
    per_tile = pl.BlockSpec((EXPERT_TOPK, ROUTE_TM), lambda i: (0, i))
    return pl.pallas_call(
        _router_kernel,
        grid=(T // ROUTE_TM,),
        in_specs=[pl.BlockSpec((ROUTE_TM, D), lambda i: (i, 0)),
                  pl.BlockSpec(wr_hi.shape, c2), pl.BlockSpec(wr_lo.shape, c2), pl.BlockSpec(br.shape, c2),
                  pl.BlockSpec(tri.shape, c2)],
        out_specs=[per_tile, per_tile, per_tile, pl.BlockSpec((N_EXPERTS, 1), c2)],
        out_shape=[jax.ShapeDtypeStruct((EXPERT_TOPK, T), jnp.int32),
                   jax.ShapeDtypeStruct((EXPERT_TOPK, T), F32),
                   jax.ShapeDtypeStruct((EXPERT_TOPK, T), jnp.int32),
                   jax.ShapeDtypeStruct((N_EXPERTS, 1), jnp.int32)],
        scratch_shapes=[pltpu.VMEM((N_EXPERTS, 1), F32)],
        compiler_params=_cparams(("arbitrary",)),
        name="router",
    )(h1, wr_hi, wr_lo, br, tri)


ROW_SUB = D_MODEL // LANES


def _dispatch_kernel(cnt_ref, pad_ref, d0_ref, d1_ref, h_ref, xs_ref, rows_ref, zero_ref, sem, zsem):
    dest_refs = (d0_ref, d1_ref)

    @pl.when(pl.program_id(0) == 0)
    def _():
        zero_ref[...] = jnp.zeros_like(zero_ref)

        def zero_row(r):
            return pltpu.make_async_copy(zero_ref.at[0], xs_ref.at[r], zsem)

        def zero_block(b):
            return pltpu.make_async_copy(zero_ref, xs_ref.at[pl.ds(b * EXPERT_BLOCK, EXPERT_BLOCK)], zsem)

        def per_expert(e, _):
            lo = pad_ref[e] + cnt_ref[e]
            hi = pad_ref[e] + (cnt_ref[e] + EXPERT_BLOCK - 1) // EXPERT_BLOCK * EXPERT_BLOCK

            def start(r, c):
                zero_row(r).start()
                return c

            def wait(r, c):
                zero_row(r).wait()
                return c

            lax.fori_loop(lo, hi, start, 0)
            lax.fori_loop(lo, hi, wait, 0)
            return hi // EXPERT_BLOCK

        used = lax.fori_loop(0, N_EXPERTS, per_expert, 0)

        def start_block(b, c):
            zero_block(b).start()
            return c

        def wait_block(b, c):
            zero_block(b).wait()
            return c

        lax.fori_loop(used, xs_ref.shape[0] // EXPERT_BLOCK, start_block, 0)
        lax.fori_loop(used, xs_ref.shape[0] // EXPERT_BLOCK, wait_block, 0)

    rows_ref[...] = h_ref[...].reshape(ROW_TM, ROW_SUB, LANES)

    def row_copy(c, t, dest):
        return pltpu.make_async_copy(rows_ref.at[t], xs_ref.at[dest], sem)

    def issue(t, _):
        for c in range(EXPERT_TOPK):
            row_copy(c, t, dest_refs[c][t]).start(priority=c)
        return 0

    lax.fori_loop(0, ROW_TM, issue, 0, unroll=ROW_UNROLL)

    def wait(t, _):
        for c in range(EXPERT_TOPK):
            row_copy(c, t, 0).wait()
        return 0

    lax.fori_loop(0, ROW_TM, wait, 0, unroll=ROW_UNROLL)


def _dispatch(counts, pad_start, dest, h1, n_rows):
    T, D = h1.shape
    grid_spec = pltpu.PrefetchScalarGridSpec(
        num_scalar_prefetch=2,
        grid=(T // ROW_TM,),
        in_specs=[
            pl.BlockSpec((ROW_TM,), lambda i, c, p: (i,), memory_space=pltpu.SMEM),
            pl.BlockSpec((ROW_TM,), lambda i, c, p: (i,), memory_space=pltpu.SMEM),
            pl.BlockSpec((ROW_TM, D), lambda i, c, p: (i, 0)),
        ],
        out_specs=pl.BlockSpec(memory_space=pl.ANY),
        scratch_shapes=[pltpu.VMEM((ROW_TM, ROW_SUB, LANES), F32),
                        pltpu.VMEM((EXPERT_BLOCK, ROW_SUB, LANES), F32),
                        pltpu.SemaphoreType.DMA(()), pltpu.SemaphoreType.DMA(())],
    )
    return pl.pallas_call(
        _dispatch_kernel,
        grid_spec=grid_spec,
        out_shape=jax.ShapeDtypeStruct((n_rows, ROW_SUB, LANES), F32),
        compiler_params=_cparams(("arbitrary",)),
        name="dispatch",
    )(counts, pad_start, dest[0], dest[1], h1)


def _expert_kernel(be_ref, nb_ref, xs_ref, wg_ref, wu_ref, wd_ref, o_ref):
    i = pl.program_id(0)

    @pl.when(i < nb_ref[0])
    def _():
        xb = xs_ref[...].reshape(EXPERT_BLOCK, D_MODEL).astype(BF16)
        gate = _dot(xb, wg_ref[...].astype(BF16))
        up = _dot(xb, wu_ref[...].astype(BF16))
        hb = (gate * _sigmoid(gate) * up).astype(BF16)
        o_ref[...] = _dot(hb, wd_ref[...].astype(BF16)).reshape(EXPERT_BLOCK, ROW_SUB, LANES)

    @pl.when(i >= nb_ref[0])
    def _():
        o_ref[...] = jnp.zeros_like(o_ref)


def _experts(block_e, n_used, xs, wg, wu, wd):
    P = xs.shape[0]
    D = D_MODEL
    rows = pl.BlockSpec((EXPERT_BLOCK, ROW_SUB, LANES), lambda i, be, nb: (i, 0, 0))
    grid_spec = pltpu.PrefetchScalarGridSpec(
        num_scalar_prefetch=2,
        grid=(P // EXPERT_BLOCK,),
        in_specs=[
            rows,
            pl.BlockSpec((None, D, EXPERT_HIDDEN), lambda i, be, nb: (be[i], 0, 0)),
            pl.BlockSpec((None, D, EXPERT_HIDDEN), lambda i, be, nb: (be[i], 0, 0)),
            pl.BlockSpec((None, EXPERT_HIDDEN, D), lambda i, be, nb: (be[i], 0, 0)),
        ],
        out_specs=rows,
    )
    return pl.pallas_call(
        _expert_kernel,
        grid_spec=grid_spec,
        out_shape=jax.ShapeDtypeStruct(xs.shape, F32),
        compiler_params=_cparams(("arbitrary",)),
        name="experts",
    )(block_e, n_used, xs, wg, wu, wd)


def _combine_kernel(d0_ref, d1_ref, n0_ref, n1_ref, h_ref, wts_ref, g2_ref, b2_ref, ys_ref, o_ref,
                    rows_ref, sems):
    i = pl.program_id(0)
    n = pl.num_programs(0)
    slot = i % 2

    def gather(dest_refs, buf):
        def issue(t, _):
            for c in range(EXPERT_TOPK):
                pltpu.make_async_copy(ys_ref.at[dest_refs[c][t]], rows_ref.at[buf, c, t],
                                      sems.at[buf]).start(priority=c)
            return 0

        lax.fori_loop(0, ROW_TM, issue, 0, unroll=ROW_UNROLL)

    @pl.when(i == 0)
    def _():
        gather((d0_ref, d1_ref), 0)

    @pl.when(i + 1 < n)
    def _():
        gather((n0_ref, n1_ref), 1 - slot)

    def drain(t, _):
        for c in range(EXPERT_TOPK):
            pltpu.make_async_copy(ys_ref.at[0], rows_ref.at[slot, c, t], sems.at[slot]).wait()
        return 0

    lax.fori_loop(0, ROW_TM, drain, 0, unroll=ROW_UNROLL)
    w = wts_ref[...]
    ffn = (w[:, 0:1] * rows_ref[slot, 0].reshape(ROW_TM, D_MODEL)
           + w[:, 1:2] * rows_ref[slot, 1].reshape(ROW_TM, D_MODEL))
    o_ref[...] = _layer_norm(ALPHA * h_ref[...] + ffn, g2_ref[...], b2_ref[...])


def _combine(dest, h1, wts_t, g2, b2, ys):
    T, D = h1.shape
    n = T // ROW_TM
    cur = pl.BlockSpec((ROW_TM,), lambda i: (i,), memory_space=pltpu.SMEM)
    nxt = pl.BlockSpec((ROW_TM,), lambda i: (jnp.minimum(i + 1, n - 1),), memory_space=pltpu.SMEM)
    return pl.pallas_call(
        _combine_kernel,
        grid=(n,),
        in_specs=[
            cur, cur, nxt, nxt,
            pl.BlockSpec((ROW_TM, D), lambda i: (i, 0)),
            pl.BlockSpec((ROW_TM, EXPERT_TOPK), lambda i: (i, 0)),
            pl.BlockSpec((1, D), lambda i: (0, 0)),
            pl.BlockSpec((1, D), lambda i: (0, 0)),
            pl.BlockSpec(memory_space=pl.ANY),
        ],
        out_specs=pl.BlockSpec((ROW_TM, D), lambda i: (i, 0)),
        scratch_shapes=[pltpu.VMEM((2, EXPERT_TOPK, ROW_TM, ROW_SUB, LANES), F32),
                        pltpu.SemaphoreType.DMA((2,))],
        out_shape=jax.ShapeDtypeStruct((T, D), F32),
        compiler_params=_cparams(("arbitrary",)),
        name="combine",
    )(dest[0], dest[1], dest[0], dest[1], h1, wts_t, g2, b2, ys)


def _attn_bias_tables(S):
    c = jnp.asarray((2.0 ** (-8.0 * np.arange(1, DIFF_HEADS + 1, dtype=np.float64) / DIFF_HEADS)) * LOG2E, F32)
    c_hi = c.astype(BF16)
    c_mid = (c - c_hi.astype(F32)).astype(BF16)
    c_lo = (c - c_hi.astype(F32) - c_mid.astype(F32)).astype(BF16)
    pieces = jnp.stack([c_hi, c_mid, c_lo], axis=1)
    qfeat = jnp.concatenate([pieces * LANES, pieces, jnp.zeros((DIFF_HEADS, LANES - 6), BF16)], axis=1)
    j = np.arange(S)
    pos = np.zeros((S, LANES), np.float32)
    pos[:, 0:3] = (j // LANES)[:, None]
    pos[:, 3:6] = (j % LANES)[:, None]
    kj = np.arange(ATT_TK, dtype=np.float64)[:, None]
    qi = np.tile(np.arange(ATT_TQ, dtype=np.float64), 2)[None, :]
    visible = (kj // CHUNK) <= (qi // CHUNK)
    after = 2.0 * np.minimum(qi - kj, 0.0)
    corr = jnp.where(visible[None], c[:, None, None] * jnp.asarray(after, F32)[None], MASKED)
    return jnp.asarray(pos, BF16), qfeat[:, :, None], corr.astype(F32)


def _layer(h, layer, w_in, b_in, sg_ln_g, sg_ln_b, sg_w, sg_b, w_branch_a, lam_q1, lam_k1, lam_q2, lam_k2,
           subln_g, w_branch_b, w_out, ln1_g, ln1_b, w_group, b_group, w_expert, b_expert,
           w_gate, w_up, w_down, ln2_g, ln2_b):
    B, S, D = h.shape
    T = B * S
    row = lambda a: a.reshape(1, -1)
    col = lambda a: a.reshape(-1, 1)

    wk = w_in[:, OFF_K:OFF_VAL].astype(BF16)
    wqt = w_in[:, OFF_Q:OFF_K].T.astype(BF16)
    wvt = w_in[:, OFF_VAL:OFF_GA].T.astype(BF16)
    k, qt, vt = _qkv_proj(h, wk, row(b_in[OFF_K:OFF_VAL]), wqt, col(b_in[OFF_Q:OFF_K]),
                          wvt, col(b_in[OFF_VAL:OFF_GA]))
    pos, qfeat, corr = _attn_bias_tables(S)
    lambda_init = 0.8 - 0.6 * math.exp(-0.3 * layer)
    lamv = jnp.stack([lam_q1, lam_k1, lam_q2, lam_k2]).astype(F32)
    attn = _diff_attention(qt, k, pos, vt, qfeat, corr, lamv, col(subln_g), lambda_init)

    sgb = jnp.repeat(sg_b.reshape(GMLP_GROUPS // 2, 2, GMLP_BLOCK), GMLP_GROUP_DIM, axis=1)
    sgb = jnp.transpose(sgb, (0, 2, 1))
    h1 = _mixer(h.reshape(T, D), attn.reshape(T, DIFF_WIDTH),
                w_in[:, OFF_U:OFF_Q].astype(BF16), row(b_in[OFF_U:OFF_Q]),
                w_in[:, OFF_GA:IN_WIDTH].astype(BF16), row(b_in[OFF_GA:IN_WIDTH]),
                row(sg_ln_g), row(sg_ln_b), sg_w, sgb,
                w_branch_a.astype(BF16), w_branch_b.astype(BF16), w_out.astype(BF16),
                row(ln1_g), row(ln1_b))

    wr = jnp.zeros((LANES, D), F32)
    wr = wr.at[0:N_GROUPS].set(w_group.T).at[8:8 + N_EXPERTS].set(w_expert.T)
    wr_hi = wr.astype(BF16)
    wr_lo = (wr - wr_hi.astype(F32)).astype(BF16)
    br = jnp.zeros((LANES, 1), F32)
    br = br.at[0:N_GROUPS, 0].set(b_group.astype(F32)).at[8:8 + N_EXPERTS, 0].set(b_expert.astype(F32))
    tri = jnp.asarray(np.triu(np.ones((ROUTE_TM, ROUTE_TM), np.float32), k=1), BF16)
    eid, wts, rank, counts = _router(h1, wr_hi, wr_lo, br, tri)

    counts = counts[:, 0]
    padded = (counts + EXPERT_BLOCK - 1) // EXPERT_BLOCK * EXPERT_BLOCK
    pad_end = jnp.cumsum(padded)
    pad_start = (pad_end - padded).astype(jnp.int32)
    P = T * EXPERT_TOPK + N_EXPERTS * EXPERT_BLOCK
    n_blocks = P // EXPERT_BLOCK
    block_start = jnp.arange(n_blocks, dtype=jnp.int32) * EXPERT_BLOCK
    block_e = jnp.minimum(jnp.sum(pad_end[None, :] <= block_start[:, None], axis=1), N_EXPERTS - 1).astype(jnp.int32)
    n_used = (pad_end[-1:] // EXPERT_BLOCK).astype(jnp.int32)

    hot = eid[:, :, None] == jnp.arange(N_EXPERTS, dtype=jnp.int32)
    dest = jnp.sum(jnp.where(hot, pad_start, 0), axis=-1) + rank
    xs = _dispatch(counts, pad_start, dest, h1, P)
    ys = _experts(block_e, n_used, xs, w_gate, w_up, w_down)
    out = _combine(dest, h1, wts.T, row(ln2_g), row(ln2_b), ys)
    return out.reshape(B, S, D)


def kernel(x, w_in, b_in, sg_ln_g, sg_ln_b, sg_w, sg_b, w_branch_a, lam_q1, lam_k1, lam_q2, lam_k2, subln_g,
           w_branch_b, w_out, ln1_g, ln1_b, w_group, b_group, w_expert, b_expert, w_gate, w_up, w_down,
           ln2_g, ln2_b):
    h = x
    for layer in range(DEPTH):
        h = _layer(h, layer, w_in[layer], b_in[layer], sg_ln_g[layer], sg_ln_b[layer], sg_w[layer],
                   sg_b[layer], w_branch_a[layer], lam_q1[layer], lam_k1[layer], lam_q2[layer],
                   lam_k2[layer], subln_g[layer], w_branch_b[layer], w_out[layer], ln1_g[layer],
                   ln1_b[layer], w_group[layer], b_group[layer], w_expert[layer], b_expert[layer],
                   w_gate[layer], w_up[layer], w_down[layer], ln2_g[layer], ln2_b[layer])
    return h
```
